```python
import jax, jax.numpy as jnp
from jax import lax
import numpy as np

D_MODEL = 1024
BATCH = 8
SEQ = 2048
DEPTH = 4

GRID_W = 64
CTX_LEN = 256
N_MIXERS = 3
Q_BLOCK = 128
ROPE_THETA = 10000.0
EPS = 1e-6
N_MOD = 6
FFN_HIDDEN = -(-(8 * D_MODEL) // (3 * 256)) * 256
CONV_WIDTH = 3
GQA_HEAD_DIM = 128
GQA_HEADS = D_MODEL // GQA_HEAD_DIM
GQA_KV_HEADS = max(GQA_HEADS // 4, 1)
GQA_GROUP = GQA_HEADS // GQA_KV_HEADS
GQA_SCALE = GQA_HEAD_DIM ** -0.5
MLA_HEADS = D_MODEL // 128
MLA_NOPE = 128
MLA_ROPE = 64
MLA_V = 128
MLA_KV_RANK = D_MODEL // 4
MLA_Q_RANK = 3 * MLA_KV_RANK
MLA_SCALE = (MLA_NOPE + MLA_ROPE) ** -0.5
N_A = (DEPTH + 2) // 3
N_B = (DEPTH + 1) // 3
N_C = DEPTH // 3

kernel_name = 'hybrid_diffusion_backbone'


def rmsnorm(x, g):
    x32 = x.astype(jnp.float32)
    y = x32 * lax.rsqrt(jnp.mean(x32 * x32, axis=-1, keepdims=True) + EPS)
    return (y * g.astype(jnp.float32)).astype(x.dtype)


def modulation(cond_act, w, b):
    m = cond_act @ w + b
    return jnp.split(m[:, None, :], N_MOD, axis=-1)


def modulate(x, g, shift, scale):
    return rmsnorm(x, g) * (1 + scale) + shift


def swiglu(h, w1, w3, w2):
    return (jax.nn.silu(h @ w1) * (h @ w3)) @ w2


def axial_angles(rows, cols, rot_dim):
    n = rot_dim // 4
    freqs = ROPE_THETA ** (-jnp.arange(n, dtype=jnp.float32) / n)
    return jnp.concatenate([rows[:, None] * freqs, cols[:, None] * freqs], axis=-1)


def apply_rope(x, ang):
    half = ang.shape[-1]
    shape = (1, ang.shape[0]) + (1,) * (x.ndim - 3) + (half,)
    cos = jnp.cos(ang).reshape(shape)
    sin = jnp.sin(ang).reshape(shape)
    x32 = x.astype(jnp.float32)
    x1, x2 = x32[..., :half], x32[..., half:]
    return jnp.concatenate([x1 * cos - x2 * sin, x1 * sin + x2 * cos], axis=-1).astype(x.dtype)


def attend(q, k, v, scale):
    s = jnp.einsum('bqkgd,btkd->bkgqt', q, k, preferred_element_type=jnp.float32) * scale
    p = jax.nn.softmax(s, axis=-1).astype(v.dtype)
    return jnp.einsum('bkgqt,btkd->bqkgd', p, v)


def blocked_attend(q, k, v, scale):
    B, S = q.shape[0], q.shape[1]
    nb = S // Q_BLOCK
    qb = jnp.moveaxis(q.reshape((B, nb, Q_BLOCK) + q.shape[2:]), 1, 0)
    out = lax.map(lambda qq: attend(qq, k, v, scale), qb)
    out = jnp.moveaxis(out, 0, 1)
    return out.reshape((B, S) + out.shape[3:])


def short_conv_mixer(h, w_in, conv_w, w_out):
    S = h.shape[1]
    b_gate, c_gate, xv = jnp.split(h @ w_in, 3, axis=-1)
    pad = CONV_WIDTH // 2
    u = jnp.pad(c_gate * xv, ((0, 0), (pad, pad), (0, 0)))
    z = u[:, 0:S] * conv_w[0]
    for k in range(1, CONV_WIDTH):
        z = z + u[:, k:k + S] * conv_w[k]
    return (b_gate * z) @ w_out


def gqa_mixer(h_ctx, h_lat, wq, wk, wv, q_norm_g, k_norm_g, wo, ang, ctx_out):
    def proj(h):
        B, S, _ = h.shape
        q = (h @ wq).reshape(B, S, GQA_KV_HEADS, GQA_GROUP, GQA_HEAD_DIM)
        k = (h @ wk).reshape(B, S, GQA_KV_HEADS, GQA_HEAD_DIM)
        v = (h @ wv).reshape(B, S, GQA_KV_HEADS, GQA_HEAD_DIM)
        return rmsnorm(q, q_norm_g), rmsnorm(k, k_norm_g), v
    q_c, k_c, v_c = proj(h_ctx)
    q_l, k_l, v_l = proj(h_lat)
    q_l = apply_rope(q_l, ang)
    k_l = apply_rope(k_l, ang)
    k_all = jnp.concatenate([k_c, k_l], axis=1)
    v_all = jnp.concatenate([v_c, v_l], axis=1)
    B, S = h_lat.shape[0], h_lat.shape[1]
    y_l = blocked_attend(q_l, k_all, v_all, GQA_SCALE).reshape(B, S, GQA_HEADS * GQA_HEAD_DIM) @ wo
    y_c = None
    if ctx_out:
        L = h_ctx.shape[1]
        y_c = attend(q_c, k_c, v_c, GQA_SCALE).reshape(B, L, GQA_HEADS * GQA_HEAD_DIM) @ wo
    return y_c, y_l


def mla_project(h, w_dq, q_norm_g, w_uq, w_dkv, kv_norm_g, w_ukv, ang):
    B, S, _ = h.shape
    cq = rmsnorm(h @ w_dq, q_norm_g)
    q = (cq @ w_uq).reshape(B, S, MLA_HEADS, MLA_NOPE + MLA_ROPE)
    q_nope, q_pe = q[..., :MLA_NOPE], q[..., MLA_NOPE:]
    ckv_pe = h @ w_dkv
    ckv = rmsnorm(ckv_pe[..., :MLA_KV_RANK], kv_norm_g)
    k_pe = ckv_pe[..., None, MLA_KV_RANK:]
    kv = (ckv @ w_ukv).reshape(B, S, MLA_HEADS, MLA_NOPE + MLA_V)
    k_nope, v = kv[..., :MLA_NOPE], kv[..., MLA_NOPE:]
    if ang is not None:
        q_pe = apply_rope(q_pe, ang)
        k_pe = apply_rope(k_pe, ang)
    q = jnp.concatenate([q_nope, q_pe], axis=-1)[:, :, :, None, :]
    k = jnp.concatenate([k_nope, jnp.broadcast_to(k_pe, (B, S, MLA_HEADS, MLA_ROPE))], axis=-1)
    return q, k, v


def mla_mixer(h_ctx, h_lat, w_dq, q_norm_g, w_uq, w_dkv, kv_norm_g, w_ukv, wo, ang, ctx_out):
    q_c, k_c, v_c = mla_project(h_ctx, w_dq, q_norm_g, w_uq, w_dkv, kv_norm_g, w_ukv, None)
    q_l, k_l, v_l = mla_project(h_lat, w_dq, q_norm_g, w_uq, w_dkv, kv_norm_g, w_ukv, ang)
    k_all = jnp.concatenate([k_c, k_l], axis=1)
    v_all = jnp.concatenate([v_c, v_l], axis=1)
    B, S = h_lat.shape[0], h_lat.shape[1]
    y_l = blocked_attend(q_l, k_all, v_all, MLA_SCALE).reshape(B, S, MLA_HEADS * MLA_V) @ wo
    y_c = None
    if ctx_out:
        L = h_ctx.shape[1]
        y_c = attend(q_c, k_c, v_c, MLA_SCALE).reshape(B, L, MLA_HEADS * MLA_V) @ wo
    return y_c, y_l


def setup_inputs(seed: int = 0) -> dict:
    key = jax.random.key(seed)
    ks = iter(jax.random.split(key, 40))
    D = D_MODEL
    f32 = jnp.float32

    def w(shape, fan_in, mult=1.0):
        return jax.random.normal(next(ks), shape, f32) * (mult * fan_in ** -0.5)

    def gain(shape):
        return 1.0 + 0.02 * jax.random.normal(next(ks), shape, f32)

    return {
        'x': jax.random.normal(next(ks), (BATCH, SEQ, D), f32),
        'c': jax.random.normal(next(ks), (BATCH, D), f32),
        'ctx': jax.random.normal(next(ks), (BATCH, CTX_LEN, D), f32),
        'c_ctx': jax.random.normal(next(ks), (D,), f32),
        'ada_w': w((DEPTH, D, N_MOD * D), D, 0.5),
        'ada_b': 0.02 * jax.random.normal(next(ks), (DEPTH, N_MOD * D), f32),
        'norm1_g': gain((DEPTH, D)),
        'norm2_g': gain((DEPTH, D)),
        'ffn_w1': w((DEPTH, D, FFN_HIDDEN), D),
        'ffn_w3': w((DEPTH, D, FFN_HIDDEN), D),
        'ffn_w2': w((DEPTH, FFN_HIDDEN, D), FFN_HIDDEN),
        'conv_w_in': w((N_A, D, 3 * D), D),
        'conv_w': w((N_A, CONV_WIDTH, D), CONV_WIDTH),
        'conv_w_out': w((N_A, D, D), D),
        'gqa_wq': w((N_B, D, GQA_HEADS * GQA_HEAD_DIM), D),
        'gqa_wk': w((N_B, D, GQA_KV_HEADS * GQA_HEAD_DIM), D),
        'gqa_wv': w((N_B, D, GQA_KV_HEADS * GQA_HEAD_DIM), D),
        'gqa_q_norm': gain((N_B, GQA_HEAD_DIM)),
        'gqa_k_norm': gain((N_B, GQA_HEAD_DIM)),
        'gqa_wo': w((N_B, GQA_HEADS * GQA_HEAD_DIM, D), GQA_HEADS * GQA_HEAD_DIM),
        'mla_w_dq': w((N_C, D, MLA_Q_RANK), D),
        'mla_q_norm': gain((N_C, MLA_Q_RANK)),
        'mla_w_uq': w((N_C, MLA_Q_RANK, MLA_HEADS * (MLA_NOPE + MLA_ROPE)), MLA_Q_RANK),
        'mla_w_dkv': w((N_C, D, MLA_KV_RANK + MLA_ROPE), D),
        'mla_kv_norm': gain((N_C, MLA_KV_RANK)),
        'mla_w_ukv': w((N_C, MLA_KV_RANK, MLA_HEADS * (MLA_NOPE + MLA_V)), MLA_KV_RANK),
        'mla_wo': w((N_C, MLA_HEADS * MLA_V, D), MLA_HEADS * MLA_V),
        'final_g': gain((D,)),
    }


def reference(x, c, ctx, c_ctx, ada_w, ada_b, norm1_g, norm2_g, ffn_w1, ffn_w3, ffn_w2,
              conv_w_in, conv_w, conv_w_out,
              gqa_wq, gqa_wk, gqa_wv, gqa_q_norm, gqa_k_norm, gqa_wo,
              mla_w_dq, mla_q_norm, mla_w_uq, mla_w_dkv, mla_kv_norm, mla_w_ukv, mla_wo,
              final_g):
    S = x.shape[1]
    ROWS = S // GRID_W
    rows = jnp.repeat(jnp.arange(ROWS, dtype=jnp.float32), GRID_W)
    cols = jnp.tile(jnp.arange(GRID_W, dtype=jnp.float32), ROWS)
    ang_gqa = axial_angles(rows, cols, GQA_HEAD_DIM)
    ang_mla = axial_angles(rows, cols, MLA_ROPE)

    cond_lat = jax.nn.silu(c)
    cond_ctx = jax.nn.silu(c_ctx)[None]

    for i in range(DEPTH):
        kind = i % N_MIXERS
        j = i // N_MIXERS
        ctx_out = i < DEPTH - 1
        sh1, sc1, g1, sh2, sc2, g2 = modulation(cond_lat, ada_w[i], ada_b[i])
        csh1, csc1, cg1, csh2, csc2, cg2 = modulation(cond_ctx, ada_w[i], ada_b[i])

        h_lat = modulate(x, norm1_g[i], sh1, sc1)
        h_ctx = modulate(ctx, norm1_g[i], csh1, csc1) if (ctx_out or kind != 0) else None

        if kind == 0:
            y_lat = short_conv_mixer(h_lat, conv_w_in[j], conv_w[j], conv_w_out[j])
            y_ctx = short_conv_mixer(h_ctx, conv_w_in[j], conv_w[j], conv_w_out[j]) if ctx_out else None
        elif kind == 1:
            y_ctx, y_lat = gqa_mixer(h_ctx, h_lat, gqa_wq[j], gqa_wk[j], gqa_wv[j],
                                     gqa_q_norm[j], gqa_k_norm[j], gqa_wo[j], ang_gqa, ctx_out)
        else:
            y_ctx, y_lat = mla_mixer(h_ctx, h_lat, mla_w_dq[j], mla_q_norm[j], mla_w_uq[j],
                                     mla_w_dkv[j], mla_kv_norm[j], mla_w_ukv[j], mla_wo[j],
                                     ang_mla, ctx_out)

        x = x + g1 * y_lat
        x = x + g2 * swiglu(modulate(x, norm2_g[i], sh2, sc2), ffn_w1[i], ffn_w3[i], ffn_w2[i])
        if ctx_out:
            ctx = ctx + cg1 * y_ctx
            ctx = ctx + cg2 * swiglu(modulate(ctx, norm2_g[i], csh2, csc2), ffn_w1[i], ffn_w3[i], ffn_w2[i])

    return rmsnorm(x, final_g)
```

```python
import functools

import jax
import jax.numpy as jnp
import numpy as np
from jax import lax
from jax.experimental import pallas as pl
from jax.experimental.pallas import tpu as pltpu

D = 1024
BATCH = 8
SEQ = 2048
DEPTH = 4
GRID_W = 64
CTX_LEN = 256
N_MOD = 6
FFN_HIDDEN = 2816
ROPE_THETA = 10000.0
EPS = 1e-6
GQA_HEAD_DIM = 128
GQA_HEADS = 8
GQA_KV_HEADS = 2
GQA_GROUP = 4
MLA_HEADS = 8
MLA_NOPE = 128
MLA_ROPE = 64
MLA_V = 128
MLA_KV_RANK = 256
MLA_Q_RANK = 768
MLA_QK = 256
LANES = 128

N_LAT = BATCH * SEQ
N_CTX = BATCH * CTX_LEN
N_ALL = N_LAT + N_CTX
MOD_ROWS = 16

TM = 512
TQ = 256
HALO = 8
VMEM_LIMIT = 56 * 1024 * 1024
LOG2E = 1.4426950408889634

F32 = jnp.float32
BF16 = jnp.bfloat16


def _params(n_axes):
    return pltpu.CompilerParams(
        dimension_semantics=("arbitrary",) * n_axes, vmem_limit_bytes=VMEM_LIMIT)


def _resident(shape):
    nd = len(shape)
    return pl.BlockSpec(shape, lambda *_: (0,) * nd, pipeline_mode=pl.Buffered(1))


def _mod_spec(tm):
    return pl.BlockSpec((1, N_MOD, D), lambda i: (jnp.minimum(i * tm // SEQ, BATCH), 0, 0))


def _row_spec(tm, width):
    return pl.BlockSpec((tm, width), lambda i: (i, 0))


def _dot(a, b):
    return jnp.dot(a, b, preferred_element_type=F32)


def _modulated_norm(x, gain, shift, scale):
    r = lax.rsqrt(jnp.mean(x * x, axis=-1, keepdims=True) + EPS)
    return (x * r) * (gain * (1.0 + scale)) + shift


def _head_norm(x, gain):
    r = lax.rsqrt(jnp.mean(x * x, axis=-1, keepdims=True) + EPS)
    return (x * r) * gain


def _rope(x, cos, sin):
    return x * cos + pltpu.roll(x, LANES // 2, axis=1) * sin


def _mod_kernel(cond_ref, w_ref, b_ref, o_ref):
    c = cond_ref[...]
    act = (c * jax.nn.sigmoid(c)).astype(BF16)
    o_ref[0] = _dot(act, w_ref[0].astype(BF16)) + b_ref[0]


def _modulation_tables(cond, ada_w, ada_b):
    tn = 1536
    return pl.pallas_call(
        _mod_kernel,
        grid=(DEPTH, N_MOD * D // tn),
        in_specs=[
            pl.BlockSpec((MOD_ROWS, D), lambda l, j: (0, 0)),
            pl.BlockSpec((1, D, tn), lambda l, j: (l, 0, j)),
            pl.BlockSpec((1, 1, tn), lambda l, j: (l, 0, j)),
        ],
        out_specs=pl.BlockSpec((1, MOD_ROWS, tn), lambda l, j: (l, 0, j)),
        out_shape=jax.ShapeDtypeStruct((DEPTH, MOD_ROWS, N_MOD * D), F32),
        compiler_params=_params(2),
        name="modulation",
    )(cond, ada_w, ada_b.reshape(DEPTH, 1, N_MOD * D))


def _conv_in_kernel(x_ref, mod_ref, g_ref, w_ref, b_ref, u_ref):
    m = mod_ref[0]
    h = _modulated_norm(x_ref[...], g_ref[...], m[0:1], m[1:2]).astype(BF16)
    y = _dot(h, w_ref[...])
    b_ref[...] = y[:, :D].astype(BF16)
    u_ref[...] = y[:, D:2 * D] * y[:, 2 * D:]


def _conv_in(x, mod, gain, w_in, n_rows):
    return pl.pallas_call(
        _conv_in_kernel,
        grid=(n_rows // TM,),
        in_specs=[_row_spec(TM, D), _mod_spec(TM), _resident((1, D)), _resident((D, 3 * D))],
        out_specs=[_row_spec(TM, D), _row_spec(TM, D)],
        out_shape=[jax.ShapeDtypeStruct((n_rows, D), BF16), jax.ShapeDtypeStruct((n_rows, D), F32)],
        compiler_params=_params(1),
        name="conv_in",
    )(x, mod, gain, w_in)


def _conv_out_kernel(b_ref, u_ref, up_ref, un_ref, cw_ref, w_ref, x_ref, mod_ref, o_ref):
    tm = u_ref.shape[0]
    u = u_ref[...]
    local = lax.broadcasted_iota(jnp.int32, (tm, 1), 0)
    row = local + pl.program_id(0) * tm
    in_ctx = row >= N_LAT
    first = ((row & (CTX_LEN - 1)) == 0) & (in_ctx | ((row & (SEQ - 1)) == 0))
    last = ((row & (CTX_LEN - 1)) == CTX_LEN - 1) & (in_ctx | ((row & (SEQ - 1)) == SEQ - 1))
    prev = jnp.where(local == 0, up_ref[HALO - 1:HALO, :], pltpu.roll(u, 1, axis=0))
    prev = jnp.where(first, 0.0, prev)
    nxt = jnp.where(local == tm - 1, un_ref[0:1, :], pltpu.roll(u, tm - 1, axis=0))
    nxt = jnp.where(last, 0.0, nxt)
    cw = cw_ref[...]
    z = prev * cw[0:1] + u * cw[1:2] + nxt * cw[2:3]
    bz = (b_ref[...].astype(F32) * z).astype(BF16)
    o_ref[...] = x_ref[...] + mod_ref[0][2:3] * _dot(bz, w_ref[...])


def _conv_out(b, u, conv_w, w_out, x, mod, n_rows):
    per = TM // HALO
    n_halo = n_rows // HALO
    return pl.pallas_call(
        _conv_out_kernel,
        grid=(n_rows // TM,),
        in_specs=[
            _row_spec(TM, D), _row_spec(TM, D),
            pl.BlockSpec((HALO, D), lambda i: (jnp.maximum(i * per - 1, 0), 0)),
            pl.BlockSpec((HALO, D), lambda i: (jnp.minimum((i + 1) * per, n_halo - 1), 0)),
            _resident((3, D)), _resident((D, D)), _row_spec(TM, D), _mod_spec(TM),
        ],
        out_specs=_row_spec(TM, D),
        out_shape=jax.ShapeDtypeStruct((n_rows, D), F32),
        compiler_params=_params(1),
        name="conv_out",
    )(b, u, u, u, conv_w, w_out, x, mod)


def _gqa_proj_kernel(x_ref, mod_ref, g_ref, w_ref, qg_ref, kg_ref, cos_ref, sin_ref,
                     q_ref, k_ref, v_ref):
    m = mod_ref[0]
    h = _modulated_norm(x_ref[...], g_ref[...], m[0:1], m[1:2]).astype(BF16)
    y = _dot(h, w_ref[...])
    cos = cos_ref[...]
    sin = sin_ref[...]
    qg = qg_ref[...]
    kg = kg_ref[...]
    hd = GQA_HEAD_DIM
    for j in range(GQA_HEADS):
        q = _head_norm(y[:, j * hd:(j + 1) * hd], qg)
        q_ref[:, j * hd:(j + 1) * hd] = _rope(q, cos, sin).astype(BF16)
    k0 = GQA_HEADS * hd
    for j in range(GQA_KV_HEADS):
        k = _head_norm(y[:, k0 + j * hd:k0 + (j + 1) * hd], kg)
        k_ref[:, j * hd:(j + 1) * hd] = _rope(k, cos, sin).astype(BF16)
    v0 = k0 + GQA_KV_HEADS * hd
    v_ref[...] = y[:, v0:].astype(BF16)


def _rope_spec(tm):
    per_seq = SEQ // tm
    return pl.BlockSpec(
        (tm, LANES), lambda i: (jnp.where(i < N_LAT // tm, i % per_seq, per_seq), 0))


def _gqa_proj(x, mod, gain, w_qkv, q_gain, k_gain, cos, sin):
    kvw = GQA_KV_HEADS * GQA_HEAD_DIM
    return pl.pallas_call(
        _gqa_proj_kernel,
        grid=(N_ALL // TM,),
        in_specs=[
            _row_spec(TM, D), _mod_spec(TM), _resident((1, D)), _resident((D, D + 2 * kvw)),
            _resident((1, GQA_HEAD_DIM)), _resident((1, GQA_HEAD_DIM)),
            _rope_spec(TM), _rope_spec(TM),
        ],
        out_specs=[_row_spec(TM, D), _row_spec(TM, kvw), _row_spec(TM, kvw)],
        out_shape=[jax.ShapeDtypeStruct((N_ALL, D), BF16),
                   jax.ShapeDtypeStruct((N_ALL, kvw), BF16),
                   jax.ShapeDtypeStruct((N_ALL, kvw), BF16)],
        compiler_params=_params(1),
        name="gqa_proj",
    )(x, mod, gain, w_qkv, q_gain, k_gain, cos, sin)


def _mla_proj_kernel(x_ref, mod_ref, g_ref, wdq_ref, qg_ref, wuq_ref, wdkv_ref, kvg_ref, wukv_ref,
                     cos_ref, sin_ref, q_ref, k_ref, v_ref):
    m = mod_ref[0]
    h = _modulated_norm(x_ref[...], g_ref[...], m[0:1], m[1:2]).astype(BF16)
    cos = cos_ref[...]
    sin = sin_ref[...]
    cq = _head_norm(_dot(h, wdq_ref[...]), qg_ref[...]).astype(BF16)
    q = _dot(cq, wuq_ref[...])
    ckv_pe = _dot(h, wdkv_ref[...])
    ckv = _head_norm(ckv_pe[:, :MLA_KV_RANK], kvg_ref[...]).astype(BF16)
    k_pe = _rope(ckv_pe[:, MLA_KV_RANK:], cos, sin).astype(BF16)
    kv = _dot(ckv, wukv_ref[...])
    for j in range(MLA_HEADS):
        c0 = j * MLA_QK
        q_ref[:, c0:c0 + MLA_NOPE] = q[:, c0:c0 + MLA_NOPE].astype(BF16)
        q_ref[:, c0 + MLA_NOPE:c0 + MLA_QK] = _rope(q[:, c0 + MLA_NOPE:c0 + MLA_QK], cos, sin).astype(BF16)
        k_ref[:, c0:c0 + MLA_NOPE] = kv[:, j * MLA_NOPE:(j + 1) * MLA_NOPE].astype(BF16)
        k_ref[:, c0 + MLA_NOPE:c0 + MLA_QK] = k_pe
    v_ref[...] = kv[:, MLA_HEADS * MLA_NOPE:].astype(BF16)


def _mla_proj(x, mod, gain, w_dq, q_gain, w_uq, w_dkv, kv_gain, w_ukv, cos, sin):
    qkw = MLA_HEADS * MLA_QK
    vw = MLA_HEADS * MLA_V
    return pl.pallas_call(
        _mla_proj_kernel,
        grid=(N_ALL // TM,),
        in_specs=[
            _row_spec(TM, D), _mod_spec(TM), _resident((1, D)),
            _resident((D, MLA_Q_RANK)), _resident((1, MLA_Q_RANK)), _resident((MLA_Q_RANK, qkw)),
            _resident((D, MLA_KV_RANK + LANES)), _resident((1, MLA_KV_RANK)),
            _resident((MLA_KV_RANK, MLA_HEADS * (MLA_NOPE + MLA_V))),
            _rope_spec(TM), _rope_spec(TM),
        ],
        out_specs=[_row_spec(TM, qkw), _row_spec(TM, qkw), _row_spec(TM, vw)],
        out_shape=[jax.ShapeDtypeStruct((N_ALL, qkw), BF16),
                   jax.ShapeDtypeStruct((N_ALL, qkw), BF16),
                   jax.ShapeDtypeStruct((N_ALL, vw), BF16)],
        compiler_params=_params(1),
        name="mla_proj",
    )(x, mod, gain, w_dq, q_gain, w_uq, w_dkv, kv_gain, w_ukv, cos, sin)


def _nt_dot(a, b):
    return lax.dot_general(a, b, (((1,), (1,)), ((), ())), preferred_element_type=F32)


def _attn_kernel(q_ref, kl_ref, kc_ref, vl_ref, vc_ref, o_ref, *, heads, group, dq, dv):
    is_lat = pl.program_id(2) < SEQ // TQ

    def one_head(kv, g, with_lat):
        qh = kv * group + g
        q = q_ref[:, qh * dq:(qh + 1) * dq]
        kc = kc_ref[:, kv * dq:(kv + 1) * dq]
        vc = vc_ref[:, kv * dv:(kv + 1) * dv]
        sc = _nt_dot(q, kc)
        mx = jnp.max(sc, axis=-1, keepdims=True)
        if with_lat:
            sl = _nt_dot(q, kl_ref[:, kv * dq:(kv + 1) * dq])
            mx = jnp.maximum(mx, jnp.max(sl, axis=-1, keepdims=True))
            pl_ = jnp.exp2(sl - mx)
        pc = jnp.exp2(sc - mx)
        den = jnp.sum(pc, axis=-1, keepdims=True)
        acc = _dot(pc.astype(BF16), vc)
        if with_lat:
            den = den + jnp.sum(pl_, axis=-1, keepdims=True)
            acc = acc + _dot(pl_.astype(BF16), vl_ref[:, kv * dv:(kv + 1) * dv])
        o_ref[:, qh * dv:(qh + 1) * dv] = (acc / den).astype(BF16)

    @pl.when(is_lat)
    def _():
        for kv in range(heads):
            for g in range(group):
                one_head(kv, g, True)

    @pl.when(jnp.logical_not(is_lat))
    def _():
        for kv in range(heads):
            for g in range(group):
                one_head(kv, g, False)


def _attention(q, k, v, *, kv_heads, group, dq, dv, heads_per_step):
    hp = heads_per_step
    lat_tiles = SEQ // TQ
    ctx_block0 = N_LAT // CTX_LEN

    def q_map(b, h, t):
        return (jnp.where(t < lat_tiles, b * lat_tiles + t, N_LAT // TQ + b), h)

    kern = functools.partial(_attn_kernel, heads=hp, group=group, dq=dq, dv=dv)
    return pl.pallas_call(
        kern,
        grid=(BATCH, kv_heads // hp, lat_tiles + 1),
        in_specs=[
            pl.BlockSpec((TQ, hp * group * dq), q_map),
            pl.BlockSpec((SEQ, hp * dq), lambda b, h, t: (b, h)),
            pl.BlockSpec((CTX_LEN, hp * dq), lambda b, h, t: (ctx_block0 + b, h)),
            pl.BlockSpec((SEQ, hp * dv), lambda b, h, t: (b, h)),
            pl.BlockSpec((CTX_LEN, hp * dv), lambda b, h, t: (ctx_block0 + b, h)),
        ],
        out_specs=pl.BlockSpec((TQ, hp * group * dv), q_map),
        out_shape=jax.ShapeDtypeStruct((N_ALL, kv_heads * group * dv), BF16),
        compiler_params=_params(3),
        name="attention",
    )(q, k, k, v, v)


def _attn_out_kernel(a_ref, w_ref, x_ref, mod_ref, o_ref):
    o_ref[...] = x_ref[...] + mod_ref[0][2:3] * _dot(a_ref[...], w_ref[...])


def _attn_out(a, w_o, x, mod):
    return pl.pallas_call(
        _attn_out_kernel,
        grid=(N_ALL // TM,),
        in_specs=[_row_spec(TM, D), _resident((D, D)), _row_spec(TM, D), _mod_spec(TM)],
        out_specs=_row_spec(TM, D),
        out_shape=jax.ShapeDtypeStruct((N_ALL, D), F32),
        compiler_params=_params(1),
        name="attn_out",
    )(a, w_o, x, mod)


def _ffn_kernel(x_ref, mod_ref, g_ref, w1_ref, w3_ref, w2_ref, o_ref):
    m = mod_ref[0]
    x = x_ref[...]
    h = _modulated_norm(x, g_ref[...], m[3:4], m[4:5]).astype(BF16)
    a = _dot(h, w1_ref[...])
    b = _dot(h, w3_ref[...])
    gated = (a * jax.nn.sigmoid(a) * b).astype(BF16)
    o_ref[...] = x + m[5:6] * _dot(gated, w2_ref[...])


def _ffn(x, mod, gain, w1, w3, w2, n_rows):
    return pl.pallas_call(
        _ffn_kernel,
        grid=(n_rows // TM,),
        in_specs=[
            _row_spec(TM, D), _mod_spec(TM), _resident((1, D)),
            _resident((D, FFN_HIDDEN)), _resident((D, FFN_HIDDEN)), _resident((FFN_HIDDEN, D)),
        ],
        out_specs=_row_spec(TM, D),
        out_shape=jax.ShapeDtypeStruct((n_rows, D), F32),
        compiler_params=_params(1),
        name="ffn",
    )(x, mod, gain, w1, w3, w2)


def _final_norm_kernel(x_ref, g_ref, o_ref):
    o_ref[...] = _head_norm(x_ref[...], g_ref[...])


def _final_norm(x, gain):
    return pl.pallas_call(
        _final_norm_kernel,
        grid=(N_LAT // TM,),
        in_specs=[_row_spec(TM, D), _resident((1, D))],
        out_specs=_row_spec(TM, D),
        out_shape=jax.ShapeDtypeStruct((N_LAT, D), F32),
        compiler_params=_params(1),
        name="final_norm",
    )(x, gain)


def _axial_angles(rot_dim):
    n = rot_dim // 4
    rows = jnp.repeat(jnp.arange(SEQ // GRID_W, dtype=F32), GRID_W)
    cols = jnp.tile(jnp.arange(GRID_W, dtype=F32), SEQ // GRID_W)
    freqs = ROPE_THETA ** (-jnp.arange(n, dtype=F32) / n)
    return jnp.concatenate([rows[:, None] * freqs, cols[:, None] * freqs], axis=-1)


def _rope_tables(rot_dim):
    ang = _axial_angles(rot_dim)
    cos, sin = jnp.cos(ang), jnp.sin(ang)
    pad = jnp.zeros((SEQ, (LANES - rot_dim) // 2), F32)
    cos_t = jnp.concatenate([cos, pad, cos, pad], axis=-1)
    sin_t = jnp.concatenate([-sin, pad, sin, pad], axis=-1)
    cos_t = jnp.concatenate([cos_t, jnp.ones((TM, LANES), F32)], axis=0)
    sin_t = jnp.concatenate([sin_t, jnp.zeros((TM, LANES), F32)], axis=0)
    return cos_t, sin_t


def _rope_slab_cols(w):
    half = MLA_ROPE // 2
    z = jnp.zeros((w.shape[0], LANES // 2 - half), w.dtype)
    return jnp.concatenate([w[:, :half], z, w[:, half:], z], axis=-1)


def _mla_weights(w_dq, w_uq, w_dkv, w_ukv):
    qk = MLA_NOPE + MLA_ROPE
    uq = w_uq.reshape(MLA_Q_RANK, MLA_HEADS, qk)
    uq = jnp.concatenate(
        [jnp.concatenate([uq[:, j, :MLA_NOPE], _rope_slab_cols(uq[:, j, MLA_NOPE:])], axis=-1)
         for j in range(MLA_HEADS)], axis=-1)
    dkv = jnp.concatenate([w_dkv[:, :MLA_KV_RANK], _rope_slab_cols(w_dkv[:, MLA_KV_RANK:])], axis=-1)
    ukv = w_ukv.reshape(MLA_KV_RANK, MLA_HEADS, MLA_NOPE + MLA_V)
    ukv = jnp.concatenate([ukv[:, :, :MLA_NOPE].reshape(MLA_KV_RANK, -1),
                           ukv[:, :, MLA_NOPE:].reshape(MLA_KV_RANK, -1)], axis=-1)
    return w_dq.astype(BF16), uq.astype(BF16), dkv.astype(BF16), ukv.astype(BF16)


def kernel(x, c, ctx, c_ctx, ada_w, ada_b, norm1_g, norm2_g, ffn_w1, ffn_w3, ffn_w2, conv_w_in, conv_w, conv_w_out, gqa_wq, gqa_wk, gqa_wv, gqa_q_norm, gqa_k_norm, gqa_wo, mla_w_dq, mla_q_norm, mla_w_uq, mla_w_dkv, mla_kv_norm, mla_w_ukv, mla_wo, final_g):
    assert x.shape == (BATCH, SEQ, D) and ctx.shape == (BATCH, CTX_LEN, D)
    xs = jnp.concatenate([x.reshape(N_LAT, D), ctx.reshape(N_CTX, D)], axis=0)

    cond = jnp.concatenate(
        [c, c_ctx[None], jnp.zeros((MOD_ROWS - BATCH - 1, D), F32)], axis=0)
    mods = _modulation_tables(cond, ada_w, ada_b).reshape(DEPTH, MOD_ROWS, N_MOD, D)

    gqa_cos, gqa_sin = _rope_tables(GQA_HEAD_DIM)
    mla_cos, mla_sin = _rope_tables(MLA_ROPE)

    for i in range(DEPTH):
        kind, j = i % 3, i // 3
        mod = mods[i]
        n_rows = N_ALL if i < DEPTH - 1 else N_LAT
        g1 = norm1_g[i][None]
        if kind == 0:
            b, u = _conv_in(xs, mod, g1, conv_w_in[j].astype(BF16), n_rows)
            xs = _conv_out(b, u, conv_w[j], conv_w_out[j].astype(BF16), xs, mod, n_rows)
        elif kind == 1:
            w_qkv = jnp.concatenate([gqa_wq[j], gqa_wk[j], gqa_wv[j]], axis=-1).astype(BF16)
            q_gain = gqa_q_norm[j][None] * (GQA_HEAD_DIM ** -0.5 * LOG2E)
            q, k, v = _gqa_proj(xs, mod, g1, w_qkv, q_gain, gqa_k_norm[j][None], gqa_cos, gqa_sin)
            a = _attention(q, k, v, kv_heads=GQA_KV_HEADS, group=GQA_GROUP,
                           dq=GQA_HEAD_DIM, dv=GQA_HEAD_DIM, heads_per_step=1)
            xs = _attn_out(a, gqa_wo[j].astype(BF16), xs, mod)
        else:
            w_dq, w_uq, w_dkv, w_ukv = _mla_weights(mla_w_dq[j], mla_w_uq[j], mla_w_dkv[j], mla_w_ukv[j])
            q_gain = mla_q_norm[j][None] * ((MLA_NOPE + MLA_ROPE) ** -0.5 * LOG2E)
            q, k, v = _mla_proj(xs, mod, g1, w_dq, q_gain, w_uq, w_dkv,
                                mla_kv_norm[j][None], w_ukv, mla_cos, mla_sin)
            a = _attention(q, k, v, kv_heads=MLA_HEADS, group=1, dq=MLA_QK, dv=MLA_V, heads_per_step=4)
            xs = _attn_out(a, mla_wo[j].astype(BF16), xs, mod)
        xs = _ffn(xs, mod, norm2_g[i][None], ffn_w1[i].astype(BF16), ffn_w3[i].astype(BF16),
                  ffn_w2[i].astype(BF16), n_rows)

    return _final_norm(xs, final_g[None]).reshape(BATCH, SEQ, D)
```

```python
import functools

import jax
import jax.numpy as jnp
import numpy as np
from jax import lax
from jax.experimental import pallas as pl
from jax.experimental.pallas import tpu as pltpu

D = 1024
BATCH = 8
SEQ = 2048
DEPTH = 4
GRID_W = 64
CTX_LEN = 256
N_MOD = 6
FFN_HIDDEN = 2816
ROPE_THETA = 10000.0
EPS = 1e-6
GQA_HEAD_DIM = 128
GQA_HEADS = 8
GQA_KV_HEADS = 2
GQA_GROUP = 4
MLA_HEADS = 8
MLA_NOPE = 128
MLA_ROPE = 64
MLA_V = 128
MLA_KV_RANK = 256
MLA_Q_RANK = 768
MLA_QK = 256
LANES = 128

N_LAT = BATCH * SEQ
N_CTX = BATCH * CTX_LEN
N_ALL = N_LAT + N_CTX
MOD_ROWS = 16

TM = 512
TQ = 512
HALO = 8
VMEM_LIMIT = 56 * 1024 * 1024
LOG2E = 1.4426950408889634

F32 = jnp.float32
BF16 = jnp.bfloat16


def _params(n_axes):
    return pltpu.CompilerParams(
        dimension_semantics=("arbitrary",) * n_axes, vmem_limit_bytes=VMEM_LIMIT)


def _resident(shape):
    nd = len(shape)
    return pl.BlockSpec(shape, lambda *_: (0,) * nd, pipeline_mode=pl.Buffered(1))


def _mod_spec(tm):
    return pl.BlockSpec((1, N_MOD, D), lambda i: (jnp.minimum(i * tm // SEQ, BATCH), 0, 0))


def _row_spec(tm, width):
    return pl.BlockSpec((tm, width), lambda i: (i, 0))


def _col_spec(height, tm):
    return pl.BlockSpec((height, tm), lambda i: (0, i))


def _dot(a, b):
    return jnp.dot(a, b, preferred_element_type=F32)


def _modulated_norm(x, gain, shift, scale):
    r = lax.rsqrt(jnp.mean(x * x, axis=-1, keepdims=True) + EPS)
    return (x * r) * (gain * (1.0 + scale)) + shift


def _head_norm(x, gain):
    r = lax.rsqrt(jnp.mean(x * x, axis=-1, keepdims=True) + EPS)
    return (x * r) * gain


def _rope(x, cos, sin):
    return x * cos + pltpu.roll(x, LANES // 2, axis=1) * sin


def _mod_kernel(cond_ref, w_ref, b_ref, o_ref):
    c = cond_ref[...]
    act = (c * jax.nn.sigmoid(c)).astype(BF16)
    o_ref[0] = _dot(act, w_ref[0].astype(BF16)) + b_ref[0]


def _modulation_tables(cond, ada_w, ada_b):
    tn = 1536
    return pl.pallas_call(
        _mod_kernel,
        grid=(DEPTH, N_MOD * D // tn),
        in_specs=[
            pl.BlockSpec((MOD_ROWS, D), lambda l, j: (0, 0)),
            pl.BlockSpec((1, D, tn), lambda l, j: (l, 0, j)),
            pl.BlockSpec((1, 1, tn), lambda l, j: (l, 0, j)),
        ],
        out_specs=pl.BlockSpec((1, MOD_ROWS, tn), lambda l, j: (l, 0, j)),
        out_shape=jax.ShapeDtypeStruct((DEPTH, MOD_ROWS, N_MOD * D), F32),
        compiler_params=_params(2),
        name="modulation",
    )(cond, ada_w, ada_b.reshape(DEPTH, 1, N_MOD * D))


def _conv_in_kernel(x_ref, mod_ref, g_ref, w_ref, b_ref, u_ref):
    m = mod_ref[0]
    h = _modulated_norm(x_ref[...], g_ref[...], m[0:1], m[1:2]).astype(BF16)
    y = _dot(h, w_ref[...])
    b_ref[...] = y[:, :D].astype(BF16)
    u_ref[...] = y[:, D:2 * D] * y[:, 2 * D:]


def _conv_in(x, mod, gain, w_in, n_rows):
    return pl.pallas_call(
        _conv_in_kernel,
        grid=(n_rows // TM,),
        in_specs=[_row_spec(TM, D), _mod_spec(TM), _resident((1, D)), _resident((D, 3 * D))],
        out_specs=[_row_spec(TM, D), _row_spec(TM, D)],
        out_shape=[jax.ShapeDtypeStruct((n_rows, D), BF16), jax.ShapeDtypeStruct((n_rows, D), F32)],
        compiler_params=_params(1),
        name="conv_in",
    )(x, mod, gain, w_in)


def _conv_out_kernel(b_ref, u_ref, up_ref, un_ref, cw_ref, w_ref, x_ref, mod_ref, o_ref):
    tm = u_ref.shape[0]
    u = u_ref[...]
    local = lax.broadcasted_iota(jnp.int32, (tm, 1), 0)
    row = local + pl.program_id(0) * tm
    in_ctx = row >= N_LAT
    first = ((row & (CTX_LEN - 1)) == 0) & (in_ctx | ((row & (SEQ - 1)) == 0))
    last = ((row & (CTX_LEN - 1)) == CTX_LEN - 1) & (in_ctx | ((row & (SEQ - 1)) == SEQ - 1))
    prev = jnp.where(local == 0, up_ref[HALO - 1:HALO, :], pltpu.roll(u, 1, axis=0))
    prev = jnp.where(first, 0.0, prev)
    nxt = jnp.where(local == tm - 1, un_ref[0:1, :], pltpu.roll(u, tm - 1, axis=0))
    nxt = jnp.where(last, 0.0, nxt)
    cw = cw_ref[...]
    z = prev * cw[0:1] + u * cw[1:2] + nxt * cw[2:3]
    bz = (b_ref[...].astype(F32) * z).astype(BF16)
    o_ref[...] = x_ref[...] + mod_ref[0][2:3] * _dot(bz, w_ref[...])


def _conv_out(b, u, conv_w, w_out, x, mod, n_rows):
    per = TM // HALO
    n_halo = n_rows // HALO
    return pl.pallas_call(
        _conv_out_kernel,
        grid=(n_rows // TM,),
        in_specs=[
            _row_spec(TM, D), _row_spec(TM, D),
            pl.BlockSpec((HALO, D), lambda i: (jnp.maximum(i * per - 1, 0), 0)),
            pl.BlockSpec((HALO, D), lambda i: (jnp.minimum((i + 1) * per, n_halo - 1), 0)),
            _resident((3, D)), _resident((D, D)), _row_spec(TM, D), _mod_spec(TM),
        ],
        out_specs=_row_spec(TM, D),
        out_shape=jax.ShapeDtypeStruct((n_rows, D), F32),
        compiler_params=_params(1),
        name="conv_out",
    )(b, u, u, u, conv_w, w_out, x, mod)


def _gqa_proj_kernel(x_ref, mod_ref, g_ref, w_ref, qg_ref, kg_ref, cos_ref, sin_ref,
                     q_ref, k_ref, vt_ref):
    m = mod_ref[0]
    h = _modulated_norm(x_ref[...], g_ref[...], m[0:1], m[1:2]).astype(BF16)
    y = _dot(h, w_ref[...])
    cos = cos_ref[...]
    sin = sin_ref[...]
    qg = qg_ref[...]
    kg = kg_ref[...]
    hd = GQA_HEAD_DIM
    for j in range(GQA_HEADS):
        q = _head_norm(y[:, j * hd:(j + 1) * hd], qg)
        q_ref[:, j * hd:(j + 1) * hd] = _rope(q, cos, sin).astype(BF16)
    k0 = GQA_HEADS * hd
    for j in range(GQA_KV_HEADS):
        k = _head_norm(y[:, k0 + j * hd:k0 + (j + 1) * hd], kg)
        k_ref[:, j * hd:(j + 1) * hd] = _rope(k, cos, sin).astype(BF16)
    v0 = k0 + GQA_KV_HEADS * hd
    vt_ref[...] = y[:, v0:].T.astype(BF16)


def _rope_spec(tm):
    per_seq = SEQ // tm
    return pl.BlockSpec(
        (tm, LANES), lambda i: (jnp.where(i < N_LAT // tm, i % per_seq, per_seq), 0))


def _gqa_proj(x, mod, gain, w_qkv, q_gain, k_gain, cos, sin):
    kvw = GQA_KV_HEADS * GQA_HEAD_DIM
    return pl.pallas_call(
        _gqa_proj_kernel,
        grid=(N_ALL // TM,),
        in_specs=[
            _row_spec(TM, D), _mod_spec(TM), _resident((1, D)), _resident((D, D + 2 * kvw)),
            _resident((1, GQA_HEAD_DIM)), _resident((1, GQA_HEAD_DIM)),
            _rope_spec(TM), _rope_spec(TM),
        ],
        out_specs=[_row_spec(TM, D), _row_spec(TM, kvw), _col_spec(kvw, TM)],
        out_shape=[jax.ShapeDtypeStruct((N_ALL, D), BF16),
                   jax.ShapeDtypeStruct((N_ALL, kvw), BF16),
                   jax.ShapeDtypeStruct((kvw, N_ALL), BF16)],
        compiler_params=_params(1),
        name="gqa_proj",
    )(x, mod, gain, w_qkv, q_gain, k_gain, cos, sin)


def _mla_proj_kernel(x_ref, mod_ref, g_ref, wdq_ref, qg_ref, wuq_ref, wdkv_ref, kvg_ref, wukv_ref,
                     cos_ref, sin_ref, q_ref, k_ref, vt_ref):
    m = mod_ref[0]
    h = _modulated_norm(x_ref[...], g_ref[...], m[0:1], m[1:2]).astype(BF16)
    cos = cos_ref[...]
    sin = sin_ref[...]
    cq = _head_norm(_dot(h, wdq_ref[...]), qg_ref[...]).astype(BF16)
    q = _dot(cq, wuq_ref[...])
    ckv_pe = _dot(h, wdkv_ref[...])
    ckv = _head_norm(ckv_pe[:, :MLA_KV_RANK], kvg_ref[...]).astype(BF16)
    k_pe = _rope(ckv_pe[:, MLA_KV_RANK:], cos, sin).astype(BF16)
    kv = _dot(ckv, wukv_ref[...])
    for j in range(MLA_HEADS):
        c0 = j * MLA_QK
        q_ref[:, c0:c0 + MLA_NOPE] = q[:, c0:c0 + MLA_NOPE].astype(BF16)
        q_ref[:, c0 + MLA_NOPE:c0 + MLA_QK] = _rope(q[:, c0 + MLA_NOPE:c0 + MLA_QK], cos, sin).astype(BF16)
        k_ref[:, c0:c0 + MLA_NOPE] = kv[:, j * MLA_NOPE:(j + 1) * MLA_NOPE].astype(BF16)
        k_ref[:, c0 + MLA_NOPE:c0 + MLA_QK] = k_pe
    vt_ref[...] = kv[:, MLA_HEADS * MLA_NOPE:].T.astype(BF16)


def _mla_proj(x, mod, gain, w_dq, q_gain, w_uq, w_dkv, kv_gain, w_ukv, cos, sin):
    qkw = MLA_HEADS * MLA_QK
    vw = MLA_HEADS * MLA_V
    return pl.pallas_call(
        _mla_proj_kernel,
        grid=(N_ALL // TM,),
        in_specs=[
            _row_spec(TM, D), _mod_spec(TM), _resident((1, D)),
            _resident((D, MLA_Q_RANK)), _resident((1, MLA_Q_RANK)), _resident((MLA_Q_RANK, qkw)),
            _resident((D, MLA_KV_RANK + LANES)), _resident((1, MLA_KV_RANK)),
            _resident((MLA_KV_RANK, MLA_HEADS * (MLA_NOPE + MLA_V))),
            _rope_spec(TM), _rope_spec(TM),
        ],
        out_specs=[_row_spec(TM, qkw), _row_spec(TM, qkw), _col_spec(vw, TM)],
        out_shape=[jax.ShapeDtypeStruct((N_ALL, qkw), BF16),
                   jax.ShapeDtypeStruct((N_ALL, qkw), BF16),
                   jax.ShapeDtypeStruct((vw, N_ALL), BF16)],
        compiler_params=_params(1),
        name="mla_proj",
    )(x, mod, gain, w_dq, q_gain, w_uq, w_dkv, kv_gain, w_ukv, cos, sin)


def _nt_dot(a, b):
    return lax.dot_general(a, b, (((1,), (1,)), ((), ())), preferred_element_type=F32)


def _attn_head(q, k_parts, vt_parts):
    return _attn_finish(_attn_scores(q, k_parts), vt_parts)


def _attn_scores(q, k_parts):
    return [_nt_dot(k, q) for k in k_parts]


def _attn_finish(scores, vt_parts):
    mx = functools.reduce(jnp.maximum, [jnp.max(s, axis=0, keepdims=True) for s in scores])
    probs = [jnp.exp2(s - mx) for s in scores]
    den = sum(jnp.sum(p, axis=0, keepdims=True) for p in probs)
    acc = sum(_dot(vt, p.astype(BF16)) for vt, p in zip(vt_parts, probs))
    return (acc / den).T


def _attn_lat_kernel(q_ref, kl_ref, kc_ref, vtl_ref, vtc_ref, o_ref, *, heads, group, dq, dv):
    def scores(qh):
        ks = slice(qh // group * dq, (qh // group + 1) * dq)
        return _attn_scores(q_ref[:, qh * dq:(qh + 1) * dq], [kl_ref[:, ks], kc_ref[:, ks]])

    nxt = scores(0)
    for qh in range(heads * group):
        cur = nxt
        if qh + 1 < heads * group:
            nxt = scores(qh + 1)
        vs = slice(qh // group * dv, (qh // group + 1) * dv)
        out = _attn_finish(cur, [vtl_ref[vs, :], vtc_ref[vs, :]])
        o_ref[:, qh * dv:(qh + 1) * dv] = out.astype(BF16)


def _attn_ctx_kernel(q_ref, kc_ref, vtc_ref, prev_ref, o_ref, *, heads, group, dq, dv):
    del prev_ref
    for kv in range(heads):
        ks = slice(kv * dq, (kv + 1) * dq)
        vs = slice(kv * dv, (kv + 1) * dv)
        for g in range(group):
            qh = kv * group + g
            out = _attn_head(q_ref[:, qh * dq:(qh + 1) * dq], [kc_ref[:, ks]], [vtc_ref[vs, :]])
            o_ref[:, qh * dv:(qh + 1) * dv] = out.astype(BF16)


def _attention(q, k, vt, *, kv_heads, group, dq, dv, heads_per_step):
    hp = heads_per_step
    lat_tiles = SEQ // TQ
    ctx_block0 = N_LAT // CTX_LEN

    static = dict(heads=hp, group=group, dq=dq, dv=dv)
    out_shape = jax.ShapeDtypeStruct((N_ALL, kv_heads * group * dv), BF16)

    lat = pl.pallas_call(
        functools.partial(_attn_lat_kernel, **static),
        grid=(BATCH, kv_heads // hp, lat_tiles),
        in_specs=[
            pl.BlockSpec((TQ, hp * group * dq), lambda b, h, t: (b * lat_tiles + t, h)),
            pl.BlockSpec((SEQ, hp * dq), lambda b, h, t: (b, h)),
            pl.BlockSpec((CTX_LEN, hp * dq), lambda b, h, t: (ctx_block0 + b, h)),
            pl.BlockSpec((hp * dv, SEQ), lambda b, h, t: (h, b)),
            pl.BlockSpec((hp * dv, CTX_LEN), lambda b, h, t: (h, ctx_block0 + b)),
        ],
        out_specs=pl.BlockSpec((TQ, hp * group * dv), lambda b, h, t: (b * lat_tiles + t, h)),
        out_shape=out_shape,
        compiler_params=_params(3),
        name="attention_lat",
    )(q, k, k, vt, vt)

    return pl.pallas_call(
        functools.partial(_attn_ctx_kernel, **static),
        grid=(BATCH, kv_heads // hp),
        in_specs=[
            pl.BlockSpec((CTX_LEN, hp * group * dq), lambda b, h: (ctx_block0 + b, h)),
            pl.BlockSpec((CTX_LEN, hp * dq), lambda b, h: (ctx_block0 + b, h)),
            pl.BlockSpec((hp * dv, CTX_LEN), lambda b, h: (h, ctx_block0 + b)),
            pl.BlockSpec(memory_space=pl.ANY),
        ],
        out_specs=pl.BlockSpec((CTX_LEN, hp * group * dv), lambda b, h: (ctx_block0 + b, h)),
        out_shape=out_shape,
        input_output_aliases={3: 0},
        compiler_params=_params(2),
        name="attention_ctx",
    )(q, k, vt, lat)


def _attn_out_kernel(a_ref, w_ref, x_ref, mod_ref, o_ref):
    o_ref[...] = x_ref[...] + mod_ref[0][2:3] * _dot(a_ref[...], w_ref[...])


def _attn_out(a, w_o, x, mod):
    return pl.pallas_call(
        _attn_out_kernel,
        grid=(N_ALL // TM,),
        in_specs=[_row_spec(TM, D), _resident((D, D)), _row_spec(TM, D), _mod_spec(TM)],
        out_specs=_row_spec(TM, D),
        out_shape=jax.ShapeDtypeStruct((N_ALL, D), F32),
        compiler_params=_params(1),
        name="attn_out",
    )(a, w_o, x, mod)


def _ffn_kernel(x_ref, mod_ref, g_ref, w1_ref, w3_ref, w2_ref, o_ref):
    m = mod_ref[0]
    x = x_ref[...]
    h = _modulated_norm(x, g_ref[...], m[3:4], m[4:5]).astype(BF16)
    a = _dot(h, w1_ref[...])
    b = _dot(h, w3_ref[...])
    gated = (a * jax.nn.sigmoid(a) * b).astype(BF16)
    o_ref[...] = x + m[5:6] * _dot(gated, w2_ref[...])


def _ffn(x, mod, gain, w1, w3, w2, n_rows):
    return pl.pallas_call(
        _ffn_kernel,
        grid=(n_rows // TM,),
        in_specs=[
            _row_spec(TM, D), _mod_spec(TM), _resident((1, D)),
            _resident((D, FFN_HIDDEN)), _resident((D, FFN_HIDDEN)), _resident((FFN_HIDDEN, D)),
        ],
        out_specs=_row_spec(TM, D),
        out_shape=jax.ShapeDtypeStruct((n_rows, D), F32),
        compiler_params=_params(1),
        name="ffn",
    )(x, mod, gain, w1, w3, w2)


def _final_norm_kernel(x_ref, g_ref, o_ref):
    o_ref[...] = _head_norm(x_ref[...], g_ref[...])


def _final_norm(x, gain):
    return pl.pallas_call(
        _final_norm_kernel,
        grid=(N_LAT // TM,),
        in_specs=[_row_spec(TM, D), _resident((1, D))],
        out_specs=_row_spec(TM, D),
        out_shape=jax.ShapeDtypeStruct((N_LAT, D), F32),
        compiler_params=_params(1),
        name="final_norm",
    )(x, gain)


def _axial_angles(rot_dim):
    n = rot_dim // 4
    rows = jnp.repeat(jnp.arange(SEQ // GRID_W, dtype=F32), GRID_W)
    cols = jnp.tile(jnp.arange(GRID_W, dtype=F32), SEQ // GRID_W)
    freqs = ROPE_THETA ** (-jnp.arange(n, dtype=F32) / n)
    return jnp.concatenate([rows[:, None] * freqs, cols[:, None] * freqs], axis=-1)


def _rope_tables(rot_dim):
    ang = _axial_angles(rot_dim)
    cos, sin = jnp.cos(ang), jnp.sin(ang)
    pad = jnp.zeros((SEQ, (LANES - rot_dim) // 2), F32)
    cos_t = jnp.concatenate([cos, pad, cos, pad], axis=-1)
    sin_t = jnp.concatenate([-sin, pad, sin, pad], axis=-1)
    cos_t = jnp.concatenate([cos_t, jnp.ones((TM, LANES), F32)], axis=0)
    sin_t = jnp.concatenate([sin_t, jnp.zeros((TM, LANES), F32)], axis=0)
    return cos_t, sin_t


def _rope_slab_cols(w):
    half = MLA_ROPE // 2
    z = jnp.zeros((w.shape[0], LANES // 2 - half), w.dtype)
    return jnp.concatenate([w[:, :half], z, w[:, half:], z], axis=-1)


def _mla_weights(w_dq, w_uq, w_dkv, w_ukv):
    qk = MLA_NOPE + MLA_ROPE
    uq = w_uq.reshape(MLA_Q_RANK, MLA_HEADS, qk)
    uq = jnp.concatenate(
        [jnp.concatenate([uq[:, j, :MLA_NOPE], _rope_slab_cols(uq[:, j, MLA_NOPE:])], axis=-1)
         for j in range(MLA_HEADS)], axis=-1)
    dkv = jnp.concatenate([w_dkv[:, :MLA_KV_RANK], _rope_slab_cols(w_dkv[:, MLA_KV_RANK:])], axis=-1)
    ukv = w_ukv.reshape(MLA_KV_RANK, MLA_HEADS, MLA_NOPE + MLA_V)
    ukv = jnp.concatenate([ukv[:, :, :MLA_NOPE].reshape(MLA_KV_RANK, -1),
                           ukv[:, :, MLA_NOPE:].reshape(MLA_KV_RANK, -1)], axis=-1)
    return w_dq.astype(BF16), uq.astype(BF16), dkv.astype(BF16), ukv.astype(BF16)


def kernel(x, c, ctx, c_ctx, ada_w, ada_b, norm1_g, norm2_g, ffn_w1, ffn_w3, ffn_w2, conv_w_in, conv_w, conv_w_out, gqa_wq, gqa_wk, gqa_wv, gqa_q_norm, gqa_k_norm, gqa_wo, mla_w_dq, mla_q_norm, mla_w_uq, mla_w_dkv, mla_kv_norm, mla_w_ukv, mla_wo, final_g):
    assert x.shape == (BATCH, SEQ, D) and ctx.shape == (BATCH, CTX_LEN, D)
    xs = jnp.concatenate([x.reshape(N_LAT, D), ctx.reshape(N_CTX, D)], axis=0)

    cond = jnp.concatenate(
        [c, c_ctx[None], jnp.zeros((MOD_ROWS - BATCH - 1, D), F32)], axis=0)
    mods = _modulation_tables(cond, ada_w, ada_b).reshape(DEPTH, MOD_ROWS, N_MOD, D)

    gqa_cos, gqa_sin = _rope_tables(GQA_HEAD_DIM)
    mla_cos, mla_sin = _rope_tables(MLA_ROPE)

    for i in range(DEPTH):
        kind, j = i % 3, i // 3
        mod = mods[i]
        n_rows = N_ALL if i < DEPTH - 1 else N_LAT
        g1 = norm1_g[i][None]
        if kind == 0:
            b, u = _conv_in(xs, mod, g1, conv_w_in[j].astype(BF16), n_rows)
            xs = _conv_out(b, u, conv_w[j], conv_w_out[j].astype(BF16), xs, mod, n_rows)
        elif kind == 1:
            w_qkv = jnp.concatenate([gqa_wq[j], gqa_wk[j], gqa_wv[j]], axis=-1).astype(BF16)
            q_gain = gqa_q_norm[j][None] * (GQA_HEAD_DIM ** -0.5 * LOG2E)
            q, k, v = _gqa_proj(xs, mod, g1, w_qkv, q_gain, gqa_k_norm[j][None], gqa_cos, gqa_sin)
            a = _attention(q, k, v, kv_heads=GQA_KV_HEADS, group=GQA_GROUP,
                           dq=GQA_HEAD_DIM, dv=GQA_HEAD_DIM, heads_per_step=1)
            xs = _attn_out(a, gqa_wo[j].astype(BF16), xs, mod)
        else:
            w_dq, w_uq, w_dkv, w_ukv = _mla_weights(mla_w_dq[j], mla_w_uq[j], mla_w_dkv[j], mla_w_ukv[j])
            q_gain = mla_q_norm[j][None] * ((MLA_NOPE + MLA_ROPE) ** -0.5 * LOG2E)
            q, k, v = _mla_proj(xs, mod, g1, w_dq, q_gain, w_uq, w_dkv,
                                mla_kv_norm[j][None], w_ukv, mla_cos, mla_sin)
            a = _attention(q, k, v, kv_heads=MLA_HEADS, group=1, dq=MLA_QK, dv=MLA_V, heads_per_step=4)
            xs = _attn_out(a, mla_wo[j].astype(BF16), xs, mod)
        xs = _ffn(xs, mod, norm2_g[i][None], ffn_w1[i].astype(BF16), ffn_w3[i].astype(BF16),
                  ffn_w2[i].astype(BF16), n_rows)

    return _final_norm(xs, final_g[None]).reshape(BATCH, SEQ, D)
```

```python
import functools

import jax
import jax.numpy as jnp
import numpy as np
from jax import lax
from jax.experimental import pallas as pl
from jax.experimental.pallas import tpu as pltpu

D = 1024
BATCH = 8
SEQ = 2048
DEPTH = 4
GRID_W = 64
CTX_LEN = 256
N_MOD = 6
FFN_HIDDEN = 2816
ROPE_THETA = 10000.0
EPS = 1e-6
GQA_HEAD_DIM = 128
GQA_HEADS = 8
GQA_KV_HEADS = 2
GQA_GROUP = 4
MLA_HEADS = 8
MLA_NOPE = 128
MLA_ROPE = 64
MLA_V = 128
MLA_KV_RANK = 256
MLA_Q_RANK = 768
MLA_QK = 256
LANES = 128
BF16_SUBLANES = 16
VT_ROWS = 128 + BF16_SUBLANES

N_LAT = BATCH * SEQ
N_CTX = BATCH * CTX_LEN
N_ALL = N_LAT + N_CTX
MOD_ROWS = 16

TM = 512
TQ = 512
HALO = 8
VMEM_LIMIT = 56 * 1024 * 1024
LOG2E = 1.4426950408889634

F32 = jnp.float32
BF16 = jnp.bfloat16


def _params(n_axes):
    return pltpu.CompilerParams(
        dimension_semantics=("arbitrary",) * n_axes, vmem_limit_bytes=VMEM_LIMIT)


def _resident(shape):
    nd = len(shape)
    return pl.BlockSpec(shape, lambda *_: (0,) * nd, pipeline_mode=pl.Buffered(1))


def _mod_spec(tm):
    return pl.BlockSpec((1, N_MOD, D), lambda i: (jnp.minimum(i * tm // SEQ, BATCH), 0, 0))


def _row_spec(tm, width):
    return pl.BlockSpec((tm, width), lambda i: (i, 0))


def _col_spec(height, tm):
    return pl.BlockSpec((height, tm), lambda i: (0, i))


def _dot(a, b):
    return jnp.dot(a, b, preferred_element_type=F32)


def _modulated_norm(x, gain, shift, scale):
    r = lax.rsqrt(jnp.mean(x * x, axis=-1, keepdims=True) + EPS)
    return (x * r) * (gain * (1.0 + scale)) + shift


def _head_norm(x, gain):
    r = lax.rsqrt(jnp.mean(x * x, axis=-1, keepdims=True) + EPS)
    return (x * r) * gain


def _store_vt(vt_ref, v, n_heads):
    tm = v.shape[0]
    for j in range(n_heads):
        r0 = j * VT_ROWS
        vt_ref[r0:r0 + 128, :] = v[:, j * 128:(j + 1) * 128].T.astype(BF16)
        vt_ref[r0 + 128:r0 + VT_ROWS, :] = jnp.ones((BF16_SUBLANES, tm), BF16)


def _rope(x, cos, sin):
    return x * cos + pltpu.roll(x, LANES // 2, axis=1) * sin


def _mod_kernel(cond_ref, w_ref, b_ref, o_ref):
    c = cond_ref[...]
    act = (c * jax.nn.sigmoid(c)).astype(BF16)
    o_ref[0] = _dot(act, w_ref[0].astype(BF16)) + b_ref[0]


def _modulation_tables(cond, ada_w, ada_b):
    tn = 1536
    return pl.pallas_call(
        _mod_kernel,
        grid=(DEPTH, N_MOD * D // tn),
        in_specs=[
            pl.BlockSpec((MOD_ROWS, D), lambda l, j: (0, 0)),
            pl.BlockSpec((1, D, tn), lambda l, j: (l, 0, j)),
            pl.BlockSpec((1, 1, tn), lambda l, j: (l, 0, j)),
        ],
        out_specs=pl.BlockSpec((1, MOD_ROWS, tn), lambda l, j: (l, 0, j)),
        out_shape=jax.ShapeDtypeStruct((DEPTH, MOD_ROWS, N_MOD * D), F32),
        compiler_params=_params(2),
        name="modulation",
    )(cond, ada_w, ada_b.reshape(DEPTH, 1, N_MOD * D))


def _conv_in_compute(x, mod_ref, g_ref, w_ref, b_ref, u_ref):
    m = mod_ref[0]
    h = _modulated_norm(x, g_ref[...], m[0:1], m[1:2]).astype(BF16)
    y = _dot(h, w_ref[...])
    b_ref[...] = y[:, :D].astype(BF16)
    u_ref[...] = y[:, D:2 * D] * y[:, 2 * D:]


def _conv_in_kernel(x_ref, mod_ref, g_ref, w_ref, b_ref, u_ref):
    _conv_in_compute(x_ref[...], mod_ref, g_ref, w_ref, b_ref, u_ref)


def _conv_in_first_kernel(lat_ref, ctx_ref, mod_ref, g_ref, w_ref, b_ref, u_ref, xs_ref):
    x = jnp.where(pl.program_id(0) < N_LAT // TM, lat_ref[...], ctx_ref[...])
    xs_ref[...] = x
    _conv_in_compute(x, mod_ref, g_ref, w_ref, b_ref, u_ref)


def _conv_in(x, mod, gain, w_in, n_rows):
    return pl.pallas_call(
        _conv_in_kernel,
        grid=(n_rows // TM,),
        in_specs=[_row_spec(TM, D), _mod_spec(TM), _resident((1, D)), _resident((D, 3 * D))],
        out_specs=[_row_spec(TM, D), _row_spec(TM, D)],
        out_shape=[jax.ShapeDtypeStruct((n_rows, D), BF16), jax.ShapeDtypeStruct((n_rows, D), F32)],
        compiler_params=_params(1),
        name="conv_in",
    )(x, mod, gain, w_in)


def _conv_in_first(lat, ctx, mod, gain, w_in):
    lat_tiles = N_LAT // TM
    return pl.pallas_call(
        _conv_in_first_kernel,
        grid=(N_ALL // TM,),
        in_specs=[
            pl.BlockSpec((TM, D), lambda i: (jnp.minimum(i, lat_tiles - 1), 0)),
            pl.BlockSpec((TM, D), lambda i: (jnp.maximum(i - lat_tiles, 0), 0)),
            _mod_spec(TM), _resident((1, D)), _resident((D, 3 * D)),
        ],
        out_specs=[_row_spec(TM, D), _row_spec(TM, D), _row_spec(TM, D)],
        out_shape=[jax.ShapeDtypeStruct((N_ALL, D), BF16), jax.ShapeDtypeStruct((N_ALL, D), F32),
                   jax.ShapeDtypeStruct((N_ALL, D), F32)],
        compiler_params=_params(1),
        name="conv_in_first",
    )(lat, ctx, mod, gain, w_in)


def _conv_gate(b_ref, u_ref, up_ref, un_ref, cw_ref):
    tm = u_ref.shape[0]
    u = u_ref[...]
    local = lax.broadcasted_iota(jnp.int32, (tm, 1), 0)
    row = local + pl.program_id(0) * tm
    in_ctx = row >= N_LAT
    first = ((row & (CTX_LEN - 1)) == 0) & (in_ctx | ((row & (SEQ - 1)) == 0))
    last = ((row & (CTX_LEN - 1)) == CTX_LEN - 1) & (in_ctx | ((row & (SEQ - 1)) == SEQ - 1))
    prev = jnp.where(local == 0, up_ref[HALO - 1:HALO, :], pltpu.roll(u, 1, axis=0))
    prev = jnp.where(first, 0.0, prev)
    nxt = jnp.where(local == tm - 1, un_ref[0:1, :], pltpu.roll(u, tm - 1, axis=0))
    nxt = jnp.where(last, 0.0, nxt)
    cw = cw_ref[...]
    z = prev * cw[0:1] + u * cw[1:2] + nxt * cw[2:3]
    return (b_ref[...].astype(F32) * z).astype(BF16)


def _gqa_proj_kernel(x_ref, mod_ref, g_ref, w_ref, qg_ref, kg_ref, cos_ref, sin_ref,
                     q_ref, k_ref, vt_ref):
    m = mod_ref[0]
    h = _modulated_norm(x_ref[...], g_ref[...], m[0:1], m[1:2]).astype(BF16)
    y = _dot(h, w_ref[...])
    cos = cos_ref[...]
    sin = sin_ref[...]
    qg = qg_ref[...]
    kg = kg_ref[...]
    hd = GQA_HEAD_DIM
    for j in range(GQA_HEADS):
        q = _head_norm(y[:, j * hd:(j + 1) * hd], qg)
        q_ref[:, j * hd:(j + 1) * hd] = _rope(q, cos, sin).astype(BF16)
    k0 = GQA_HEADS * hd
    for j in range(GQA_KV_HEADS):
        k = _head_norm(y[:, k0 + j * hd:k0 + (j + 1) * hd], kg)
        k_ref[:, j * hd:(j + 1) * hd] = _rope(k, cos, sin).astype(BF16)
    v0 = k0 + GQA_KV_HEADS * hd
    _store_vt(vt_ref, y[:, v0:], GQA_KV_HEADS)


def _rope_spec(tm):
    per_seq = SEQ // tm
    return pl.BlockSpec(
        (tm, LANES), lambda i: (jnp.where(i < N_LAT // tm, i % per_seq, per_seq), 0))


def _gqa_proj(x, mod, gain, w_qkv, q_gain, k_gain, cos, sin):
    kvw = GQA_KV_HEADS * GQA_HEAD_DIM
    return pl.pallas_call(
        _gqa_proj_kernel,
        grid=(N_ALL // TM,),
        in_specs=[
            _row_spec(TM, D), _mod_spec(TM), _resident((1, D)), _resident((D, D + 2 * kvw)),
            _resident((1, GQA_HEAD_DIM)), _resident((1, GQA_HEAD_DIM)),
            _rope_spec(TM), _rope_spec(TM),
        ],
        out_specs=[_row_spec(TM, D), _row_spec(TM, kvw), _col_spec(GQA_KV_HEADS * VT_ROWS, TM)],
        out_shape=[jax.ShapeDtypeStruct((N_ALL, D), BF16),
                   jax.ShapeDtypeStruct((N_ALL, kvw), BF16),
                   jax.ShapeDtypeStruct((GQA_KV_HEADS * VT_ROWS, N_ALL), BF16)],
        compiler_params=_params(1),
        name="gqa_proj",
    )(x, mod, gain, w_qkv, q_gain, k_gain, cos, sin)


def _mla_proj_kernel(x_ref, mod_ref, g_ref, wdq_ref, qg_ref, wuq_ref, wdkv_ref, kvg_ref, wukv_ref,
                     cos_ref, sin_ref, q_ref, k_ref, vt_ref):
    m = mod_ref[0]
    h = _modulated_norm(x_ref[...], g_ref[...], m[0:1], m[1:2]).astype(BF16)
    cos = cos_ref[...]
    sin = sin_ref[...]
    cq = _head_norm(_dot(h, wdq_ref[...]), qg_ref[...]).astype(BF16)
    q = _dot(cq, wuq_ref[...])
    ckv_pe = _dot(h, wdkv_ref[...])
    ckv = _head_norm(ckv_pe[:, :MLA_KV_RANK], kvg_ref[...]).astype(BF16)
    k_pe = _rope(ckv_pe[:, MLA_KV_RANK:], cos, sin).astype(BF16)
    kv = _dot(ckv, wukv_ref[...])
    for j in range(MLA_HEADS):
        c0 = j * MLA_QK
        q_ref[:, c0:c0 + MLA_NOPE] = q[:, c0:c0 + MLA_NOPE].astype(BF16)
        q_ref[:, c0 + MLA_NOPE:c0 + MLA_QK] = _rope(q[:, c0 + MLA_NOPE:c0 + MLA_QK], cos, sin).astype(BF16)
        k_ref[:, c0:c0 + MLA_NOPE] = kv[:, j * MLA_NOPE:(j + 1) * MLA_NOPE].astype(BF16)
        k_ref[:, c0 + MLA_NOPE:c0 + MLA_QK] = k_pe
    _store_vt(vt_ref, kv[:, MLA_HEADS * MLA_NOPE:], MLA_HEADS)


def _mla_proj(x, mod, gain, w_dq, q_gain, w_uq, w_dkv, kv_gain, w_ukv, cos, sin):
    qkw = MLA_HEADS * MLA_QK
    return pl.pallas_call(
        _mla_proj_kernel,
        grid=(N_ALL // TM,),
        in_specs=[
            _row_spec(TM, D), _mod_spec(TM), _resident((1, D)),
            _resident((D, MLA_Q_RANK)), _resident((1, MLA_Q_RANK)), _resident((MLA_Q_RANK, qkw)),
            _resident((D, MLA_KV_RANK + LANES)), _resident((1, MLA_KV_RANK)),
            _resident((MLA_KV_RANK, MLA_HEADS * (MLA_NOPE + MLA_V))),
            _rope_spec(TM), _rope_spec(TM),
        ],
        out_specs=[_row_spec(TM, qkw), _row_spec(TM, qkw), _col_spec(MLA_HEADS * VT_ROWS, TM)],
        out_shape=[jax.ShapeDtypeStruct((N_ALL, qkw), BF16),
                   jax.ShapeDtypeStruct((N_ALL, qkw), BF16),
                   jax.ShapeDtypeStruct((MLA_HEADS * VT_ROWS, N_ALL), BF16)],
        compiler_params=_params(1),
        name="mla_proj",
    )(x, mod, gain, w_dq, q_gain, w_uq, w_dkv, kv_gain, w_ukv, cos, sin)


def _nt_dot(a, b):
    return lax.dot_general(a, b, (((1,), (1,)), ((), ())), preferred_element_type=F32)


def _attn_head(q, k_parts, vt_parts):
    return _attn_finish(_attn_scores(q, k_parts), vt_parts)


def _attn_scores(q, k_parts):
    return [_nt_dot(k, q) for k in k_parts]


def _attn_finish(scores, vt_parts):
    mx = functools.reduce(jnp.maximum, [jnp.max(s, axis=0, keepdims=True) for s in scores])
    acc = sum(_dot(vt, jnp.exp2(s - mx).astype(BF16)) for vt, s in zip(vt_parts, scores))
    dv = acc.shape[0] - BF16_SUBLANES
    return (acc[:dv] / acc[dv:dv + 1]).T


def _attn_lat_kernel(q_ref, kl_ref, kc_ref, vtl_ref, vtc_ref, o_ref, *, heads, group, dq, dv):
    def scores(qh):
        ks = slice(qh // group * dq, (qh // group + 1) * dq)
        return _attn_scores(q_ref[:, qh * dq:(qh + 1) * dq], [kl_ref[:, ks], kc_ref[:, ks]])

    nxt = scores(0)
    for qh in range(heads * group):
        cur = nxt
        if qh + 1 < heads * group:
            nxt = scores(qh + 1)
        vs = slice(qh // group * VT_ROWS, (qh // group + 1) * VT_ROWS)
        out = _attn_finish(cur, [vtl_ref[vs, :], vtc_ref[vs, :]])
        o_ref[:, qh * dv:(qh + 1) * dv] = out.astype(BF16)


def _attn_ctx_kernel(q_ref, kc_ref, vtc_ref, prev_ref, o_ref, *, heads, group, dq, dv):
    del prev_ref
    for kv in range(heads):
        ks = slice(kv * dq, (kv + 1) * dq)
        vs = slice(kv * VT_ROWS, (kv + 1) * VT_ROWS)
        for g in range(group):
            qh = kv * group + g
            out = _attn_head(q_ref[:, qh * dq:(qh + 1) * dq], [kc_ref[:, ks]], [vtc_ref[vs, :]])
            o_ref[:, qh * dv:(qh + 1) * dv] = out.astype(BF16)


def _attention(q, k, vt, *, kv_heads, group, dq, dv, heads_per_step):
    hp = heads_per_step
    lat_tiles = SEQ // TQ
    ctx_block0 = N_LAT // CTX_LEN

    static = dict(heads=hp, group=group, dq=dq, dv=dv)
    out_shape = jax.ShapeDtypeStruct((N_ALL, kv_heads * group * dv), BF16)

    lat = pl.pallas_call(
        functools.partial(_attn_lat_kernel, **static),
        grid=(BATCH, kv_heads // hp, lat_tiles),
        in_specs=[
            pl.BlockSpec((TQ, hp * group * dq), lambda b, h, t: (b * lat_tiles + t, h)),
            pl.BlockSpec((SEQ, hp * dq), lambda b, h, t: (b, h)),
            pl.BlockSpec((CTX_LEN, hp * dq), lambda b, h, t: (ctx_block0 + b, h)),
            pl.BlockSpec((hp * VT_ROWS, SEQ), lambda b, h, t: (h, b)),
            pl.BlockSpec((hp * VT_ROWS, CTX_LEN), lambda b, h, t: (h, ctx_block0 + b)),
        ],
        out_specs=pl.BlockSpec((TQ, hp * group * dv), lambda b, h, t: (b * lat_tiles + t, h)),
        out_shape=out_shape,
        compiler_params=_params(3),
        name="attention_lat",
    )(q, k, k, vt, vt)

    return pl.pallas_call(
        functools.partial(_attn_ctx_kernel, **static),
        grid=(BATCH, kv_heads // hp),
        in_specs=[
            pl.BlockSpec((CTX_LEN, hp * group * dq), lambda b, h: (ctx_block0 + b, h)),
            pl.BlockSpec((CTX_LEN, hp * dq), lambda b, h: (ctx_block0 + b, h)),
            pl.BlockSpec((hp * VT_ROWS, CTX_LEN), lambda b, h: (h, ctx_block0 + b)),
            pl.BlockSpec(memory_space=pl.ANY),
        ],
        out_specs=pl.BlockSpec((CTX_LEN, hp * group * dv), lambda b, h: (ctx_block0 + b, h)),
        out_shape=out_shape,
        input_output_aliases={3: 0},
        compiler_params=_params(2),
        name="attention_ctx",
    )(q, k, vt, lat)


def _ffn_tail(x1, m, g2_ref, w1_ref, w3_ref, w2_ref):
    h = _modulated_norm(x1, g2_ref[...], m[3:4], m[4:5]).astype(BF16)
    a = _dot(h, w1_ref[...])
    b = _dot(h, w3_ref[...])
    gated = (a * jax.nn.sigmoid(a) * b).astype(BF16)
    return x1 + m[5:6] * _dot(gated, w2_ref[...])


def _attn_tail_kernel(a_ref, wo_ref, x_ref, mod_ref, g2_ref, w1_ref, w3_ref, w2_ref, o_ref):
    m = mod_ref[0]
    x1 = x_ref[...] + m[2:3] * _dot(a_ref[...], wo_ref[...])
    o_ref[...] = _ffn_tail(x1, m, g2_ref, w1_ref, w3_ref, w2_ref)


def _conv_tail_kernel(b_ref, u_ref, up_ref, un_ref, cw_ref, wo_ref, x_ref, mod_ref,
                      g2_ref, w1_ref, w3_ref, w2_ref, fg_ref, o_ref, *, final):
    m = mod_ref[0]
    x1 = x_ref[...] + m[2:3] * _dot(_conv_gate(b_ref, u_ref, up_ref, un_ref, cw_ref), wo_ref[...])
    x2 = _ffn_tail(x1, m, g2_ref, w1_ref, w3_ref, w2_ref)
    o_ref[...] = _head_norm(x2, fg_ref[...]) if final else x2


def _ffn_specs():
    return [_mod_spec(TM), _resident((1, D)),
            _resident((D, FFN_HIDDEN)), _resident((D, FFN_HIDDEN)), _resident((FFN_HIDDEN, D))]


def _attn_tail(a, w_o, x, mod, g2, w1, w3, w2):
    return pl.pallas_call(
        _attn_tail_kernel,
        grid=(N_ALL // TM,),
        in_specs=[_row_spec(TM, D), _resident((D, D)), _row_spec(TM, D)] + _ffn_specs(),
        out_specs=_row_spec(TM, D),
        out_shape=jax.ShapeDtypeStruct((N_ALL, D), F32),
        compiler_params=_params(1),
        name="attn_tail",
    )(a, w_o, x, mod, g2, w1, w3, w2)


def _conv_tail(b, u, conv_w, w_out, x, mod, g2, w1, w3, w2, final_gain, n_rows, final):
    per = TM // HALO
    n_halo = n_rows // HALO
    return pl.pallas_call(
        functools.partial(_conv_tail_kernel, final=final),
        grid=(n_rows // TM,),
        in_specs=[
            _row_spec(TM, D), _row_spec(TM, D),
            pl.BlockSpec((HALO, D), lambda i: (jnp.maximum(i * per - 1, 0), 0)),
            pl.BlockSpec((HALO, D), lambda i: (jnp.minimum((i + 1) * per, n_halo - 1), 0)),
            _resident((3, D)), _resident((D, D)), _row_spec(TM, D),
        ] + _ffn_specs() + [_resident((1, D))],
        out_specs=_row_spec(TM, D),
        out_shape=jax.ShapeDtypeStruct((n_rows, D), F32),
        compiler_params=_params(1),
        name="conv_tail",
    )(b, u, u, u, conv_w, w_out, x, mod, g2, w1, w3, w2, final_gain)


def _axial_angles(rot_dim):
    n = rot_dim // 4
    rows = jnp.repeat(jnp.arange(SEQ // GRID_W, dtype=F32), GRID_W)
    cols = jnp.tile(jnp.arange(GRID_W, dtype=F32), SEQ // GRID_W)
    freqs = ROPE_THETA ** (-jnp.arange(n, dtype=F32) / n)
    return jnp.concatenate([rows[:, None] * freqs, cols[:, None] * freqs], axis=-1)


def _rope_tables(rot_dim):
    ang = _axial_angles(rot_dim)
    cos, sin = jnp.cos(ang), jnp.sin(ang)
    pad = jnp.zeros((SEQ, (LANES - rot_dim) // 2), F32)
    cos_t = jnp.concatenate([cos, pad, cos, pad], axis=-1)
    sin_t = jnp.concatenate([-sin, pad, sin, pad], axis=-1)
    cos_t = jnp.concatenate([cos_t, jnp.ones((TM, LANES), F32)], axis=0)
    sin_t = jnp.concatenate([sin_t, jnp.zeros((TM, LANES), F32)], axis=0)
    return cos_t, sin_t


def _rope_slab_cols(w):
    half = MLA_ROPE // 2
    z = jnp.zeros((w.shape[0], LANES // 2 - half), w.dtype)
    return jnp.concatenate([w[:, :half], z, w[:, half:], z], axis=-1)


def _mla_weights(w_dq, w_uq, w_dkv, w_ukv):
    qk = MLA_NOPE + MLA_ROPE
    uq = w_uq.reshape(MLA_Q_RANK, MLA_HEADS, qk)
    uq = jnp.concatenate(
        [jnp.concatenate([uq[:, j, :MLA_NOPE], _rope_slab_cols(uq[:, j, MLA_NOPE:])], axis=-1)
         for j in range(MLA_HEADS)], axis=-1)
    dkv = jnp.concatenate([w_dkv[:, :MLA_KV_RANK], _rope_slab_cols(w_dkv[:, MLA_KV_RANK:])], axis=-1)
    ukv = w_ukv.reshape(MLA_KV_RANK, MLA_HEADS, MLA_NOPE + MLA_V)
    ukv = jnp.concatenate([ukv[:, :, :MLA_NOPE].reshape(MLA_KV_RANK, -1),
                           ukv[:, :, MLA_NOPE:].reshape(MLA_KV_RANK, -1)], axis=-1)
    return w_dq.astype(BF16), uq.astype(BF16), dkv.astype(BF16), ukv.astype(BF16)


def kernel(x, c, ctx, c_ctx, ada_w, ada_b, norm1_g, norm2_g, ffn_w1, ffn_w3, ffn_w2, conv_w_in, conv_w, conv_w_out, gqa_wq, gqa_wk, gqa_wv, gqa_q_norm, gqa_k_norm, gqa_wo, mla_w_dq, mla_q_norm, mla_w_uq, mla_w_dkv, mla_kv_norm, mla_w_ukv, mla_wo, final_g):
    assert x.shape == (BATCH, SEQ, D) and ctx.shape == (BATCH, CTX_LEN, D)
    cond = jnp.concatenate(
        [c, c_ctx[None], jnp.zeros((MOD_ROWS - BATCH - 1, D), F32)], axis=0)
    mods = _modulation_tables(cond, ada_w, ada_b).reshape(DEPTH, MOD_ROWS, N_MOD, D)

    gqa_cos, gqa_sin = _rope_tables(GQA_HEAD_DIM)
    mla_cos, mla_sin = _rope_tables(MLA_ROPE)

    xs = None
    for i in range(DEPTH):
        kind, j = i % 3, i // 3
        mod = mods[i]
        last = i == DEPTH - 1
        n_rows = N_LAT if last else N_ALL
        g1 = norm1_g[i][None]
        ffn = (norm2_g[i][None], ffn_w1[i].astype(BF16), ffn_w3[i].astype(BF16), ffn_w2[i].astype(BF16))
        if kind == 0:
            w_in = conv_w_in[j].astype(BF16)
            if i == 0:
                b, u, xs = _conv_in_first(x.reshape(N_LAT, D), ctx.reshape(N_CTX, D), mod, g1, w_in)
            else:
                b, u = _conv_in(xs, mod, g1, w_in, n_rows)
            xs = _conv_tail(b, u, conv_w[j], conv_w_out[j].astype(BF16), xs, mod, *ffn,
                            final_g[None], n_rows, last)
        elif kind == 1:
            w_qkv = jnp.concatenate([gqa_wq[j], gqa_wk[j], gqa_wv[j]], axis=-1).astype(BF16)
            q_gain = gqa_q_norm[j][None] * (GQA_HEAD_DIM ** -0.5 * LOG2E)
            q, k, vt = _gqa_proj(xs, mod, g1, w_qkv, q_gain, gqa_k_norm[j][None], gqa_cos, gqa_sin)
            a = _attention(q, k, vt, kv_heads=GQA_KV_HEADS, group=GQA_GROUP,
                           dq=GQA_HEAD_DIM, dv=GQA_HEAD_DIM, heads_per_step=1)
            xs = _attn_tail(a, gqa_wo[j].astype(BF16), xs, mod, *ffn)
        else:
            w_dq, w_uq, w_dkv, w_ukv = _mla_weights(mla_w_dq[j], mla_w_uq[j], mla_w_dkv[j], mla_w_ukv[j])
            q_gain = mla_q_norm[j][None] * ((MLA_NOPE + MLA_ROPE) ** -0.5 * LOG2E)
            q, k, vt = _mla_proj(xs, mod, g1, w_dq, q_gain, w_uq, w_dkv,
                                 mla_kv_norm[j][None], w_ukv, mla_cos, mla_sin)
            a = _attention(q, k, vt, kv_heads=MLA_HEADS, group=1, dq=MLA_QK, dv=MLA_V, heads_per_step=4)
            xs = _attn_tail(a, mla_wo[j].astype(BF16), xs, mod, *ffn)

    assert (DEPTH - 1) % 3 == 0
    return xs.reshape(BATCH, SEQ, D)
```

```python
import functools

import jax
import jax.numpy as jnp
import numpy as np
from jax import lax
from jax.experimental import pallas as pl
from jax.experimental.pallas import tpu as pltpu

D = 1024
BATCH = 8
SEQ = 2048
DEPTH = 4
GRID_W = 64
CTX_LEN = 256
N_MOD = 6
FFN_HIDDEN = 2816
ROPE_THETA = 10000.0
EPS = 1e-6
GQA_HEAD_DIM = 128
GQA_HEADS = 8
GQA_KV_HEADS = 2
GQA_GROUP = 4
MLA_HEADS = 8
MLA_NOPE = 128
MLA_ROPE = 64
MLA_V = 128
MLA_KV_RANK = 256
MLA_Q_RANK = 768
MLA_QK = 256
GQA_QK = 256
LANES = 128
VT_ROWS = 128

N_LAT = BATCH * SEQ
N_CTX = BATCH * CTX_LEN
N_ALL = N_LAT + N_CTX
MOD_ROWS = 16

TM = 512
TQ = 512
HALO = 8
VMEM_LIMIT = 56 * 1024 * 1024
LOG2E = 1.4426950408889634

F32 = jnp.float32
BF16 = jnp.bfloat16


def _params(n_axes):
    return pltpu.CompilerParams(
        dimension_semantics=("arbitrary",) * n_axes, vmem_limit_bytes=VMEM_LIMIT)


def _resident(shape, layer=None):
    nd = len(shape)
    if layer is None:
        return pl.BlockSpec(shape, lambda *_: (0,) * nd, pipeline_mode=pl.Buffered(1))
    return pl.BlockSpec((1,) + tuple(shape), lambda *_: (layer,) + (0,) * nd, pipeline_mode=pl.Buffered(1))


def _w(ref):
    return ref[0] if len(ref.shape) == 3 else ref[...]


def _mod_spec(tm):
    return pl.BlockSpec((1, N_MOD, D), lambda i: (jnp.minimum(i * tm // SEQ, BATCH), 0, 0))


def _row_spec(tm, width):
    return pl.BlockSpec((tm, width), lambda i: (i, 0))


def _col_spec(height, tm):
    return pl.BlockSpec((height, tm), lambda i: (0, i))


def _dot(a, b):
    return jnp.dot(a, b, preferred_element_type=F32)


def _modulated_norm(x, gain, shift, scale):
    r = lax.rsqrt(jnp.mean(x * x, axis=-1, keepdims=True) + EPS)
    return (x * r) * (gain * (1.0 + scale)) + shift


def _head_norm(x, gain):
    r = lax.rsqrt(jnp.mean(x * x, axis=-1, keepdims=True) + EPS)
    return (x * r) * gain


def _store_vt(vt_ref, v, n_heads):
    for j in range(n_heads):
        vt_ref[j * VT_ROWS:(j + 1) * VT_ROWS, :] = v[:, j * VT_ROWS:(j + 1) * VT_ROWS].T.astype(BF16)


def _rope(a, b, cos, sin):
    return a * cos - b * sin, a * sin + b * cos


def _lane_mask(lo, hi):
    lane = lax.broadcasted_iota(jnp.int32, (1, LANES), 1)
    return (lane >= lo) & (lane < hi)


def _mod_kernel(cond_ref, w_ref, b_ref, o_ref):
    c = cond_ref[...]
    act = (c * jax.nn.sigmoid(c)).astype(BF16)
    o_ref[0] = _dot(act, w_ref[0].astype(BF16)) + b_ref[0]


def _modulation_tables(cond, ada_w, ada_b):
    tn = 1536
    return pl.pallas_call(
        _mod_kernel,
        grid=(DEPTH, N_MOD * D // tn),
        in_specs=[
            pl.BlockSpec((MOD_ROWS, D), lambda l, j: (0, 0)),
            pl.BlockSpec((1, D, tn), lambda l, j: (l, 0, j)),
            pl.BlockSpec((1, 1, tn), lambda l, j: (l, 0, j)),
        ],
        out_specs=pl.BlockSpec((1, MOD_ROWS, tn), lambda l, j: (l, 0, j)),
        out_shape=jax.ShapeDtypeStruct((DEPTH, MOD_ROWS, N_MOD * D), F32),
        compiler_params=_params(2),
        name="modulation",
    )(cond, ada_w, ada_b.reshape(DEPTH, 1, N_MOD * D))


def _conv_in_compute(x, mod_ref, g_ref, w_ref, b_ref, u_ref):
    m = mod_ref[0]
    h = _modulated_norm(x, g_ref[...], m[0:1], m[1:2]).astype(BF16)
    y = _dot(h, _w(w_ref))
    b_ref[...] = y[:, :D].astype(BF16)
    u_ref[...] = y[:, D:2 * D] * y[:, 2 * D:]


def _conv_in_kernel(x_ref, mod_ref, g_ref, w_ref, b_ref, u_ref):
    _conv_in_compute(x_ref[...], mod_ref, g_ref, w_ref, b_ref, u_ref)


def _conv_in_first_kernel(lat_ref, ctx_ref, mod_ref, g_ref, w_ref, b_ref, u_ref, xs_ref):
    x = jnp.where(pl.program_id(0) < N_LAT // TM, lat_ref[...], ctx_ref[...])
    xs_ref[...] = x
    _conv_in_compute(x, mod_ref, g_ref, w_ref, b_ref, u_ref)


def _conv_in(x, mod, gain, w_in, layer, n_rows):
    return pl.pallas_call(
        _conv_in_kernel,
        grid=(n_rows // TM,),
        in_specs=[_row_spec(TM, D), _mod_spec(TM), _resident((1, D)), _resident((D, 3 * D), layer)],
        out_specs=[_row_spec(TM, D), _row_spec(TM, D)],
        out_shape=[jax.ShapeDtypeStruct((n_rows, D), BF16), jax.ShapeDtypeStruct((n_rows, D), F32)],
        compiler_params=_params(1),
        name="conv_in",
    )(x, mod, gain, w_in)


def _conv_in_first(lat, ctx, mod, gain, w_in, layer):
    lat_tiles = N_LAT // TM
    return pl.pallas_call(
        _conv_in_first_kernel,
        grid=(N_ALL // TM,),
        in_specs=[
            pl.BlockSpec((TM, D), lambda i: (jnp.minimum(i, lat_tiles - 1), 0)),
            pl.BlockSpec((TM, D), lambda i: (jnp.maximum(i - lat_tiles, 0), 0)),
            _mod_spec(TM), _resident((1, D)), _resident((D, 3 * D), layer),
        ],
        out_specs=[_row_spec(TM, D), _row_spec(TM, D), _row_spec(TM, D)],
        out_shape=[jax.ShapeDtypeStruct((N_ALL, D), BF16), jax.ShapeDtypeStruct((N_ALL, D), F32),
                   jax.ShapeDtypeStruct((N_ALL, D), F32)],
        compiler_params=_params(1),
        name="conv_in_first",
    )(lat, ctx, mod, gain, w_in)


def _conv_gate(b_ref, u_ref, up_ref, un_ref, cw_ref):
    tm = u_ref.shape[0]
    u = u_ref[...]
    local = lax.broadcasted_iota(jnp.int32, (tm, 1), 0)
    row = local + pl.program_id(0) * tm
    in_ctx = row >= N_LAT
    first = ((row & (CTX_LEN - 1)) == 0) & (in_ctx | ((row & (SEQ - 1)) == 0))
    last = ((row & (CTX_LEN - 1)) == CTX_LEN - 1) & (in_ctx | ((row & (SEQ - 1)) == SEQ - 1))
    prev = jnp.where(local == 0, up_ref[HALO - 1:HALO, :], pltpu.roll(u, 1, axis=0))
    prev = jnp.where(first, 0.0, prev)
    nxt = jnp.where(local == tm - 1, un_ref[0:1, :], pltpu.roll(u, tm - 1, axis=0))
    nxt = jnp.where(last, 0.0, nxt)
    cw = cw_ref[...]
    z = prev * cw[0:1] + u * cw[1:2] + nxt * cw[2:3]
    return (b_ref[...].astype(F32) * z).astype(BF16)


def _gqa_proj_kernel(x_ref, mod_ref, g_ref, w_ref, qg_ref, kg_ref, ones_ref, cos_ref, sin_ref,
                     q_ref, k_ref, vt_ref):
    m = mod_ref[0]
    h = _modulated_norm(x_ref[...], g_ref[...], m[0:1], m[1:2]).astype(BF16)
    cos = cos_ref[...]
    sin = sin_ref[...]
    ones2 = ones_ref[...]

    def normed_rotated(y, gain):
        a, b = y[:, :LANES], y[:, LANES:]
        ss = a * a + b * b
        hi = ss.astype(BF16)
        lo = (ss - hi.astype(F32)).astype(BF16)
        tot = _dot(jnp.concatenate([hi, lo], axis=1), ones2)
        r = lax.rsqrt(tot * (1.0 / GQA_HEAD_DIM) + EPS)
        return _rope(a * r * gain[:, :LANES], b * r * gain[:, LANES:], cos, sin)

    wide = 2 * GQA_QK
    kv0 = _lane_mask(0, LANES // 2)
    for gg in range(GQA_GROUP // 2):
        y = _dot(h, w_ref[:, gg * wide:(gg + 1) * wide])
        for t in range(2):
            g = 2 * gg + t
            o1, o2 = normed_rotated(y[:, t * GQA_QK:(t + 1) * GQA_QK], qg_ref[...])
            for kv, keep in ((0, kv0), (1, jnp.logical_not(kv0))):
                c0 = (kv * GQA_GROUP + g) * GQA_QK
                q_ref[:, c0:c0 + LANES] = jnp.where(keep, o1, 0.0).astype(BF16)
                q_ref[:, c0 + LANES:c0 + GQA_QK] = jnp.where(keep, o2, 0.0).astype(BF16)
    y = _dot(h, w_ref[:, GQA_GROUP * GQA_QK:])
    o1, o2 = normed_rotated(y[:, :GQA_QK], kg_ref[...])
    k_ref[:, :LANES] = o1.astype(BF16)
    k_ref[:, LANES:] = o2.astype(BF16)
    _store_vt(vt_ref, y[:, GQA_QK:], GQA_KV_HEADS)


def _rope_spec(tm):
    per_seq = SEQ // tm
    return pl.BlockSpec(
        (tm, LANES), lambda i: (jnp.where(i < N_LAT // tm, i % per_seq, per_seq), 0))


def _gqa_proj(x, mod, gain, w_qkv, q_gain, k_gain, half_ones, cos, sin):
    kvw = GQA_KV_HEADS * GQA_HEAD_DIM
    qw = GQA_HEADS * GQA_QK
    return pl.pallas_call(
        _gqa_proj_kernel,
        grid=(N_ALL // TM,),
        in_specs=[
            _row_spec(TM, D), _mod_spec(TM), _resident((1, D)),
            _resident((D, D + 2 * kvw)),
            _resident((1, GQA_QK)), _resident((1, GQA_QK)), _resident((2 * LANES, LANES)),
            _rope_spec(TM), _rope_spec(TM),
        ],
        out_specs=[_row_spec(TM, qw), _row_spec(TM, GQA_QK), _col_spec(GQA_KV_HEADS * VT_ROWS, TM)],
        out_shape=[jax.ShapeDtypeStruct((N_ALL, qw), BF16),
                   jax.ShapeDtypeStruct((N_ALL, GQA_QK), BF16),
                   jax.ShapeDtypeStruct((GQA_KV_HEADS * VT_ROWS, N_ALL), BF16)],
        compiler_params=_params(1),
        name="gqa_proj",
    )(x, mod, gain, w_qkv, q_gain, k_gain, half_ones, cos, sin)


def _mla_proj_kernel(x_ref, mod_ref, g_ref, wdq_ref, qg_ref, wuq_ref, wdkv_ref, kvg_ref, wukv_ref,
                     cos_ref, sin_ref, q_ref, k_ref, vt_ref):
    m = mod_ref[0]
    h = _modulated_norm(x_ref[...], g_ref[...], m[0:1], m[1:2]).astype(BF16)
    cos = cos_ref[...]
    sin = sin_ref[...]
    half = MLA_ROPE // 2
    cq = _head_norm(_dot(h, wdq_ref[...]), qg_ref[...]).astype(BF16)
    q = _dot(cq, wuq_ref[...])
    ckv_pe = _dot(h, wdkv_ref[...])
    ckv = _head_norm(ckv_pe[:, :MLA_KV_RANK], kvg_ref[...]).astype(BF16)
    ko1, ko2 = _rope(ckv_pe[:, MLA_KV_RANK:MLA_KV_RANK + LANES], ckv_pe[:, MLA_KV_RANK + LANES:], cos, sin)
    k_pe = jnp.where(_lane_mask(0, 2 * half), ko1, ko2).astype(BF16)
    kv = _dot(ckv, wukv_ref[...])
    for pp in range(MLA_HEADS // 2):
        p0 = pp * 2 * MLA_QK
        o1, o2 = _rope(q[:, p0 + 2 * MLA_NOPE:p0 + 2 * MLA_NOPE + LANES], q[:, p0 + 2 * MLA_NOPE + LANES:p0 + 2 * MLA_QK],
                       cos, sin)
        for t in range(2):
            j = 2 * pp + t
            c0 = j * MLA_QK
            pe = jnp.where(_lane_mask(t * half, (t + 1) * half), o1,
                           jnp.where(_lane_mask(2 * half + t * half, 2 * half + (t + 1) * half), o2, 0.0))
            q_ref[:, c0:c0 + MLA_NOPE] = q[:, p0 + t * MLA_NOPE:p0 + (t + 1) * MLA_NOPE].astype(BF16)
            q_ref[:, c0 + MLA_NOPE:c0 + MLA_QK] = pe.astype(BF16)
            k_ref[:, c0:c0 + MLA_NOPE] = kv[:, j * MLA_NOPE:(j + 1) * MLA_NOPE].astype(BF16)
            k_ref[:, c0 + MLA_NOPE:c0 + MLA_QK] = k_pe
    _store_vt(vt_ref, kv[:, MLA_HEADS * MLA_NOPE:], MLA_HEADS)


def _mla_proj(x, mod, gain, w_dq, q_gain, w_uq, w_dkv, kv_gain, w_ukv, cos, sin):
    qkw = MLA_HEADS * MLA_QK
    return pl.pallas_call(
        _mla_proj_kernel,
        grid=(N_ALL // TM,),
        in_specs=[
            _row_spec(TM, D), _mod_spec(TM), _resident((1, D)),
            _resident((D, MLA_Q_RANK)), _resident((1, MLA_Q_RANK)), _resident((MLA_Q_RANK, qkw)),
            _resident((D, MLA_KV_RANK + 2 * LANES)), _resident((1, MLA_KV_RANK)),
            _resident((MLA_KV_RANK, MLA_HEADS * (MLA_NOPE + MLA_V))),
            _rope_spec(TM), _rope_spec(TM),
        ],
        out_specs=[_row_spec(TM, qkw), _row_spec(TM, qkw), _col_spec(MLA_HEADS * VT_ROWS, TM)],
        out_shape=[jax.ShapeDtypeStruct((N_ALL, qkw), BF16),
                   jax.ShapeDtypeStruct((N_ALL, qkw), BF16),
                   jax.ShapeDtypeStruct((MLA_HEADS * VT_ROWS, N_ALL), BF16)],
        compiler_params=_params(1),
        name="mla_proj",
    )(x, mod, gain, w_dq, q_gain, w_uq, w_dkv, kv_gain, w_ukv, cos, sin)


def _nt_dot(a, b):
    return lax.dot_general(a, b, (((1,), (1,)), ((), ())), preferred_element_type=F32)


def _attn_head(q, k_parts, vt_parts):
    return _attn_finish(_attn_scores(q, k_parts), vt_parts)


def _attn_scores(q, k_parts):
    return [_nt_dot(k, q) for k in k_parts]


def _attn_finish(scores, vt_parts):
    mx = functools.reduce(jnp.maximum, [jnp.max(s, axis=0, keepdims=True) for s in scores])
    probs = [jnp.exp2(s - mx) for s in scores]
    den = sum(jnp.sum(p, axis=0, keepdims=True) for p in probs)
    acc = sum(_dot(vt, p.astype(BF16)) for vt, p in zip(vt_parts, probs))
    return (acc / den).T


def _attn_lat_kernel(q_ref, kl_ref, kc_ref, vtl_ref, vtc_ref, o_ref, *, heads, group, dq, dv, shared_k):
    def scores(qh):
        kv = 0 if shared_k else qh // group
        ks = slice(kv * dq, (kv + 1) * dq)
        return _attn_scores(q_ref[:, qh * dq:(qh + 1) * dq], [kl_ref[:, ks], kc_ref[:, ks]])

    nxt = scores(0)
    for qh in range(heads * group):
        cur = nxt
        if qh + 1 < heads * group:
            nxt = scores(qh + 1)
        vs = slice(qh // group * VT_ROWS, (qh // group + 1) * VT_ROWS)
        out = _attn_finish(cur, [vtl_ref[vs, :], vtc_ref[vs, :]])
        o_ref[:, qh * dv:(qh + 1) * dv] = out.astype(BF16)


def _attn_ctx_kernel(q_ref, kc_ref, vtc_ref, prev_ref, o_ref, *, heads, group, dq, dv, shared_k):
    del prev_ref
    for kv in range(heads):
        ks = slice(0, dq) if shared_k else slice(kv * dq, (kv + 1) * dq)
        vs = slice(kv * VT_ROWS, (kv + 1) * VT_ROWS)
        for g in range(group):
            qh = kv * group + g
            out = _attn_head(q_ref[:, qh * dq:(qh + 1) * dq], [kc_ref[:, ks]], [vtc_ref[vs, :]])
            o_ref[:, qh * dv:(qh + 1) * dv] = out.astype(BF16)


def _attention(q, k, vt, *, kv_heads, group, dq, dv, heads_per_step, shared_k):
    hp = heads_per_step
    lat_tiles = SEQ // TQ
    ctx_block0 = N_LAT // CTX_LEN

    static = dict(heads=hp, group=group, dq=dq, dv=dv, shared_k=shared_k)
    kw = dq if shared_k else hp * dq
    k_col = (lambda h: 0) if shared_k else (lambda h: h)
    out_shape = jax.ShapeDtypeStruct((N_ALL, kv_heads * group * dv), BF16)

    lat = pl.pallas_call(
        functools.partial(_attn_lat_kernel, **static),
        grid=(BATCH, kv_heads // hp, lat_tiles),
        in_specs=[
            pl.BlockSpec((TQ, hp * group * dq), lambda b, h, t: (b * lat_tiles + t, h)),
            pl.BlockSpec((SEQ, kw), lambda b, h, t: (b, k_col(h))),
            pl.BlockSpec((CTX_LEN, kw), lambda b, h, t: (ctx_block0 + b, k_col(h))),
            pl.BlockSpec((hp * VT_ROWS, SEQ), lambda b, h, t: (h, b)),
            pl.BlockSpec((hp * VT_ROWS, CTX_LEN), lambda b, h, t: (h, ctx_block0 + b)),
        ],
        out_specs=pl.BlockSpec((TQ, hp * group * dv), lambda b, h, t: (b * lat_tiles + t, h)),
        out_shape=out_shape,
        compiler_params=_params(3),
        name="attention_lat",
    )(q, k, k, vt, vt)

    return pl.pallas_call(
        functools.partial(_attn_ctx_kernel, **static),
        grid=(BATCH, kv_heads // hp),
        in_specs=[
            pl.BlockSpec((CTX_LEN, hp * group * dq), lambda b, h: (ctx_block0 + b, h)),
            pl.BlockSpec((CTX_LEN, kw), lambda b, h: (ctx_block0 + b, k_col(h))),
            pl.BlockSpec((hp * VT_ROWS, CTX_LEN), lambda b, h: (h, ctx_block0 + b)),
            pl.BlockSpec(memory_space=pl.ANY),
        ],
        out_specs=pl.BlockSpec((CTX_LEN, hp * group * dv), lambda b, h: (ctx_block0 + b, h)),
        out_shape=out_shape,
        input_output_aliases={3: 0},
        compiler_params=_params(2),
        name="attention_ctx",
    )(q, k, vt, lat)


def _ffn_tail(x1, m, g2_ref, w1_ref, w3_ref, w2_ref):
    h = _modulated_norm(x1, g2_ref[...], m[3:4], m[4:5]).astype(BF16)
    a = _dot(h, _w(w1_ref))
    b = _dot(h, _w(w3_ref))
    gated = (a * jax.nn.sigmoid(a) * b).astype(BF16)
    return x1 + m[5:6] * _dot(gated, _w(w2_ref))


def _attn_tail_kernel(a_ref, wo_ref, x_ref, mod_ref, g2_ref, w1_ref, w3_ref, w2_ref, o_ref):
    m = mod_ref[0]
    x1 = x_ref[...] + m[2:3] * _dot(a_ref[...], _w(wo_ref))
    o_ref[...] = _ffn_tail(x1, m, g2_ref, w1_ref, w3_ref, w2_ref)


def _conv_tail_kernel(b_ref, u_ref, up_ref, un_ref, cw_ref, wo_ref, x_ref, mod_ref,
                      g2_ref, w1_ref, w3_ref, w2_ref, fg_ref, o_ref, *, final):
    m = mod_ref[0]
    x1 = x_ref[...] + m[2:3] * _dot(_conv_gate(b_ref, u_ref, up_ref, un_ref, cw_ref), _w(wo_ref))
    x2 = _ffn_tail(x1, m, g2_ref, w1_ref, w3_ref, w2_ref)
    o_ref[...] = _head_norm(x2, fg_ref[...]) if final else x2


def _ffn_specs(layer):
    return [_mod_spec(TM), _resident((1, D)), _resident((D, FFN_HIDDEN), layer),
            _resident((D, FFN_HIDDEN), layer), _resident((FFN_HIDDEN, D), layer)]


def _attn_tail(a, w_o, mixer_layer, x, mod, g2, w1, w3, w2, layer):
    return pl.pallas_call(
        _attn_tail_kernel,
        grid=(N_ALL // TM,),
        in_specs=[_row_spec(TM, D), _resident((D, D), mixer_layer), _row_spec(TM, D)] + _ffn_specs(layer),
        out_specs=_row_spec(TM, D),
        out_shape=jax.ShapeDtypeStruct((N_ALL, D), F32),
        compiler_params=_params(1),
        name="attn_tail",
    )(a, w_o, x, mod, g2, w1, w3, w2)


def _conv_tail(b, u, conv_w, w_out, mixer_layer, x, mod, g2, w1, w3, w2, layer, final_gain, n_rows, final):
    per = TM // HALO
    n_halo = n_rows // HALO
    return pl.pallas_call(
        functools.partial(_conv_tail_kernel, final=final),
        grid=(n_rows // TM,),
        in_specs=[
            _row_spec(TM, D), _row_spec(TM, D),
            pl.BlockSpec((HALO, D), lambda i: (jnp.maximum(i * per - 1, 0), 0)),
            pl.BlockSpec((HALO, D), lambda i: (jnp.minimum((i + 1) * per, n_halo - 1), 0)),
            _resident((3, D)), _resident((D, D), mixer_layer), _row_spec(TM, D),
        ] + _ffn_specs(layer) + [_resident((1, D))],
        out_specs=_row_spec(TM, D),
        out_shape=jax.ShapeDtypeStruct((n_rows, D), F32),
        compiler_params=_params(1),
        name="conv_tail",
    )(b, u, u, u, conv_w, w_out, x, mod, g2, w1, w3, w2, final_gain)


def _axial_angles(rot_dim):
    n = rot_dim // 4
    rows = jnp.repeat(jnp.arange(SEQ // GRID_W, dtype=F32), GRID_W)
    cols = jnp.tile(jnp.arange(GRID_W, dtype=F32), SEQ // GRID_W)
    freqs = ROPE_THETA ** (-jnp.arange(n, dtype=F32) / n)
    return jnp.concatenate([rows[:, None] * freqs, cols[:, None] * freqs], axis=-1)


def _rope_tables(rot_dim):
    ang = _axial_angles(rot_dim)
    copies = LANES // (rot_dim // 2)
    cos_t = jnp.concatenate([jnp.tile(jnp.cos(ang), (1, copies)), jnp.ones((TM, LANES), F32)], axis=0)
    sin_t = jnp.concatenate([jnp.tile(jnp.sin(ang), (1, copies)), jnp.zeros((TM, LANES), F32)], axis=0)
    return cos_t, sin_t


def _gqa_slab_pairs(w, n_kv, n_group):
    half = GQA_HEAD_DIM // 2
    w = w.reshape(w.shape[0], n_kv, n_group, 2, half)
    return jnp.transpose(w, (0, 2, 3, 1, 4)).reshape(w.shape[0], n_group * GQA_QK)


def _mla_weights(w_dq, w_uq, w_dkv, w_ukv):
    half = MLA_ROPE // 2
    qk = MLA_NOPE + MLA_ROPE
    uq = w_uq.reshape(MLA_Q_RANK, MLA_HEADS // 2, 2, qk)
    nope = uq[..., :MLA_NOPE].reshape(MLA_Q_RANK, MLA_HEADS // 2, 2 * MLA_NOPE)
    x1 = uq[..., MLA_NOPE:MLA_NOPE + half].reshape(MLA_Q_RANK, MLA_HEADS // 2, 2 * half)
    x2 = uq[..., MLA_NOPE + half:].reshape(MLA_Q_RANK, MLA_HEADS // 2, 2 * half)
    uq = jnp.concatenate([nope, x1, x1, x2, x2], axis=-1).reshape(MLA_Q_RANK, MLA_HEADS * MLA_QK)
    k1 = w_dkv[:, MLA_KV_RANK:MLA_KV_RANK + half]
    k2 = w_dkv[:, MLA_KV_RANK + half:]
    dkv = jnp.concatenate([w_dkv[:, :MLA_KV_RANK]] + [k1] * 4 + [k2] * 4, axis=-1)
    ukv = w_ukv.reshape(MLA_KV_RANK, MLA_HEADS, MLA_NOPE + MLA_V)
    ukv = jnp.concatenate([ukv[:, :, :MLA_NOPE].reshape(MLA_KV_RANK, -1),
                           ukv[:, :, MLA_NOPE:].reshape(MLA_KV_RANK, -1)], axis=-1)
    return w_dq.astype(BF16), uq.astype(BF16), dkv.astype(BF16), ukv.astype(BF16)


def kernel(x, c, ctx, c_ctx, ada_w, ada_b, norm1_g, norm2_g, ffn_w1, ffn_w3, ffn_w2, conv_w_in, conv_w, conv_w_out, gqa_wq, gqa_wk, gqa_wv, gqa_q_norm, gqa_k_norm, gqa_wo, mla_w_dq, mla_q_norm, mla_w_uq, mla_w_dkv, mla_kv_norm, mla_w_ukv, mla_wo, final_g):
    assert x.shape == (BATCH, SEQ, D) and ctx.shape == (BATCH, CTX_LEN, D)
    cond = jnp.concatenate(
        [c, c_ctx[None], jnp.zeros((MOD_ROWS - BATCH - 1, D), F32)], axis=0)
    mods = _modulation_tables(cond, ada_w, ada_b).reshape(DEPTH, MOD_ROWS, N_MOD, D)

    gqa_cos, gqa_sin = _rope_tables(GQA_HEAD_DIM)
    lane_half = jnp.arange(LANES) // (LANES // 2)
    half_ones = jnp.tile((lane_half[:, None] == lane_half[None, :]).astype(BF16), (2, 1))
    mla_cos, mla_sin = _rope_tables(MLA_ROPE)

    ffn_w1, ffn_w3, ffn_w2 = ffn_w1.astype(BF16), ffn_w3.astype(BF16), ffn_w2.astype(BF16)
    conv_w_in, conv_w_out = conv_w_in.astype(BF16), conv_w_out.astype(BF16)
    gqa_wo = gqa_wo.astype(BF16)
    mla_wo = mla_wo.astype(BF16)

    xs = None
    for i in range(DEPTH):
        kind, j = i % 3, i // 3
        mod = mods[i]
        last = i == DEPTH - 1
        n_rows = N_LAT if last else N_ALL
        g1 = norm1_g[i][None]
        ffn = (norm2_g[i][None], ffn_w1, ffn_w3, ffn_w2, i)
        if kind == 0:
            if i == 0:
                b, u, xs = _conv_in_first(x.reshape(N_LAT, D), ctx.reshape(N_CTX, D), mod, g1, conv_w_in, j)
            else:
                b, u = _conv_in(xs, mod, g1, conv_w_in, j, n_rows)
            xs = _conv_tail(b, u, conv_w[j], conv_w_out, j, xs, mod, *ffn, final_g[None], n_rows, last)
        elif kind == 1:
            w_qkv = jnp.concatenate([_gqa_slab_pairs(gqa_wq[j].astype(BF16), GQA_KV_HEADS, GQA_GROUP),
                                     _gqa_slab_pairs(gqa_wk[j].astype(BF16), GQA_KV_HEADS, 1),
                                     gqa_wv[j].astype(BF16)], axis=-1)
            q_gain = _gqa_slab_pairs(jnp.tile(gqa_q_norm[j], GQA_KV_HEADS)[None], GQA_KV_HEADS, 1)
            k_gain = _gqa_slab_pairs(jnp.tile(gqa_k_norm[j], GQA_KV_HEADS)[None], GQA_KV_HEADS, 1)
            q_gain = q_gain * (GQA_HEAD_DIM ** -0.5 * LOG2E)
            q, k, vt = _gqa_proj(xs, mod, g1, w_qkv, q_gain, k_gain, half_ones, gqa_cos, gqa_sin)
            a = _attention(q, k, vt, kv_heads=GQA_KV_HEADS, group=GQA_GROUP,
                           dq=GQA_QK, dv=GQA_HEAD_DIM, heads_per_step=1, shared_k=True)
            xs = _attn_tail(a, gqa_wo, j, xs, mod, *ffn)
        else:
            w_dq, w_uq, w_dkv, w_ukv = _mla_weights(mla_w_dq[j], mla_w_uq[j], mla_w_dkv[j], mla_w_ukv[j])
            q_gain = mla_q_norm[j][None] * ((MLA_NOPE + MLA_ROPE) ** -0.5 * LOG2E)
            q, k, vt = _mla_proj(xs, mod, g1, w_dq, q_gain, w_uq, w_dkv,
                                 mla_kv_norm[j][None], w_ukv, mla_cos, mla_sin)
            a = _attention(q, k, vt, kv_heads=MLA_HEADS, group=1, dq=MLA_QK, dv=MLA_V, heads_per_step=4,
                           shared_k=False)
            xs = _attn_tail(a, mla_wo, j, xs, mod, *ffn)

    assert (DEPTH - 1) % 3 == 0
    return xs.reshape(BATCH, SEQ, D)
```

```python
import functools

import jax
import jax.numpy as jnp
import numpy as np
from jax import lax
from jax.experimental import pallas as pl
from jax.experimental.pallas import tpu as pltpu

D = 1024
BATCH = 8
SEQ = 2048
DEPTH = 4
GRID_W = 64
CTX_LEN = 256
N_MOD = 6
FFN_HIDDEN = 2816
ROPE_THETA = 10000.0
EPS = 1e-6
GQA_HEAD_DIM = 128
GQA_HEADS = 8
GQA_KV_HEADS = 2
GQA_GROUP = 4
MLA_HEADS = 8
MLA_NOPE = 128
MLA_ROPE = 64
MLA_V = 128
MLA_KV_RANK = 256
MLA_Q_RANK = 768
MLA_QK = 256
GQA_QK = 256
LANES = 128
VT_ROWS = 128

N_LAT = BATCH * SEQ
N_CTX = BATCH * CTX_LEN
N_ALL = N_LAT + N_CTX
MOD_ROWS = 16

TM = 512
TP = 1024
TQ = 1024
HALO = 8
VMEM_LIMIT = 56 * 1024 * 1024
LOG2E = 1.4426950408889634

F32 = jnp.float32
BF16 = jnp.bfloat16


def _params(n_axes):
    return pltpu.CompilerParams(
        dimension_semantics=("arbitrary",) * n_axes, vmem_limit_bytes=VMEM_LIMIT)


def _resident(shape, layer=None):
    nd = len(shape)
    if layer is None:
        return pl.BlockSpec(shape, lambda *_: (0,) * nd, pipeline_mode=pl.Buffered(1))
    return pl.BlockSpec((1,) + tuple(shape), lambda *_: (layer,) + (0,) * nd, pipeline_mode=pl.Buffered(1))


def _w(ref):
    return ref[0] if len(ref.shape) == 3 else ref[...]


def _mod_spec(tm):
    return pl.BlockSpec((1, N_MOD, D), lambda i: (jnp.minimum(i * tm // SEQ, BATCH), 0, 0))


def _row_spec(tm, width):
    return pl.BlockSpec((tm, width), lambda i: (i, 0))


def _col_spec(height, tm):
    return pl.BlockSpec((height, tm), lambda i: (0, i))


def _dot(a, b):
    return jnp.dot(a, b, preferred_element_type=F32)


def _modulated_norm(x, gain, shift, scale):
    r = lax.rsqrt(jnp.mean(x * x, axis=-1, keepdims=True) + EPS)
    return (x * r) * (gain * (1.0 + scale)) + shift


def _head_norm(x, gain):
    r = lax.rsqrt(jnp.mean(x * x, axis=-1, keepdims=True) + EPS)
    return (x * r) * gain


def _store_vt(vt_ref, v, n_heads):
    for j in range(n_heads):
        vt_ref[j * VT_ROWS:(j + 1) * VT_ROWS, :] = v[:, j * VT_ROWS:(j + 1) * VT_ROWS].T.astype(BF16)


def _rope(a, b, cos, sin):
    return a * cos - b * sin, a * sin + b * cos


def _lane_mask(lo, hi):
    lane = lax.broadcasted_iota(jnp.int32, (1, LANES), 1)
    return (lane >= lo) & (lane < hi)


def _mod_kernel(cond_ref, w_ref, b_ref, o_ref):
    c = cond_ref[...]
    act = (c * jax.nn.sigmoid(c)).astype(BF16)
    o_ref[0] = _dot(act, w_ref[0].astype(BF16)) + b_ref[0]


def _modulation_tables(cond, ada_w, ada_b):
    tn = 1536
    return pl.pallas_call(
        _mod_kernel,
        grid=(DEPTH, N_MOD * D // tn),
        in_specs=[
            pl.BlockSpec((MOD_ROWS, D), lambda l, j: (0, 0)),
            pl.BlockSpec((1, D, tn), lambda l, j: (l, 0, j)),
            pl.BlockSpec((1, 1, tn), lambda l, j: (l, 0, j)),
        ],
        out_specs=pl.BlockSpec((1, MOD_ROWS, tn), lambda l, j: (l, 0, j)),
        out_shape=jax.ShapeDtypeStruct((DEPTH, MOD_ROWS, N_MOD * D), F32),
        compiler_params=_params(2),
        name="modulation",
    )(cond, ada_w, ada_b.reshape(DEPTH, 1, N_MOD * D))


def _conv_in_compute(x, mod_ref, g_ref, w_ref, b_ref, u_ref):
    m = mod_ref[0]
    h = _modulated_norm(x, g_ref[...], m[0:1], m[1:2]).astype(BF16)
    y = _dot(h, _w(w_ref))
    b_ref[...] = y[:, :D].astype(BF16)
    u_ref[...] = y[:, D:2 * D] * y[:, 2 * D:]


def _conv_in_kernel(x_ref, mod_ref, g_ref, w_ref, b_ref, u_ref):
    _conv_in_compute(x_ref[...], mod_ref, g_ref, w_ref, b_ref, u_ref)


def _conv_in_first_kernel(lat_ref, ctx_ref, mod_ref, g_ref, w_ref, b_ref, u_ref, xs_ref):
    x = jnp.where(pl.program_id(0) < N_LAT // TP, lat_ref[...], ctx_ref[...])
    xs_ref[...] = x
    _conv_in_compute(x, mod_ref, g_ref, w_ref, b_ref, u_ref)


def _conv_in(x, mod, gain, w_in, layer, n_rows):
    return pl.pallas_call(
        _conv_in_kernel,
        grid=(n_rows // TP,),
        in_specs=[_row_spec(TP, D), _mod_spec(TP), _resident((1, D)), _resident((D, 3 * D), layer)],
        out_specs=[_row_spec(TP, D), _row_spec(TP, D)],
        out_shape=[jax.ShapeDtypeStruct((n_rows, D), BF16), jax.ShapeDtypeStruct((n_rows, D), F32)],
        compiler_params=_params(1),
        name="conv_in",
    )(x, mod, gain, w_in)


def _conv_in_first(lat, ctx, mod, gain, w_in, layer):
    lat_tiles = N_LAT // TP
    return pl.pallas_call(
        _conv_in_first_kernel,
        grid=(N_ALL // TP,),
        in_specs=[
            pl.BlockSpec((TP, D), lambda i: (jnp.minimum(i, lat_tiles - 1), 0)),
            pl.BlockSpec((TP, D), lambda i: (jnp.maximum(i - lat_tiles, 0), 0)),
            _mod_spec(TP), _resident((1, D)), _resident((D, 3 * D), layer),
        ],
        out_specs=[_row_spec(TP, D), _row_spec(TP, D), _row_spec(TP, D)],
        out_shape=[jax.ShapeDtypeStruct((N_ALL, D), BF16), jax.ShapeDtypeStruct((N_ALL, D), F32),
                   jax.ShapeDtypeStruct((N_ALL, D), F32)],
        compiler_params=_params(1),
        name="conv_in_first",
    )(lat, ctx, mod, gain, w_in)


def _conv_gate(b_ref, u_ref, up_ref, un_ref, cw_ref):
    tm = u_ref.shape[0]
    u = u_ref[...]
    local = lax.broadcasted_iota(jnp.int32, (tm, 1), 0)
    row = local + pl.program_id(0) * tm
    in_ctx = row >= N_LAT
    first = ((row & (CTX_LEN - 1)) == 0) & (in_ctx | ((row & (SEQ - 1)) == 0))
    last = ((row & (CTX_LEN - 1)) == CTX_LEN - 1) & (in_ctx | ((row & (SEQ - 1)) == SEQ - 1))
    prev = jnp.where(local == 0, up_ref[HALO - 1:HALO, :], pltpu.roll(u, 1, axis=0))
    prev = jnp.where(first, 0.0, prev)
    nxt = jnp.where(local == tm - 1, un_ref[0:1, :], pltpu.roll(u, tm - 1, axis=0))
    nxt = jnp.where(last, 0.0, nxt)
    cw = cw_ref[...]
    z = prev * cw[0:1] + u * cw[1:2] + nxt * cw[2:3]
    return (b_ref[...].astype(F32) * z).astype(BF16)


def _gqa_proj_kernel(x_ref, mod_ref, g_ref, w_ref, qg_ref, kg_ref, ones_ref, cos_ref, sin_ref,
                     q_ref, k_ref, vt_ref):
    m = mod_ref[0]
    h = _modulated_norm(x_ref[...], g_ref[...], m[0:1], m[1:2]).astype(BF16)
    cos = cos_ref[...]
    sin = sin_ref[...]
    ones2 = ones_ref[...]

    def normed_rotated(y, gain):
        a, b = y[:, :LANES], y[:, LANES:]
        ss = a * a + b * b
        hi = ss.astype(BF16)
        lo = (ss - hi.astype(F32)).astype(BF16)
        tot = _dot(jnp.concatenate([hi, lo], axis=1), ones2)
        r = lax.rsqrt(tot * (1.0 / GQA_HEAD_DIM) + EPS)
        return _rope(a * r * gain[:, :LANES], b * r * gain[:, LANES:], cos, sin)

    wide = 2 * GQA_QK
    kv0 = _lane_mask(0, LANES // 2)
    for gg in range(GQA_GROUP // 2):
        y = _dot(h, w_ref[:, gg * wide:(gg + 1) * wide])
        for t in range(2):
            g = 2 * gg + t
            o1, o2 = normed_rotated(y[:, t * GQA_QK:(t + 1) * GQA_QK], qg_ref[...])
            for kv, keep in ((0, kv0), (1, jnp.logical_not(kv0))):
                c0 = (kv * GQA_GROUP + g) * GQA_QK
                q_ref[:, c0:c0 + LANES] = jnp.where(keep, o1, 0.0).astype(BF16)
                q_ref[:, c0 + LANES:c0 + GQA_QK] = jnp.where(keep, o2, 0.0).astype(BF16)
    y = _dot(h, w_ref[:, GQA_GROUP * GQA_QK:])
    o1, o2 = normed_rotated(y[:, :GQA_QK], kg_ref[...])
    k_ref[:, :LANES] = o1.astype(BF16)
    k_ref[:, LANES:] = o2.astype(BF16)
    _store_vt(vt_ref, y[:, GQA_QK:], GQA_KV_HEADS)


def _rope_spec(tm):
    per_seq = SEQ // tm
    return pl.BlockSpec(
        (tm, LANES), lambda i: (jnp.where(i < N_LAT // tm, i % per_seq, per_seq), 0))


def _gqa_proj(x, mod, gain, w_qkv, q_gain, k_gain, half_ones, cos, sin):
    kvw = GQA_KV_HEADS * GQA_HEAD_DIM
    qw = GQA_HEADS * GQA_QK
    return pl.pallas_call(
        _gqa_proj_kernel,
        grid=(N_ALL // TP,),
        in_specs=[
            _row_spec(TP, D), _mod_spec(TP), _resident((1, D)),
            _resident((D, D + 2 * kvw)),
            _resident((1, GQA_QK)), _resident((1, GQA_QK)), _resident((2 * LANES, LANES)),
            _rope_spec(TP), _rope_spec(TP),
        ],
        out_specs=[_row_spec(TP, qw), _row_spec(TP, GQA_QK), _col_spec(GQA_KV_HEADS * VT_ROWS, TP)],
        out_shape=[jax.ShapeDtypeStruct((N_ALL, qw), BF16),
                   jax.ShapeDtypeStruct((N_ALL, GQA_QK), BF16),
                   jax.ShapeDtypeStruct((GQA_KV_HEADS * VT_ROWS, N_ALL), BF16)],
        compiler_params=_params(1),
        name="gqa_proj",
    )(x, mod, gain, w_qkv, q_gain, k_gain, half_ones, cos, sin)


def _mla_proj_kernel(x_ref, mod_ref, g_ref, wdq_ref, qg_ref, wuq_ref, wdkv_ref, kvg_ref, wukv_ref,
                     cos_ref, sin_ref, q_ref, k_ref, vt_ref):
    m = mod_ref[0]
    h = _modulated_norm(x_ref[...], g_ref[...], m[0:1], m[1:2]).astype(BF16)
    cos = cos_ref[...]
    sin = sin_ref[...]
    half = MLA_ROPE // 2
    cq = _head_norm(_dot(h, wdq_ref[...]), qg_ref[...]).astype(BF16)
    q = _dot(cq, wuq_ref[...])
    ckv_pe = _dot(h, wdkv_ref[...])
    ckv = _head_norm(ckv_pe[:, :MLA_KV_RANK], kvg_ref[...]).astype(BF16)
    ko1, ko2 = _rope(ckv_pe[:, MLA_KV_RANK:MLA_KV_RANK + LANES], ckv_pe[:, MLA_KV_RANK + LANES:], cos, sin)
    k_pe = jnp.where(_lane_mask(0, 2 * half), ko1, ko2).astype(BF16)
    kv = _dot(ckv, wukv_ref[...])
    for pp in range(MLA_HEADS // 2):
        p0 = pp * 2 * MLA_QK
        o1, o2 = _rope(q[:, p0 + 2 * MLA_NOPE:p0 + 2 * MLA_NOPE + LANES], q[:, p0 + 2 * MLA_NOPE + LANES:p0 + 2 * MLA_QK],
                       cos, sin)
        for t in range(2):
            j = 2 * pp + t
            c0 = j * MLA_QK
            pe = jnp.where(_lane_mask(t * half, (t + 1) * half), o1,
                           jnp.where(_lane_mask(2 * half + t * half, 2 * half + (t + 1) * half), o2, 0.0))
            q_ref[:, c0:c0 + MLA_NOPE] = q[:, p0 + t * MLA_NOPE:p0 + (t + 1) * MLA_NOPE].astype(BF16)
            q_ref[:, c0 + MLA_NOPE:c0 + MLA_QK] = pe.astype(BF16)
            k_ref[:, c0:c0 + MLA_NOPE] = kv[:, j * MLA_NOPE:(j + 1) * MLA_NOPE].astype(BF16)
            k_ref[:, c0 + MLA_NOPE:c0 + MLA_QK] = k_pe
    _store_vt(vt_ref, kv[:, MLA_HEADS * MLA_NOPE:], MLA_HEADS)


def _mla_proj(x, mod, gain, w_dq, q_gain, w_uq, w_dkv, kv_gain, w_ukv, cos, sin):
    qkw = MLA_HEADS * MLA_QK
    return pl.pallas_call(
        _mla_proj_kernel,
        grid=(N_ALL // TP,),
        in_specs=[
            _row_spec(TP, D), _mod_spec(TP), _resident((1, D)),
            _resident((D, MLA_Q_RANK)), _resident((1, MLA_Q_RANK)), _resident((MLA_Q_RANK, qkw)),
            _resident((D, MLA_KV_RANK + 2 * LANES)), _resident((1, MLA_KV_RANK)),
            _resident((MLA_KV_RANK, MLA_HEADS * (MLA_NOPE + MLA_V))),
            _rope_spec(TP), _rope_spec(TP),
        ],
        out_specs=[_row_spec(TP, qkw), _row_spec(TP, qkw), _col_spec(MLA_HEADS * VT_ROWS, TP)],
        out_shape=[jax.ShapeDtypeStruct((N_ALL, qkw), BF16),
                   jax.ShapeDtypeStruct((N_ALL, qkw), BF16),
                   jax.ShapeDtypeStruct((MLA_HEADS * VT_ROWS, N_ALL), BF16)],
        compiler_params=_params(1),
        name="mla_proj",
    )(x, mod, gain, w_dq, q_gain, w_uq, w_dkv, kv_gain, w_ukv, cos, sin)


def _nt_dot(a, b):
    return lax.dot_general(a, b, (((1,), (1,)), ((), ())), preferred_element_type=F32)


def _attn_head(q, k_parts, vt_parts):
    return _attn_finish(_attn_scores(q, k_parts), vt_parts)


def _attn_scores(q, k_parts):
    return [_nt_dot(k, q) for k in k_parts]


def _attn_finish(scores, vt_parts):
    mx = functools.reduce(jnp.maximum, [jnp.max(s, axis=0, keepdims=True) for s in scores])
    probs = [jnp.exp2(s - mx) for s in scores]
    den = sum(jnp.sum(p, axis=0, keepdims=True) for p in probs)
    acc = sum(_dot(vt, p.astype(BF16)) for vt, p in zip(vt_parts, probs))
    return (acc / den).T


def _attn_lat_kernel(q_ref, kl_ref, kc_ref, vtl_ref, vtc_ref, o_ref, *, heads, group, dq, dv, shared_k):
    def scores(qh):
        kv = 0 if shared_k else qh // group
        ks = slice(kv * dq, (kv + 1) * dq)
        return _attn_scores(q_ref[:, qh * dq:(qh + 1) * dq], [kl_ref[:, ks], kc_ref[:, ks]])

    n = heads * group
    pending = [scores(0), scores(1)] if n > 1 else [scores(0)]
    for qh in range(n):
        cur = pending.pop(0)
        mx = functools.reduce(jnp.maximum, [jnp.max(s, axis=0, keepdims=True) for s in cur])
        probs = [jnp.exp2(s - mx) for s in cur]
        den = sum(jnp.sum(p, axis=0, keepdims=True) for p in probs)
        probs = [p.astype(BF16) for p in probs]
        if qh + 2 < n:
            pending.append(scores(qh + 2))
        vs = slice(qh // group * VT_ROWS, (qh // group + 1) * VT_ROWS)
        acc = sum(_dot(vt, p) for vt, p in zip([vtl_ref[vs, :], vtc_ref[vs, :]], probs))
        o_ref[:, qh * dv:(qh + 1) * dv] = (acc / den).T.astype(BF16)


def _attn_ctx_kernel(q_ref, kc_ref, vtc_ref, prev_ref, o_ref, *, heads, group, dq, dv, shared_k):
    del prev_ref
    for kv in range(heads):
        ks = slice(0, dq) if shared_k else slice(kv * dq, (kv + 1) * dq)
        vs = slice(kv * VT_ROWS, (kv + 1) * VT_ROWS)
        for g in range(group):
            qh = kv * group + g
            out = _attn_head(q_ref[:, qh * dq:(qh + 1) * dq], [kc_ref[:, ks]], [vtc_ref[vs, :]])
            o_ref[:, qh * dv:(qh + 1) * dv] = out.astype(BF16)


def _attention(q, k, vt, *, kv_heads, group, dq, dv, heads_per_step, shared_k):
    hp = heads_per_step
    lat_tiles = SEQ // TQ
    ctx_block0 = N_LAT // CTX_LEN

    static = dict(heads=hp, group=group, dq=dq, dv=dv, shared_k=shared_k)
    kw = dq if shared_k else hp * dq
    k_col = (lambda h: 0) if shared_k else (lambda h: h)
    out_shape = jax.ShapeDtypeStruct((N_ALL, kv_heads * group * dv), BF16)

    lat = pl.pallas_call(
        functools.partial(_attn_lat_kernel, **static),
        grid=(BATCH, kv_heads // hp, lat_tiles),
        in_specs=[
            pl.BlockSpec((TQ, hp * group * dq), lambda b, h, t: (b * lat_tiles + t, h)),
            pl.BlockSpec((SEQ, kw), lambda b, h, t: (b, k_col(h))),
            pl.BlockSpec((CTX_LEN, kw), lambda b, h, t: (ctx_block0 + b, k_col(h))),
            pl.BlockSpec((hp * VT_ROWS, SEQ), lambda b, h, t: (h, b)),
            pl.BlockSpec((hp * VT_ROWS, CTX_LEN), lambda b, h, t: (h, ctx_block0 + b)),
        ],
        out_specs=pl.BlockSpec((TQ, hp * group * dv), lambda b, h, t: (b * lat_tiles + t, h)),
        out_shape=out_shape,
        compiler_params=_params(3),
        name="attention_lat",
    )(q, k, k, vt, vt)

    return pl.pallas_call(
        functools.partial(_attn_ctx_kernel, **static),
        grid=(BATCH, kv_heads // hp),
        in_specs=[
            pl.BlockSpec((CTX_LEN, hp * group * dq), lambda b, h: (ctx_block0 + b, h)),
            pl.BlockSpec((CTX_LEN, kw), lambda b, h: (ctx_block0 + b, k_col(h))),
            pl.BlockSpec((hp * VT_ROWS, CTX_LEN), lambda b, h: (h, ctx_block0 + b)),
            pl.BlockSpec(memory_space=pl.ANY),
        ],
        out_specs=pl.BlockSpec((CTX_LEN, hp * group * dv), lambda b, h: (ctx_block0 + b, h)),
        out_shape=out_shape,
        input_output_aliases={3: 0},
        compiler_params=_params(2),
        name="attention_ctx",
    )(q, k, vt, lat)


def _ffn_tail(x1, m, g2_ref, w1_ref, w3_ref, w2_ref):
    h = _modulated_norm(x1, g2_ref[...], m[3:4], m[4:5]).astype(BF16)
    a = _dot(h, _w(w1_ref))
    b = _dot(h, _w(w3_ref))
    gated = (a * jax.nn.sigmoid(a) * b).astype(BF16)
    return x1 + m[5:6] * _dot(gated, _w(w2_ref))


def _attn_tail_kernel(a_ref, wo_ref, x_ref, mod_ref, g2_ref, w1_ref, w3_ref, w2_ref, o_ref):
    m = mod_ref[0]
    x1 = x_ref[...] + m[2:3] * _dot(a_ref[...], _w(wo_ref))
    o_ref[...] = _ffn_tail(x1, m, g2_ref, w1_ref, w3_ref, w2_ref)


def _conv_tail_kernel(b_ref, u_ref, up_ref, un_ref, cw_ref, wo_ref, x_ref, mod_ref,
                      g2_ref, w1_ref, w3_ref, w2_ref, fg_ref, o_ref, *, final):
    m = mod_ref[0]
    x1 = x_ref[...] + m[2:3] * _dot(_conv_gate(b_ref, u_ref, up_ref, un_ref, cw_ref), _w(wo_ref))
    x2 = _ffn_tail(x1, m, g2_ref, w1_ref, w3_ref, w2_ref)
    o_ref[...] = _head_norm(x2, fg_ref[...]) if final else x2


def _ffn_specs(layer):
    return [_mod_spec(TM), _resident((1, D)), _resident((D, FFN_HIDDEN), layer),
            _resident((D, FFN_HIDDEN), layer), _resident((FFN_HIDDEN, D), layer)]


def _attn_tail(a, w_o, mixer_layer, x, mod, g2, w1, w3, w2, layer):
    return pl.pallas_call(
        _attn_tail_kernel,
        grid=(N_ALL // TM,),
        in_specs=[_row_spec(TM, D), _resident((D, D), mixer_layer), _row_spec(TM, D)] + _ffn_specs(layer),
        out_specs=_row_spec(TM, D),
        out_shape=jax.ShapeDtypeStruct((N_ALL, D), F32),
        compiler_params=_params(1),
        name="attn_tail",
    )(a, w_o, x, mod, g2, w1, w3, w2)


def _conv_tail(b, u, conv_w, w_out, mixer_layer, x, mod, g2, w1, w3, w2, layer, final_gain, n_rows, final):
    per = TM // HALO
    n_halo = n_rows // HALO
    return pl.pallas_call(
        functools.partial(_conv_tail_kernel, final=final),
        grid=(n_rows // TM,),
        in_specs=[
            _row_spec(TM, D), _row_spec(TM, D),
            pl.BlockSpec((HALO, D), lambda i: (jnp.maximum(i * per - 1, 0), 0)),
            pl.BlockSpec((HALO, D), lambda i: (jnp.minimum((i + 1) * per, n_halo - 1), 0)),
            _resident((3, D)), _resident((D, D), mixer_layer), _row_spec(TM, D),
        ] + _ffn_specs(layer) + [_resident((1, D))],
        out_specs=_row_spec(TM, D),
        out_shape=jax.ShapeDtypeStruct((n_rows, D), F32),
        compiler_params=_params(1),
        name="conv_tail",
    )(b, u, u, u, conv_w, w_out, x, mod, g2, w1, w3, w2, final_gain)


def _axial_angles(rot_dim):
    n = rot_dim // 4
    rows = jnp.repeat(jnp.arange(SEQ // GRID_W, dtype=F32), GRID_W)
    cols = jnp.tile(jnp.arange(GRID_W, dtype=F32), SEQ // GRID_W)
    freqs = ROPE_THETA ** (-jnp.arange(n, dtype=F32) / n)
    return jnp.concatenate([rows[:, None] * freqs, cols[:, None] * freqs], axis=-1)


def _rope_tables(rot_dim):
    ang = _axial_angles(rot_dim)
    copies = LANES // (rot_dim // 2)
    cos_t = jnp.concatenate([jnp.tile(jnp.cos(ang), (1, copies)), jnp.ones((TP, LANES), F32)], axis=0)
    sin_t = jnp.concatenate([jnp.tile(jnp.sin(ang), (1, copies)), jnp.zeros((TP, LANES), F32)], axis=0)
    return cos_t, sin_t


def _gqa_slab_pairs(w, n_kv, n_group):
    half = GQA_HEAD_DIM // 2
    w = w.reshape(w.shape[0], n_kv, n_group, 2, half)
    return jnp.transpose(w, (0, 2, 3, 1, 4)).reshape(w.shape[0], n_group * GQA_QK)


def _mla_weights(w_dq, w_uq, w_dkv, w_ukv):
    half = MLA_ROPE // 2
    qk = MLA_NOPE + MLA_ROPE
    uq = w_uq.reshape(MLA_Q_RANK, MLA_HEADS // 2, 2, qk)
    nope = uq[..., :MLA_NOPE].reshape(MLA_Q_RANK, MLA_HEADS // 2, 2 * MLA_NOPE)
    x1 = uq[..., MLA_NOPE:MLA_NOPE + half].reshape(MLA_Q_RANK, MLA_HEADS // 2, 2 * half)
    x2 = uq[..., MLA_NOPE + half:].reshape(MLA_Q_RANK, MLA_HEADS // 2, 2 * half)
    uq = jnp.concatenate([nope, x1, x1, x2, x2], axis=-1).reshape(MLA_Q_RANK, MLA_HEADS * MLA_QK)
    k1 = w_dkv[:, MLA_KV_RANK:MLA_KV_RANK + half]
    k2 = w_dkv[:, MLA_KV_RANK + half:]
    dkv = jnp.concatenate([w_dkv[:, :MLA_KV_RANK]] + [k1] * 4 + [k2] * 4, axis=-1)
    ukv = w_ukv.reshape(MLA_KV_RANK, MLA_HEADS, MLA_NOPE + MLA_V)
    ukv = jnp.concatenate([ukv[:, :, :MLA_NOPE].reshape(MLA_KV_RANK, -1),
                           ukv[:, :, MLA_NOPE:].reshape(MLA_KV_RANK, -1)], axis=-1)
    return w_dq.astype(BF16), uq.astype(BF16), dkv.astype(BF16), ukv.astype(BF16)


def kernel(x, c, ctx, c_ctx, ada_w, ada_b, norm1_g, norm2_g, ffn_w1, ffn_w3, ffn_w2, conv_w_in, conv_w, conv_w_out, gqa_wq, gqa_wk, gqa_wv, gqa_q_norm, gqa_k_norm, gqa_wo, mla_w_dq, mla_q_norm, mla_w_uq, mla_w_dkv, mla_kv_norm, mla_w_ukv, mla_wo, final_g):
    assert x.shape == (BATCH, SEQ, D) and ctx.shape == (BATCH, CTX_LEN, D)
    cond = jnp.concatenate(
        [c, c_ctx[None], jnp.zeros((MOD_ROWS - BATCH - 1, D), F32)], axis=0)
    mods = _modulation_tables(cond, ada_w, ada_b).reshape(DEPTH, MOD_ROWS, N_MOD, D)

    gqa_cos, gqa_sin = _rope_tables(GQA_HEAD_DIM)
    lane_half = jnp.arange(LANES) // (LANES // 2)
    half_ones = jnp.tile((lane_half[:, None] == lane_half[None, :]).astype(BF16), (2, 1))
    mla_cos, mla_sin = _rope_tables(MLA_ROPE)

    ffn_w1, ffn_w3, ffn_w2 = ffn_w1.astype(BF16), ffn_w3.astype(BF16), ffn_w2.astype(BF16)
    conv_w_in, conv_w_out = conv_w_in.astype(BF16), conv_w_out.astype(BF16)
    gqa_wo = gqa_wo.astype(BF16)
    mla_wo = mla_wo.astype(BF16)

    xs = None
    for i in range(DEPTH):
        kind, j = i % 3, i // 3
        mod = mods[i]
        last = i == DEPTH - 1
        n_rows = N_LAT if last else N_ALL
        g1 = norm1_g[i][None]
        ffn = (norm2_g[i][None], ffn_w1, ffn_w3, ffn_w2, i)
        if kind == 0:
            if i == 0:
                b, u, xs = _conv_in_first(x.reshape(N_LAT, D), ctx.reshape(N_CTX, D), mod, g1, conv_w_in, j)
            else:
                b, u = _conv_in(xs, mod, g1, conv_w_in, j, n_rows)
            xs = _conv_tail(b, u, conv_w[j], conv_w_out, j, xs, mod, *ffn, final_g[None], n_rows, last)
        elif kind == 1:
            w_qkv = jnp.concatenate([_gqa_slab_pairs(gqa_wq[j].astype(BF16), GQA_KV_HEADS, GQA_GROUP),
                                     _gqa_slab_pairs(gqa_wk[j].astype(BF16), GQA_KV_HEADS, 1),
                                     gqa_wv[j].astype(BF16)], axis=-1)
            q_gain = _gqa_slab_pairs(jnp.tile(gqa_q_norm[j], GQA_KV_HEADS)[None], GQA_KV_HEADS, 1)
            k_gain = _gqa_slab_pairs(jnp.tile(gqa_k_norm[j], GQA_KV_HEADS)[None], GQA_KV_HEADS, 1)
            q_gain = q_gain * (GQA_HEAD_DIM ** -0.5 * LOG2E)
            q, k, vt = _gqa_proj(xs, mod, g1, w_qkv, q_gain, k_gain, half_ones, gqa_cos, gqa_sin)
            a = _attention(q, k, vt, kv_heads=GQA_KV_HEADS, group=GQA_GROUP,
                           dq=GQA_QK, dv=GQA_HEAD_DIM, heads_per_step=1, shared_k=True)
            xs = _attn_tail(a, gqa_wo, j, xs, mod, *ffn)
        else:
            w_dq, w_uq, w_dkv, w_ukv = _mla_weights(mla_w_dq[j], mla_w_uq[j], mla_w_dkv[j], mla_w_ukv[j])
            q_gain = mla_q_norm[j][None] * ((MLA_NOPE + MLA_ROPE) ** -0.5 * LOG2E)
            q, k, vt = _mla_proj(xs, mod, g1, w_dq, q_gain, w_uq, w_dkv,
                                 mla_kv_norm[j][None], w_ukv, mla_cos, mla_sin)
            a = _attention(q, k, vt, kv_heads=MLA_HEADS, group=1, dq=MLA_QK, dv=MLA_V, heads_per_step=4,
                           shared_k=False)
            xs = _attn_tail(a, mla_wo, j, xs, mod, *ffn)

    assert (DEPTH - 1) % 3 == 0
    return xs.reshape(BATCH, SEQ, D)
```

```python
import functools

import jax
import jax.numpy as jnp
import numpy as np
from jax import lax
from jax.experimental import pallas as pl
from jax.experimental.pallas import tpu as pltpu

D = 1024
BATCH = 8
SEQ = 2048
DEPTH = 4
GRID_W = 64
CTX_LEN = 256
N_MOD = 6
FFN_HIDDEN = 2816
ROPE_THETA = 10000.0
EPS = 1e-6
GQA_HEAD_DIM = 128
GQA_HEADS = 8
GQA_KV_HEADS = 2
GQA_GROUP = 4
MLA_HEADS = 8
MLA_NOPE = 128
MLA_ROPE = 64
MLA_V = 128
MLA_KV_RANK = 256
MLA_Q_RANK = 768
MLA_QK = 256
GQA_QK = 256
LANES = 128
VT_ROWS = 128

N_LAT = BATCH * SEQ
N_CTX = BATCH * CTX_LEN
N_ALL = N_LAT + N_CTX
MOD_ROWS = 16

TM = 512
TP = 1024
TQ = 1024
HALO = 8
VMEM_LIMIT = 56 * 1024 * 1024
LOG2E = 1.4426950408889634

F32 = jnp.float32
BF16 = jnp.bfloat16


def _params(n_axes):
    return pltpu.CompilerParams(
        dimension_semantics=("arbitrary",) * n_axes, vmem_limit_bytes=VMEM_LIMIT)


def _resident(shape, layer=None):
    nd = len(shape)
    if layer is None:
        return pl.BlockSpec(shape, lambda *_: (0,) * nd, pipeline_mode=pl.Buffered(1))
    return pl.BlockSpec((1,) + tuple(shape), lambda *_: (layer,) + (0,) * nd, pipeline_mode=pl.Buffered(1))


def _w(ref):
    return ref[0] if len(ref.shape) == 3 else ref[...]


def _mod_spec(tm):
    return pl.BlockSpec((1, N_MOD, D), lambda i: (jnp.minimum(i * tm // SEQ, BATCH), 0, 0))


def _row_spec(tm, width):
    return pl.BlockSpec((tm, width), lambda i: (i, 0))


def _col_spec(height, tm):
    return pl.BlockSpec((height, tm), lambda i: (0, i))


def _dot(a, b):
    return jnp.dot(a, b, preferred_element_type=F32)


def _modulated_norm(x, gain, shift, scale):
    r = lax.rsqrt(jnp.mean(x * x, axis=-1, keepdims=True) + EPS)
    return (x * r) * (gain * (1.0 + scale)) + shift


def _head_norm(x, gain):
    r = lax.rsqrt(jnp.mean(x * x, axis=-1, keepdims=True) + EPS)
    return (x * r) * gain


def _store_vt(vt_ref, v, n_heads):
    for j in range(n_heads):
        vt_ref[j * VT_ROWS:(j + 1) * VT_ROWS, :] = v[:, j * VT_ROWS:(j + 1) * VT_ROWS].T.astype(BF16)


def _rope(a, b, cos, sin):
    return a * cos - b * sin, a * sin + b * cos


def _lane_mask(lo, hi):
    lane = lax.broadcasted_iota(jnp.int32, (1, LANES), 1)
    return (lane >= lo) & (lane < hi)


def _mod_kernel(cond_ref, w_ref, b_ref, o_ref):
    c = cond_ref[...]
    act = (c * jax.nn.sigmoid(c)).astype(BF16)
    o_ref[0] = _dot(act, w_ref[0].astype(BF16)) + b_ref[0]


def _modulation_tables(cond, ada_w, ada_b):
    tn = 1536
    return pl.pallas_call(
        _mod_kernel,
        grid=(DEPTH, N_MOD * D // tn),
        in_specs=[
            pl.BlockSpec((MOD_ROWS, D), lambda l, j: (0, 0)),
            pl.BlockSpec((1, D, tn), lambda l, j: (l, 0, j)),
            pl.BlockSpec((1, 1, tn), lambda l, j: (l, 0, j)),
        ],
        out_specs=pl.BlockSpec((1, MOD_ROWS, tn), lambda l, j: (l, 0, j)),
        out_shape=jax.ShapeDtypeStruct((DEPTH, MOD_ROWS, N_MOD * D), F32),
        compiler_params=_params(2),
        name="modulation",
    )(cond, ada_w, ada_b.reshape(DEPTH, 1, N_MOD * D))


def _conv_in_compute(x, mod_ref, g_ref, w_ref, b_ref, u_ref):
    m = mod_ref[0]
    h = _modulated_norm(x, g_ref[...], m[0:1], m[1:2]).astype(BF16)
    y = _dot(h, _w(w_ref))
    b_ref[...] = y[:, :D].astype(BF16)
    u_ref[...] = y[:, D:2 * D] * y[:, 2 * D:]


def _conv_in_kernel(x_ref, mod_ref, g_ref, w_ref, b_ref, u_ref):
    _conv_in_compute(x_ref[...], mod_ref, g_ref, w_ref, b_ref, u_ref)


def _conv_in_first_kernel(lat_ref, ctx_ref, mod_ref, g_ref, w_ref, b_ref, u_ref, xs_ref):
    x = jnp.where(pl.program_id(0) < N_LAT // TP, lat_ref[...], ctx_ref[...])
    xs_ref[...] = x
    _conv_in_compute(x, mod_ref, g_ref, w_ref, b_ref, u_ref)


def _conv_in(x, mod, gain, w_in, layer, n_rows):
    return pl.pallas_call(
        _conv_in_kernel,
        grid=(n_rows // TP,),
        in_specs=[_row_spec(TP, D), _mod_spec(TP), _resident((1, D)), _resident((D, 3 * D), layer)],
        out_specs=[_row_spec(TP, D), _row_spec(TP, D)],
        out_shape=[jax.ShapeDtypeStruct((n_rows, D), BF16), jax.ShapeDtypeStruct((n_rows, D), F32)],
        compiler_params=_params(1),
        name="conv_in",
    )(x, mod, gain, w_in)


def _conv_in_first(lat, ctx, mod, gain, w_in, layer):
    lat_tiles = N_LAT // TP
    return pl.pallas_call(
        _conv_in_first_kernel,
        grid=(N_ALL // TP,),
        in_specs=[
            pl.BlockSpec((TP, D), lambda i: (jnp.minimum(i, lat_tiles - 1), 0)),
            pl.BlockSpec((TP, D), lambda i: (jnp.maximum(i - lat_tiles, 0), 0)),
            _mod_spec(TP), _resident((1, D)), _resident((D, 3 * D), layer),
        ],
        out_specs=[_row_spec(TP, D), _row_spec(TP, D), _row_spec(TP, D)],
        out_shape=[jax.ShapeDtypeStruct((N_ALL, D), BF16), jax.ShapeDtypeStruct((N_ALL, D), F32),
                   jax.ShapeDtypeStruct((N_ALL, D), F32)],
        compiler_params=_params(1),
        name="conv_in_first",
    )(lat, ctx, mod, gain, w_in)


def _conv_gate(b_ref, u_ref, up_ref, un_ref, cw_ref):
    tm = u_ref.shape[0]
    u = u_ref[...]
    local = lax.broadcasted_iota(jnp.int32, (tm, 1), 0)
    row = local + pl.program_id(0) * tm
    in_ctx = row >= N_LAT
    first = ((row & (CTX_LEN - 1)) == 0) & (in_ctx | ((row & (SEQ - 1)) == 0))
    last = ((row & (CTX_LEN - 1)) == CTX_LEN - 1) & (in_ctx | ((row & (SEQ - 1)) == SEQ - 1))
    prev = jnp.where(local == 0, up_ref[HALO - 1:HALO, :], pltpu.roll(u, 1, axis=0))
    prev = jnp.where(first, 0.0, prev)
    nxt = jnp.where(local == tm - 1, un_ref[0:1, :], pltpu.roll(u, tm - 1, axis=0))
    nxt = jnp.where(last, 0.0, nxt)
    cw = cw_ref[...]
    z = prev * cw[0:1] + u * cw[1:2] + nxt * cw[2:3]
    return (b_ref[...].astype(F32) * z).astype(BF16)


def _gqa_proj_kernel(x_ref, mod_ref, g_ref, w_ref, qg_ref, kg_ref, ones_ref, cos_ref, sin_ref,
                     q_ref, k_ref, vt_ref):
    m = mod_ref[0]
    h = _modulated_norm(x_ref[...], g_ref[...], m[0:1], m[1:2]).astype(BF16)
    cos = cos_ref[...]
    sin = sin_ref[...]
    ones2 = ones_ref[...]

    def normed_rotated(y, gain):
        a, b = y[:, :LANES], y[:, LANES:]
        ss = a * a + b * b
        hi = ss.astype(BF16)
        lo = (ss - hi.astype(F32)).astype(BF16)
        tot = _dot(jnp.concatenate([hi, lo], axis=1), ones2)
        r = lax.rsqrt(tot * (1.0 / GQA_HEAD_DIM) + EPS)
        return _rope(a * r * gain[:, :LANES], b * r * gain[:, LANES:], cos, sin)

    wide = 2 * GQA_QK
    kv0 = _lane_mask(0, LANES // 2)
    for gg in range(GQA_GROUP // 2):
        y = _dot(h, w_ref[:, gg * wide:(gg + 1) * wide])
        for t in range(2):
            g = 2 * gg + t
            o1, o2 = normed_rotated(y[:, t * GQA_QK:(t + 1) * GQA_QK], qg_ref[...])
            for kv, keep in ((0, kv0), (1, jnp.logical_not(kv0))):
                c0 = (kv * GQA_GROUP + g) * GQA_QK
                q_ref[:, c0:c0 + LANES] = jnp.where(keep, o1, 0.0).astype(BF16)
                q_ref[:, c0 + LANES:c0 + GQA_QK] = jnp.where(keep, o2, 0.0).astype(BF16)
    y = _dot(h, w_ref[:, GQA_GROUP * GQA_QK:])
    o1, o2 = normed_rotated(y[:, :GQA_QK], kg_ref[...])
    k_ref[:, :LANES] = o1.astype(BF16)
    k_ref[:, LANES:] = o2.astype(BF16)
    _store_vt(vt_ref, y[:, GQA_QK:], GQA_KV_HEADS)


def _rope_spec(tm):
    per_seq = SEQ // tm
    return pl.BlockSpec(
        (tm, LANES), lambda i: (jnp.where(i < N_LAT // tm, i % per_seq, per_seq), 0))


def _gqa_proj(x, mod, gain, w_qkv, q_gain, k_gain, half_ones, cos, sin):
    kvw = GQA_KV_HEADS * GQA_HEAD_DIM
    qw = GQA_HEADS * GQA_QK
    return pl.pallas_call(
        _gqa_proj_kernel,
        grid=(N_ALL // TP,),
        in_specs=[
            _row_spec(TP, D), _mod_spec(TP), _resident((1, D)),
            _resident((D, D + 2 * kvw)),
            _resident((1, GQA_QK)), _resident((1, GQA_QK)), _resident((2 * LANES, LANES)),
            _rope_spec(TP), _rope_spec(TP),
        ],
        out_specs=[_row_spec(TP, qw), _row_spec(TP, GQA_QK), _col_spec(GQA_KV_HEADS * VT_ROWS, TP)],
        out_shape=[jax.ShapeDtypeStruct((N_ALL, qw), BF16),
                   jax.ShapeDtypeStruct((N_ALL, GQA_QK), BF16),
                   jax.ShapeDtypeStruct((GQA_KV_HEADS * VT_ROWS, N_ALL), BF16)],
        compiler_params=_params(1),
        name="gqa_proj",
    )(x, mod, gain, w_qkv, q_gain, k_gain, half_ones, cos, sin)


def _mla_proj_kernel(x_ref, mod_ref, g_ref, wdq_ref, qg_ref, wuq_ref, wdkv_ref, kvg_ref, wukv_ref,
                     cos_ref, sin_ref, q_ref, k_ref, vt_ref):
    m = mod_ref[0]
    h = _modulated_norm(x_ref[...], g_ref[...], m[0:1], m[1:2]).astype(BF16)
    cos = cos_ref[...]
    sin = sin_ref[...]
    half = MLA_ROPE // 2
    cq = _head_norm(_dot(h, wdq_ref[...]), qg_ref[...]).astype(BF16)
    q = _dot(cq, wuq_ref[...])
    ckv_pe = _dot(h, wdkv_ref[...])
    ckv = _head_norm(ckv_pe[:, :MLA_KV_RANK], kvg_ref[...]).astype(BF16)
    ko1, ko2 = _rope(ckv_pe[:, MLA_KV_RANK:MLA_KV_RANK + LANES], ckv_pe[:, MLA_KV_RANK + LANES:], cos, sin)
    k_pe = jnp.where(_lane_mask(0, 2 * half), ko1, ko2).astype(BF16)
    kv = _dot(ckv, wukv_ref[...])
    for pp in range(MLA_HEADS // 2):
        p0 = pp * 2 * MLA_QK
        o1, o2 = _rope(q[:, p0 + 2 * MLA_NOPE:p0 + 2 * MLA_NOPE + LANES], q[:, p0 + 2 * MLA_NOPE + LANES:p0 + 2 * MLA_QK],
                       cos, sin)
        for t in range(2):
            j = 2 * pp + t
            c0 = j * MLA_QK
            pe = jnp.where(_lane_mask(t * half, (t + 1) * half), o1,
                           jnp.where(_lane_mask(2 * half + t * half, 2 * half + (t + 1) * half), o2, 0.0))
            q_ref[:, c0:c0 + MLA_NOPE] = q[:, p0 + t * MLA_NOPE:p0 + (t + 1) * MLA_NOPE].astype(BF16)
            q_ref[:, c0 + MLA_NOPE:c0 + MLA_QK] = pe.astype(BF16)
            k_ref[:, c0:c0 + MLA_NOPE] = kv[:, j * MLA_NOPE:(j + 1) * MLA_NOPE].astype(BF16)
            k_ref[:, c0 + MLA_NOPE:c0 + MLA_QK] = k_pe
    _store_vt(vt_ref, kv[:, MLA_HEADS * MLA_NOPE:], MLA_HEADS)


def _mla_proj(x, mod, gain, w_dq, q_gain, w_uq, w_dkv, kv_gain, w_ukv, cos, sin):
    qkw = MLA_HEADS * MLA_QK
    return pl.pallas_call(
        _mla_proj_kernel,
        grid=(N_ALL // TP,),
        in_specs=[
            _row_spec(TP, D), _mod_spec(TP), _resident((1, D)),
            _resident((D, MLA_Q_RANK)), _resident((1, MLA_Q_RANK)), _resident((MLA_Q_RANK, qkw)),
            _resident((D, MLA_KV_RANK + 2 * LANES)), _resident((1, MLA_KV_RANK)),
            _resident((MLA_KV_RANK, MLA_HEADS * (MLA_NOPE + MLA_V))),
            _rope_spec(TP), _rope_spec(TP),
        ],
        out_specs=[_row_spec(TP, qkw), _row_spec(TP, qkw), _col_spec(MLA_HEADS * VT_ROWS, TP)],
        out_shape=[jax.ShapeDtypeStruct((N_ALL, qkw), BF16),
                   jax.ShapeDtypeStruct((N_ALL, qkw), BF16),
                   jax.ShapeDtypeStruct((MLA_HEADS * VT_ROWS, N_ALL), BF16)],
        compiler_params=_params(1),
        name="mla_proj",
    )(x, mod, gain, w_dq, q_gain, w_uq, w_dkv, kv_gain, w_ukv, cos, sin)


def _nt_dot(a, b):
    return lax.dot_general(a, b, (((1,), (1,)), ((), ())), preferred_element_type=F32)


def _attn_scores(q, k_parts):
    return [_nt_dot(k, q) for k in k_parts]


def _attn_kernel(q_ref, qc_ref, kl_ref, kc_ref, vtl_ref, vtc_ref, o_ref, oc_ref, *, heads, group, dq, dv, shared_k):
    n = heads * group

    def k_cols(qh):
        kv = 0 if shared_k else qh // group
        return slice(kv * dq, (kv + 1) * dq)

    def v_rows(qh):
        return slice(qh // group * VT_ROWS, (qh // group + 1) * VT_ROWS)

    def run(queries, out_ref, k_refs, vt_refs):
        def scores(qh):
            return _attn_scores(queries[:, qh * dq:(qh + 1) * dq], [k[:, k_cols(qh)] for k in k_refs])

        pending = [scores(qh) for qh in range(min(2, n))]
        for qh in range(n):
            cur = pending.pop(0)
            mx = functools.reduce(jnp.maximum, [jnp.max(s, axis=0, keepdims=True) for s in cur])
            probs = [jnp.exp2(s - mx) for s in cur]
            den = sum(jnp.sum(p, axis=0, keepdims=True) for p in probs)
            probs = [p.astype(BF16) for p in probs]
            if qh + 2 < n:
                pending.append(scores(qh + 2))
            acc = sum(_dot(vt[v_rows(qh), :], p) for vt, p in zip(vt_refs, probs))
            out_ref[:, qh * dv:(qh + 1) * dv] = (acc / den).T.astype(BF16)

    run(q_ref, o_ref, [kl_ref, kc_ref], [vtl_ref, vtc_ref])

    @pl.when(pl.program_id(2) == 0)
    def _():
        run(qc_ref, oc_ref, [kc_ref], [vtc_ref])


def _attention(q, k, vt, *, kv_heads, group, dq, dv, heads_per_step, shared_k):
    hp = heads_per_step
    lat_tiles = SEQ // TQ
    ctx_block0 = N_LAT // CTX_LEN
    kw = dq if shared_k else hp * dq
    k_col = (lambda h: 0) if shared_k else (lambda h: h)
    width = kv_heads * group * dv
    return pl.pallas_call(
        functools.partial(_attn_kernel, heads=hp, group=group, dq=dq, dv=dv, shared_k=shared_k),
        grid=(BATCH, kv_heads // hp, lat_tiles),
        in_specs=[
            pl.BlockSpec((TQ, hp * group * dq), lambda b, h, t: (b * lat_tiles + t, h)),
            pl.BlockSpec((CTX_LEN, hp * group * dq), lambda b, h, t: (ctx_block0 + b, h)),
            pl.BlockSpec((SEQ, kw), lambda b, h, t: (b, k_col(h))),
            pl.BlockSpec((CTX_LEN, kw), lambda b, h, t: (ctx_block0 + b, k_col(h))),
            pl.BlockSpec((hp * VT_ROWS, SEQ), lambda b, h, t: (h, b)),
            pl.BlockSpec((hp * VT_ROWS, CTX_LEN), lambda b, h, t: (h, ctx_block0 + b)),
        ],
        out_specs=[pl.BlockSpec((TQ, hp * group * dv), lambda b, h, t: (b * lat_tiles + t, h)),
                   pl.BlockSpec((CTX_LEN, hp * group * dv), lambda b, h, t: (b, h))],
        out_shape=[jax.ShapeDtypeStruct((N_LAT, width), BF16), jax.ShapeDtypeStruct((N_CTX, width), BF16)],
        compiler_params=_params(3),
        name="attention",
    )(q, q, k, k, vt, vt)


def _ffn_tail(x1, m, g2_ref, w1_ref, w3_ref, w2_ref):
    h = _modulated_norm(x1, g2_ref[...], m[3:4], m[4:5]).astype(BF16)
    a = _dot(h, _w(w1_ref))
    b = _dot(h, _w(w3_ref))
    gated = (a * jax.nn.sigmoid(a) * b).astype(BF16)
    return x1 + m[5:6] * _dot(gated, _w(w2_ref))


def _attn_tail_kernel(al_ref, ac_ref, wo_ref, x_ref, mod_ref, g2_ref, w1_ref, w3_ref, w2_ref, o_ref):
    m = mod_ref[0]
    a = jnp.where(pl.program_id(0) < N_LAT // TM, al_ref[...], ac_ref[...])
    x1 = x_ref[...] + m[2:3] * _dot(a, _w(wo_ref))
    o_ref[...] = _ffn_tail(x1, m, g2_ref, w1_ref, w3_ref, w2_ref)


def _conv_tail_kernel(b_ref, u_ref, up_ref, un_ref, cw_ref, wo_ref, x_ref, mod_ref,
                      g2_ref, w1_ref, w3_ref, w2_ref, fg_ref, o_ref, *, final):
    m = mod_ref[0]
    x1 = x_ref[...] + m[2:3] * _dot(_conv_gate(b_ref, u_ref, up_ref, un_ref, cw_ref), _w(wo_ref))
    x2 = _ffn_tail(x1, m, g2_ref, w1_ref, w3_ref, w2_ref)
    o_ref[...] = _head_norm(x2, fg_ref[...]) if final else x2


def _ffn_specs(layer):
    return [_mod_spec(TM), _resident((1, D)), _resident((D, FFN_HIDDEN), layer),
            _resident((D, FFN_HIDDEN), layer), _resident((FFN_HIDDEN, D), layer)]


def _attn_tail(a_lat, a_ctx, w_o, mixer_layer, x, mod, g2, w1, w3, w2, layer):
    lat_tiles = N_LAT // TM
    return pl.pallas_call(
        _attn_tail_kernel,
        grid=(N_ALL // TM,),
        in_specs=[
            pl.BlockSpec((TM, D), lambda i: (jnp.minimum(i, lat_tiles - 1), 0)),
            pl.BlockSpec((TM, D), lambda i: (jnp.maximum(i - lat_tiles, 0), 0)),
            _resident((D, D), mixer_layer), _row_spec(TM, D),
        ] + _ffn_specs(layer),
        out_specs=_row_spec(TM, D),
        out_shape=jax.ShapeDtypeStruct((N_ALL, D), F32),
        compiler_params=_params(1),
        name="attn_tail",
    )(a_lat, a_ctx, w_o, x, mod, g2, w1, w3, w2)


def _conv_tail(b, u, conv_w, w_out, mixer_layer, x, mod, g2, w1, w3, w2, layer, final_gain, n_rows, final):
    per = TM // HALO
    n_halo = n_rows // HALO
    return pl.pallas_call(
        functools.partial(_conv_tail_kernel, final=final),
        grid=(n_rows // TM,),
        in_specs=[
            _row_spec(TM, D), _row_spec(TM, D),
            pl.BlockSpec((HALO, D), lambda i: (jnp.maximum(i * per - 1, 0), 0)),
            pl.BlockSpec((HALO, D), lambda i: (jnp.minimum((i + 1) * per, n_halo - 1), 0)),
            _resident((3, D)), _resident((D, D), mixer_layer), _row_spec(TM, D),
        ] + _ffn_specs(layer) + [_resident((1, D))],
        out_specs=_row_spec(TM, D),
        out_shape=jax.ShapeDtypeStruct((n_rows, D), F32),
        compiler_params=_params(1),
        name="conv_tail",
    )(b, u, u, u, conv_w, w_out, x, mod, g2, w1, w3, w2, final_gain)


def _axial_angles(rot_dim):
    n = rot_dim // 4
    rows = jnp.repeat(jnp.arange(SEQ // GRID_W, dtype=F32), GRID_W)
    cols = jnp.tile(jnp.arange(GRID_W, dtype=F32), SEQ // GRID_W)
    freqs = ROPE_THETA ** (-jnp.arange(n, dtype=F32) / n)
    return jnp.concatenate([rows[:, None] * freqs, cols[:, None] * freqs], axis=-1)


def _rope_tables(rot_dim):
    ang = _axial_angles(rot_dim)
    copies = LANES // (rot_dim // 2)
    cos_t = jnp.concatenate([jnp.tile(jnp.cos(ang), (1, copies)), jnp.ones((TP, LANES), F32)], axis=0)
    sin_t = jnp.concatenate([jnp.tile(jnp.sin(ang), (1, copies)), jnp.zeros((TP, LANES), F32)], axis=0)
    return cos_t, sin_t


def _gqa_slab_pairs(w, n_kv, n_group):
    half = GQA_HEAD_DIM // 2
    w = w.reshape(w.shape[0], n_kv, n_group, 2, half)
    return jnp.transpose(w, (0, 2, 3, 1, 4)).reshape(w.shape[0], n_group * GQA_QK)


def _mla_weights(w_dq, w_uq, w_dkv, w_ukv):
    half = MLA_ROPE // 2
    qk = MLA_NOPE + MLA_ROPE
    uq = w_uq.reshape(MLA_Q_RANK, MLA_HEADS // 2, 2, qk)
    nope = uq[..., :MLA_NOPE].reshape(MLA_Q_RANK, MLA_HEADS // 2, 2 * MLA_NOPE)
    x1 = uq[..., MLA_NOPE:MLA_NOPE + half].reshape(MLA_Q_RANK, MLA_HEADS // 2, 2 * half)
    x2 = uq[..., MLA_NOPE + half:].reshape(MLA_Q_RANK, MLA_HEADS // 2, 2 * half)
    uq = jnp.concatenate([nope, x1, x1, x2, x2], axis=-1).reshape(MLA_Q_RANK, MLA_HEADS * MLA_QK)
    k1 = w_dkv[:, MLA_KV_RANK:MLA_KV_RANK + half]
    k2 = w_dkv[:, MLA_KV_RANK + half:]
    dkv = jnp.concatenate([w_dkv[:, :MLA_KV_RANK]] + [k1] * 4 + [k2] * 4, axis=-1)
    ukv = w_ukv.reshape(MLA_KV_RANK, MLA_HEADS, MLA_NOPE + MLA_V)
    ukv = jnp.concatenate([ukv[:, :, :MLA_NOPE].reshape(MLA_KV_RANK, -1),
                           ukv[:, :, MLA_NOPE:].reshape(MLA_KV_RANK, -1)], axis=-1)
    return w_dq.astype(BF16), uq.astype(BF16), dkv.astype(BF16), ukv.astype(BF16)


def kernel(x, c, ctx, c_ctx, ada_w, ada_b, norm1_g, norm2_g, ffn_w1, ffn_w3, ffn_w2, conv_w_in, conv_w, conv_w_out, gqa_wq, gqa_wk, gqa_wv, gqa_q_norm, gqa_k_norm, gqa_wo, mla_w_dq, mla_q_norm, mla_w_uq, mla_w_dkv, mla_kv_norm, mla_w_ukv, mla_wo, final_g):
    assert x.shape == (BATCH, SEQ, D) and ctx.shape == (BATCH, CTX_LEN, D)
    cond = jnp.concatenate(
        [c, c_ctx[None], jnp.zeros((MOD_ROWS - BATCH - 1, D), F32)], axis=0)
    mods = _modulation_tables(cond, ada_w, ada_b).reshape(DEPTH, MOD_ROWS, N_MOD, D)

    gqa_cos, gqa_sin = _rope_tables(GQA_HEAD_DIM)
    lane_half = jnp.arange(LANES) // (LANES // 2)
    half_ones = jnp.tile((lane_half[:, None] == lane_half[None, :]).astype(BF16), (2, 1))
    mla_cos, mla_sin = _rope_tables(MLA_ROPE)

    ffn_w1, ffn_w3, ffn_w2 = ffn_w1.astype(BF16), ffn_w3.astype(BF16), ffn_w2.astype(BF16)
    conv_w_in, conv_w_out = conv_w_in.astype(BF16), conv_w_out.astype(BF16)
    gqa_wo = gqa_wo.astype(BF16)
    mla_wo = mla_wo.astype(BF16)

    xs = None
    for i in range(DEPTH):
        kind, j = i % 3, i // 3
        mod = mods[i]
        last = i == DEPTH - 1
        n_rows = N_LAT if last else N_ALL
        g1 = norm1_g[i][None]
        ffn = (norm2_g[i][None], ffn_w1, ffn_w3, ffn_w2, i)
        if kind == 0:
            if i == 0:
                b, u, xs = _conv_in_first(x.reshape(N_LAT, D), ctx.reshape(N_CTX, D), mod, g1, conv_w_in, j)
            else:
                b, u = _conv_in(xs, mod, g1, conv_w_in, j, n_rows)
            xs = _conv_tail(b, u, conv_w[j], conv_w_out, j, xs, mod, *ffn, final_g[None], n_rows, last)
        elif kind == 1:
            w_qkv = jnp.concatenate([_gqa_slab_pairs(gqa_wq[j].astype(BF16), GQA_KV_HEADS, GQA_GROUP),
                                     _gqa_slab_pairs(gqa_wk[j].astype(BF16), GQA_KV_HEADS, 1),
                                     gqa_wv[j].astype(BF16)], axis=-1)
            q_gain = _gqa_slab_pairs(jnp.tile(gqa_q_norm[j], GQA_KV_HEADS)[None], GQA_KV_HEADS, 1)
            k_gain = _gqa_slab_pairs(jnp.tile(gqa_k_norm[j], GQA_KV_HEADS)[None], GQA_KV_HEADS, 1)
            q_gain = q_gain * (GQA_HEAD_DIM ** -0.5 * LOG2E)
            q, k, vt = _gqa_proj(xs, mod, g1, w_qkv, q_gain, k_gain, half_ones, gqa_cos, gqa_sin)
            a = _attention(q, k, vt, kv_heads=GQA_KV_HEADS, group=GQA_GROUP,
                           dq=GQA_QK, dv=GQA_HEAD_DIM, heads_per_step=1, shared_k=True)
            xs = _attn_tail(*a, gqa_wo, j, xs, mod, *ffn)
        else:
            w_dq, w_uq, w_dkv, w_ukv = _mla_weights(mla_w_dq[j], mla_w_uq[j], mla_w_dkv[j], mla_w_ukv[j])
            q_gain = mla_q_norm[j][None] * ((MLA_NOPE + MLA_ROPE) ** -0.5 * LOG2E)
            q, k, vt = _mla_proj(xs, mod, g1, w_dq, q_gain, w_uq, w_dkv,
                                 mla_kv_norm[j][None], w_ukv, mla_cos, mla_sin)
            a = _attention(q, k, vt, kv_heads=MLA_HEADS, group=1, dq=MLA_QK, dv=MLA_V, heads_per_step=4,
                           shared_k=False)
            xs = _attn_tail(*a, mla_wo, j, xs, mod, *ffn)

    assert (DEPTH - 1) % 3 == 0
    return xs.reshape(BATCH, SEQ, D)
```

```python
import functools

import jax
import jax.numpy as jnp
import numpy as np
from jax import lax
from jax.experimental import pallas as pl
from jax.experimental.pallas import tpu as pltpu

D = 1024
BATCH = 8
SEQ = 2048
DEPTH = 4
GRID_W = 64
CTX_LEN = 256
N_MOD = 6
FFN_HIDDEN = 2816
ROPE_THETA = 10000.0
EPS = 1e-6
GQA_HEAD_DIM = 128
GQA_HEADS = 8
GQA_KV_HEADS = 2
GQA_GROUP = 4
MLA_HEADS = 8
MLA_NOPE = 128
MLA_ROPE = 64
MLA_V = 128
MLA_KV_RANK = 256
MLA_Q_RANK = 768
MLA_QK = 256
GQA_QK = 256
LANES = 128
VT_ROWS = 128

N_LAT = BATCH * SEQ
N_CTX = BATCH * CTX_LEN
N_ALL = N_LAT + N_CTX
MOD_ROWS = 16

TM = 512
TP = 1024
TQ = 1024
HALO = 8
VMEM_LIMIT = 56 * 1024 * 1024
LOG2E = 1.4426950408889634
MAX_BOUND_SHIFT = 50.0

F32 = jnp.float32
BF16 = jnp.bfloat16


def _params(n_axes):
    return pltpu.CompilerParams(
        dimension_semantics=("arbitrary",) * n_axes, vmem_limit_bytes=VMEM_LIMIT)


def _resident(shape, layer=None):
    nd = len(shape)
    if layer is None:
        return pl.BlockSpec(shape, lambda *_: (0,) * nd, pipeline_mode=pl.Buffered(1))
    return pl.BlockSpec((1,) + tuple(shape), lambda *_: (layer,) + (0,) * nd, pipeline_mode=pl.Buffered(1))


def _w(ref):
    return ref[0] if len(ref.shape) == 3 else ref[...]


def _mod_spec(tm):
    return pl.BlockSpec((1, N_MOD, D), lambda i: (jnp.minimum(i * tm // SEQ, BATCH), 0, 0))


def _row_spec(tm, width):
    return pl.BlockSpec((tm, width), lambda i: (i, 0))


def _col_spec(height, tm):
    return pl.BlockSpec((height, tm), lambda i: (0, i))


def _dot(a, b):
    return jnp.dot(a, b, preferred_element_type=F32)


def _modulated_norm(x, gain, shift, scale):
    r = lax.rsqrt(jnp.mean(x * x, axis=-1, keepdims=True) + EPS)
    return (x * r) * (gain * (1.0 + scale)) + shift


def _head_norm(x, gain):
    r = lax.rsqrt(jnp.mean(x * x, axis=-1, keepdims=True) + EPS)
    return (x * r) * gain


def _store_vt(vt_ref, v, n_heads):
    for j in range(n_heads):
        vt_ref[j * VT_ROWS:(j + 1) * VT_ROWS, :] = v[:, j * VT_ROWS:(j + 1) * VT_ROWS].T.astype(BF16)


def _rope(a, b, cos, sin):
    return a * cos - b * sin, a * sin + b * cos


def _lane_mask(lo, hi):
    lane = lax.broadcasted_iota(jnp.int32, (1, LANES), 1)
    return (lane >= lo) & (lane < hi)


def _mod_kernel(cond_ref, w_ref, b_ref, o_ref):
    c = cond_ref[...]
    act = (c * jax.nn.sigmoid(c)).astype(BF16)
    o_ref[0] = _dot(act, w_ref[0].astype(BF16)) + b_ref[0]


def _modulation_tables(cond, ada_w, ada_b):
    tn = 1536
    return pl.pallas_call(
        _mod_kernel,
        grid=(DEPTH, N_MOD * D // tn),
        in_specs=[
            pl.BlockSpec((MOD_ROWS, D), lambda l, j: (0, 0)),
            pl.BlockSpec((1, D, tn), lambda l, j: (l, 0, j)),
            pl.BlockSpec((1, 1, tn), lambda l, j: (l, 0, j)),
        ],
        out_specs=pl.BlockSpec((1, MOD_ROWS, tn), lambda l, j: (l, 0, j)),
        out_shape=jax.ShapeDtypeStruct((DEPTH, MOD_ROWS, N_MOD * D), F32),
        compiler_params=_params(2),
        name="modulation",
    )(cond, ada_w, ada_b.reshape(DEPTH, 1, N_MOD * D))


def _conv_in_compute(x, mod_ref, g_ref, w_ref, b_ref, u_ref):
    m = mod_ref[0]
    h = _modulated_norm(x, g_ref[...], m[0:1], m[1:2]).astype(BF16)
    y = _dot(h, _w(w_ref))
    b_ref[...] = y[:, :D].astype(BF16)
    u_ref[...] = y[:, D:2 * D] * y[:, 2 * D:]


def _conv_in_kernel(x_ref, mod_ref, g_ref, w_ref, b_ref, u_ref):
    _conv_in_compute(x_ref[...], mod_ref, g_ref, w_ref, b_ref, u_ref)


def _conv_in_first_kernel(lat_ref, ctx_ref, mod_ref, g_ref, w_ref, b_ref, u_ref, xs_ref):
    x = jnp.where(pl.program_id(0) < N_LAT // TP, lat_ref[...], ctx_ref[...])
    xs_ref[...] = x
    _conv_in_compute(x, mod_ref, g_ref, w_ref, b_ref, u_ref)


def _conv_in(x, mod, gain, w_in, layer, n_rows):
    return pl.pallas_call(
        _conv_in_kernel,
        grid=(n_rows // TP,),
        in_specs=[_row_spec(TP, D), _mod_spec(TP), _resident((1, D)), _resident((D, 3 * D), layer)],
        out_specs=[_row_spec(TP, D), _row_spec(TP, D)],
        out_shape=[jax.ShapeDtypeStruct((n_rows, D), BF16), jax.ShapeDtypeStruct((n_rows, D), F32)],
        compiler_params=_params(1),
        name="conv_in",
    )(x, mod, gain, w_in)


def _conv_in_first(lat, ctx, mod, gain, w_in, layer):
    lat_tiles = N_LAT // TP
    return pl.pallas_call(
        _conv_in_first_kernel,
        grid=(N_ALL // TP,),
        in_specs=[
            pl.BlockSpec((TP, D), lambda i: (jnp.minimum(i, lat_tiles - 1), 0)),
            pl.BlockSpec((TP, D), lambda i: (jnp.maximum(i - lat_tiles, 0), 0)),
            _mod_spec(TP), _resident((1, D)), _resident((D, 3 * D), layer),
        ],
        out_specs=[_row_spec(TP, D), _row_spec(TP, D), _row_spec(TP, D)],
        out_shape=[jax.ShapeDtypeStruct((N_ALL, D), BF16), jax.ShapeDtypeStruct((N_ALL, D), F32),
                   jax.ShapeDtypeStruct((N_ALL, D), F32)],
        compiler_params=_params(1),
        name="conv_in_first",
    )(lat, ctx, mod, gain, w_in)


def _conv_gate(b_ref, u_ref, up_ref, un_ref, cw_ref):
    tm = u_ref.shape[0]
    u = u_ref[...]
    local = lax.broadcasted_iota(jnp.int32, (tm, 1), 0)
    row = local + pl.program_id(0) * tm
    in_ctx = row >= N_LAT
    first = ((row & (CTX_LEN - 1)) == 0) & (in_ctx | ((row & (SEQ - 1)) == 0))
    last = ((row & (CTX_LEN - 1)) == CTX_LEN - 1) & (in_ctx | ((row & (SEQ - 1)) == SEQ - 1))
    prev = jnp.where(local == 0, up_ref[HALO - 1:HALO, :], pltpu.roll(u, 1, axis=0))
    prev = jnp.where(first, 0.0, prev)
    nxt = jnp.where(local == tm - 1, un_ref[0:1, :], pltpu.roll(u, tm - 1, axis=0))
    nxt = jnp.where(last, 0.0, nxt)
    cw = cw_ref[...]
    z = prev * cw[0:1] + u * cw[1:2] + nxt * cw[2:3]
    return (b_ref[...].astype(F32) * z).astype(BF16)


def _gqa_proj_kernel(x_ref, mod_ref, g_ref, w_ref, qg_ref, kg_ref, ones_ref, cos_ref, sin_ref,
                     q_ref, k_ref, vt_ref):
    m = mod_ref[0]
    h = _modulated_norm(x_ref[...], g_ref[...], m[0:1], m[1:2]).astype(BF16)
    cos = cos_ref[...]
    sin = sin_ref[...]
    ones2 = ones_ref[...]

    def normed_rotated(y, gain):
        a, b = y[:, :LANES], y[:, LANES:]
        ss = a * a + b * b
        hi = ss.astype(BF16)
        lo = (ss - hi.astype(F32)).astype(BF16)
        tot = _dot(jnp.concatenate([hi, lo], axis=1), ones2)
        r = lax.rsqrt(tot * (1.0 / GQA_HEAD_DIM) + EPS)
        return _rope(a * r * gain[:, :LANES], b * r * gain[:, LANES:], cos, sin)

    wide = 2 * GQA_QK
    kv0 = _lane_mask(0, LANES // 2)
    for gg in range(GQA_GROUP // 2):
        y = _dot(h, w_ref[:, gg * wide:(gg + 1) * wide])
        for t in range(2):
            g = 2 * gg + t
            o1, o2 = normed_rotated(y[:, t * GQA_QK:(t + 1) * GQA_QK], qg_ref[...])
            for kv, keep in ((0, kv0), (1, jnp.logical_not(kv0))):
                c0 = (kv * GQA_GROUP + g) * GQA_QK
                q_ref[:, c0:c0 + LANES] = jnp.where(keep, o1, 0.0).astype(BF16)
                q_ref[:, c0 + LANES:c0 + GQA_QK] = jnp.where(keep, o2, 0.0).astype(BF16)
    y = _dot(h, w_ref[:, GQA_GROUP * GQA_QK:])
    o1, o2 = normed_rotated(y[:, :GQA_QK], kg_ref[...])
    k_ref[:, :LANES] = o1.astype(BF16)
    k_ref[:, LANES:] = o2.astype(BF16)
    _store_vt(vt_ref, y[:, GQA_QK:], GQA_KV_HEADS)


def _rope_spec(tm):
    per_seq = SEQ // tm
    return pl.BlockSpec(
        (tm, LANES), lambda i: (jnp.where(i < N_LAT // tm, i % per_seq, per_seq), 0))


def _gqa_proj(x, mod, gain, w_qkv, q_gain, k_gain, half_ones, cos, sin):
    kvw = GQA_KV_HEADS * GQA_HEAD_DIM
    qw = GQA_HEADS * GQA_QK
    return pl.pallas_call(
        _gqa_proj_kernel,
        grid=(N_ALL // TP,),
        in_specs=[
            _row_spec(TP, D), _mod_spec(TP), _resident((1, D)),
            _resident((D, D + 2 * kvw)),
            _resident((1, GQA_QK)), _resident((1, GQA_QK)), _resident((2 * LANES, LANES)),
            _rope_spec(TP), _rope_spec(TP),
        ],
        out_specs=[_row_spec(TP, qw), _row_spec(TP, GQA_QK), _col_spec(GQA_KV_HEADS * VT_ROWS, TP)],
        out_shape=[jax.ShapeDtypeStruct((N_ALL, qw), BF16),
                   jax.ShapeDtypeStruct((N_ALL, GQA_QK), BF16),
                   jax.ShapeDtypeStruct((GQA_KV_HEADS * VT_ROWS, N_ALL), BF16)],
        compiler_params=_params(1),
        name="gqa_proj",
    )(x, mod, gain, w_qkv, q_gain, k_gain, half_ones, cos, sin)


def _mla_proj_kernel(x_ref, mod_ref, g_ref, wdq_ref, qg_ref, wuq_ref, wdkv_ref, kvg_ref, wukv_ref,
                     cos_ref, sin_ref, q_ref, k_ref, vt_ref):
    m = mod_ref[0]
    h = _modulated_norm(x_ref[...], g_ref[...], m[0:1], m[1:2]).astype(BF16)
    cos = cos_ref[...]
    sin = sin_ref[...]
    half = MLA_ROPE // 2
    cq = _head_norm(_dot(h, wdq_ref[...]), qg_ref[...]).astype(BF16)
    q = _dot(cq, wuq_ref[...])
    ckv_pe = _dot(h, wdkv_ref[...])
    ckv = _head_norm(ckv_pe[:, :MLA_KV_RANK], kvg_ref[...]).astype(BF16)
    ko1, ko2 = _rope(ckv_pe[:, MLA_KV_RANK:MLA_KV_RANK + LANES], ckv_pe[:, MLA_KV_RANK + LANES:], cos, sin)
    k_pe = jnp.where(_lane_mask(0, 2 * half), ko1, ko2).astype(BF16)
    kv = _dot(ckv, wukv_ref[...])
    for pp in range(MLA_HEADS // 2):
        p0 = pp * 2 * MLA_QK
        o1, o2 = _rope(q[:, p0 + 2 * MLA_NOPE:p0 + 2 * MLA_NOPE + LANES], q[:, p0 + 2 * MLA_NOPE + LANES:p0 + 2 * MLA_QK],
                       cos, sin)
        for t in range(2):
            j = 2 * pp + t
            c0 = j * MLA_QK
            pe = jnp.where(_lane_mask(t * half, (t + 1) * half), o1,
                           jnp.where(_lane_mask(2 * half + t * half, 2 * half + (t + 1) * half), o2, 0.0))
            q_ref[:, c0:c0 + MLA_NOPE] = q[:, p0 + t * MLA_NOPE:p0 + (t + 1) * MLA_NOPE].astype(BF16)
            q_ref[:, c0 + MLA_NOPE:c0 + MLA_QK] = pe.astype(BF16)
            k_ref[:, c0:c0 + MLA_NOPE] = kv[:, j * MLA_NOPE:(j + 1) * MLA_NOPE].astype(BF16)
            k_ref[:, c0 + MLA_NOPE:c0 + MLA_QK] = k_pe
    _store_vt(vt_ref, kv[:, MLA_HEADS * MLA_NOPE:], MLA_HEADS)


def _mla_proj(x, mod, gain, w_dq, q_gain, w_uq, w_dkv, kv_gain, w_ukv, cos, sin):
    qkw = MLA_HEADS * MLA_QK
    return pl.pallas_call(
        _mla_proj_kernel,
        grid=(N_ALL // TP,),
        in_specs=[
            _row_spec(TP, D), _mod_spec(TP), _resident((1, D)),
            _resident((D, MLA_Q_RANK)), _resident((1, MLA_Q_RANK)), _resident((MLA_Q_RANK, qkw)),
            _resident((D, MLA_KV_RANK + 2 * LANES)), _resident((1, MLA_KV_RANK)),
            _resident((MLA_KV_RANK, MLA_HEADS * (MLA_NOPE + MLA_V))),
            _rope_spec(TP), _rope_spec(TP),
        ],
        out_specs=[_row_spec(TP, qkw), _row_spec(TP, qkw), _col_spec(MLA_HEADS * VT_ROWS, TP)],
        out_shape=[jax.ShapeDtypeStruct((N_ALL, qkw), BF16),
                   jax.ShapeDtypeStruct((N_ALL, qkw), BF16),
                   jax.ShapeDtypeStruct((MLA_HEADS * VT_ROWS, N_ALL), BF16)],
        compiler_params=_params(1),
        name="mla_proj",
    )(x, mod, gain, w_dq, q_gain, w_uq, w_dkv, kv_gain, w_ukv, cos, sin)


def _nt_dot(a, b):
    return lax.dot_general(a, b, (((1,), (1,)), ((), ())), preferred_element_type=F32)


def _attn_scores(q, k_parts):
    return [_nt_dot(k, q) for k in k_parts]


def _attn_kernel(bound_ref, q_ref, qc_ref, kl_ref, kc_ref, vtl_ref, vtc_ref, o_ref, oc_ref,
                 *, heads, group, dq, dv, shared_k, bounded):
    n = heads * group

    def k_cols(qh):
        kv = 0 if shared_k else qh // group
        return slice(kv * dq, (kv + 1) * dq)

    def v_rows(qh):
        return slice(qh // group * VT_ROWS, (qh // group + 1) * VT_ROWS)

    def run(queries, out_ref, k_refs, vt_refs, shift=None):
        def scores(qh):
            return _attn_scores(queries[:, qh * dq:(qh + 1) * dq], [k[:, k_cols(qh)] for k in k_refs])

        ahead = 2 if shift is None else 1
        pending = [scores(qh) for qh in range(min(ahead, n))]
        for qh in range(n):
            cur = pending.pop(0)
            if shift is None:
                mx = functools.reduce(jnp.maximum, [jnp.max(s, axis=0, keepdims=True) for s in cur])
            else:
                mx = shift
            probs = [jnp.exp2(s - mx) for s in cur]
            den = sum(jnp.sum(p, axis=0, keepdims=True) for p in probs)
            probs = [p.astype(BF16) for p in probs]
            if qh + ahead < n:
                pending.append(scores(qh + ahead))
            acc = sum(_dot(vt[v_rows(qh), :], p) for vt, p in zip(vt_refs, probs))
            out_ref[:, qh * dv:(qh + 1) * dv] = (acc / den).T.astype(BF16)

    lat = (q_ref, o_ref, [kl_ref, kc_ref], [vtl_ref, vtc_ref])
    if bounded:
        bound = bound_ref[0, 0]

        @pl.when(bound <= MAX_BOUND_SHIFT)
        def _():
            run(*lat, shift=bound)

        @pl.when(bound > MAX_BOUND_SHIFT)
        def _():
            run(*lat)
    else:
        run(*lat)

    @pl.when(pl.program_id(2) == 0)
    def _():
        run(qc_ref, oc_ref, [kc_ref], [vtc_ref])


def _attention(q, k, vt, score_bound, *, kv_heads, group, dq, dv, heads_per_step, shared_k):
    hp = heads_per_step
    lat_tiles = SEQ // TQ
    ctx_block0 = N_LAT // CTX_LEN
    kw = dq if shared_k else hp * dq
    k_col = (lambda h: 0) if shared_k else (lambda h: h)
    width = kv_heads * group * dv
    return pl.pallas_call(
        functools.partial(_attn_kernel, heads=hp, group=group, dq=dq, dv=dv, shared_k=shared_k,
                          bounded=score_bound is not None),
        grid=(BATCH, kv_heads // hp, lat_tiles),
        in_specs=[
            pl.BlockSpec(memory_space=pltpu.SMEM),
            pl.BlockSpec((TQ, hp * group * dq), lambda b, h, t: (b * lat_tiles + t, h)),
            pl.BlockSpec((CTX_LEN, hp * group * dq), lambda b, h, t: (ctx_block0 + b, h)),
            pl.BlockSpec((SEQ, kw), lambda b, h, t: (b, k_col(h))),
            pl.BlockSpec((CTX_LEN, kw), lambda b, h, t: (ctx_block0 + b, k_col(h))),
            pl.BlockSpec((hp * VT_ROWS, SEQ), lambda b, h, t: (h, b)),
            pl.BlockSpec((hp * VT_ROWS, CTX_LEN), lambda b, h, t: (h, ctx_block0 + b)),
        ],
        out_specs=[pl.BlockSpec((TQ, hp * group * dv), lambda b, h, t: (b * lat_tiles + t, h)),
                   pl.BlockSpec((CTX_LEN, hp * group * dv), lambda b, h, t: (b, h))],
        out_shape=[jax.ShapeDtypeStruct((N_LAT, width), BF16), jax.ShapeDtypeStruct((N_CTX, width), BF16)],
        compiler_params=_params(3),
        name="attention",
    )(jnp.full((1, 1), jnp.inf if score_bound is None else score_bound, F32), q, q, k, k, vt, vt)


def _ffn_tail(x1, m, g2_ref, w1_ref, w3_ref, w2_ref):
    h = _modulated_norm(x1, g2_ref[...], m[3:4], m[4:5]).astype(BF16)
    a = _dot(h, _w(w1_ref))
    b = _dot(h, _w(w3_ref))
    gated = (a * jax.nn.sigmoid(a) * b).astype(BF16)
    return x1 + m[5:6] * _dot(gated, _w(w2_ref))


def _attn_tail_kernel(al_ref, ac_ref, wo_ref, x_ref, mod_ref, g2_ref, w1_ref, w3_ref, w2_ref, o_ref):
    m = mod_ref[0]
    a = jnp.where(pl.program_id(0) < N_LAT // TM, al_ref[...], ac_ref[...])
    x1 = x_ref[...] + m[2:3] * _dot(a, _w(wo_ref))
    o_ref[...] = _ffn_tail(x1, m, g2_ref, w1_ref, w3_ref, w2_ref)


def _conv_tail_kernel(b_ref, u_ref, up_ref, un_ref, cw_ref, wo_ref, x_ref, mod_ref,
                      g2_ref, w1_ref, w3_ref, w2_ref, fg_ref, o_ref, *, final):
    m = mod_ref[0]
    x1 = x_ref[...] + m[2:3] * _dot(_conv_gate(b_ref, u_ref, up_ref, un_ref, cw_ref), _w(wo_ref))
    x2 = _ffn_tail(x1, m, g2_ref, w1_ref, w3_ref, w2_ref)
    o_ref[...] = _head_norm(x2, fg_ref[...]) if final else x2


def _ffn_specs(layer):
    return [_mod_spec(TM), _resident((1, D)), _resident((D, FFN_HIDDEN), layer),
            _resident((D, FFN_HIDDEN), layer), _resident((FFN_HIDDEN, D), layer)]


def _attn_tail(a_lat, a_ctx, w_o, mixer_layer, x, mod, g2, w1, w3, w2, layer):
    lat_tiles = N_LAT // TM
    return pl.pallas_call(
        _attn_tail_kernel,
        grid=(N_ALL // TM,),
        in_specs=[
            pl.BlockSpec((TM, D), lambda i: (jnp.minimum(i, lat_tiles - 1), 0)),
            pl.BlockSpec((TM, D), lambda i: (jnp.maximum(i - lat_tiles, 0), 0)),
            _resident((D, D), mixer_layer), _row_spec(TM, D),
        ] + _ffn_specs(layer),
        out_specs=_row_spec(TM, D),
        out_shape=jax.ShapeDtypeStruct((N_ALL, D), F32),
        compiler_params=_params(1),
        name="attn_tail",
    )(a_lat, a_ctx, w_o, x, mod, g2, w1, w3, w2)


def _conv_tail(b, u, conv_w, w_out, mixer_layer, x, mod, g2, w1, w3, w2, layer, final_gain, n_rows, final):
    per = TM // HALO
    n_halo = n_rows // HALO
    return pl.pallas_call(
        functools.partial(_conv_tail_kernel, final=final),
        grid=(n_rows // TM,),
        in_specs=[
            _row_spec(TM, D), _row_spec(TM, D),
            pl.BlockSpec((HALO, D), lambda i: (jnp.maximum(i * per - 1, 0), 0)),
            pl.BlockSpec((HALO, D), lambda i: (jnp.minimum((i + 1) * per, n_halo - 1), 0)),
            _resident((3, D)), _resident((D, D), mixer_layer), _row_spec(TM, D),
        ] + _ffn_specs(layer) + [_resident((1, D))],
        out_specs=_row_spec(TM, D),
        out_shape=jax.ShapeDtypeStruct((n_rows, D), F32),
        compiler_params=_params(1),
        name="conv_tail",
    )(b, u, u, u, conv_w, w_out, x, mod, g2, w1, w3, w2, final_gain)


def _axial_angles(rot_dim):
    n = rot_dim // 4
    rows = jnp.repeat(jnp.arange(SEQ // GRID_W, dtype=F32), GRID_W)
    cols = jnp.tile(jnp.arange(GRID_W, dtype=F32), SEQ // GRID_W)
    freqs = ROPE_THETA ** (-jnp.arange(n, dtype=F32) / n)
    return jnp.concatenate([rows[:, None] * freqs, cols[:, None] * freqs], axis=-1)


def _rope_tables(rot_dim):
    ang = _axial_angles(rot_dim)
    copies = LANES // (rot_dim // 2)
    cos_t = jnp.concatenate([jnp.tile(jnp.cos(ang), (1, copies)), jnp.ones((TP, LANES), F32)], axis=0)
    sin_t = jnp.concatenate([jnp.tile(jnp.sin(ang), (1, copies)), jnp.zeros((TP, LANES), F32)], axis=0)
    return cos_t, sin_t


def _gqa_slab_pairs(w, n_kv, n_group):
    half = GQA_HEAD_DIM // 2
    w = w.reshape(w.shape[0], n_kv, n_group, 2, half)
    return jnp.transpose(w, (0, 2, 3, 1, 4)).reshape(w.shape[0], n_group * GQA_QK)


def _mla_weights(w_dq, w_uq, w_dkv, w_ukv):
    half = MLA_ROPE // 2
    qk = MLA_NOPE + MLA_ROPE
    uq = w_uq.reshape(MLA_Q_RANK, MLA_HEADS // 2, 2, qk)
    nope = uq[..., :MLA_NOPE].reshape(MLA_Q_RANK, MLA_HEADS // 2, 2 * MLA_NOPE)
    x1 = uq[..., MLA_NOPE:MLA_NOPE + half].reshape(MLA_Q_RANK, MLA_HEADS // 2, 2 * half)
    x2 = uq[..., MLA_NOPE + half:].reshape(MLA_Q_RANK, MLA_HEADS // 2, 2 * half)
    uq = jnp.concatenate([nope, x1, x1, x2, x2], axis=-1).reshape(MLA_Q_RANK, MLA_HEADS * MLA_QK)
    k1 = w_dkv[:, MLA_KV_RANK:MLA_KV_RANK + half]
    k2 = w_dkv[:, MLA_KV_RANK + half:]
    dkv = jnp.concatenate([w_dkv[:, :MLA_KV_RANK]] + [k1] * 4 + [k2] * 4, axis=-1)
    ukv = w_ukv.reshape(MLA_KV_RANK, MLA_HEADS, MLA_NOPE + MLA_V)
    ukv = jnp.concatenate([ukv[:, :, :MLA_NOPE].reshape(MLA_KV_RANK, -1),
                           ukv[:, :, MLA_NOPE:].reshape(MLA_KV_RANK, -1)], axis=-1)
    return w_dq.astype(BF16), uq.astype(BF16), dkv.astype(BF16), ukv.astype(BF16)


def kernel(x, c, ctx, c_ctx, ada_w, ada_b, norm1_g, norm2_g, ffn_w1, ffn_w3, ffn_w2, conv_w_in, conv_w, conv_w_out, gqa_wq, gqa_wk, gqa_wv, gqa_q_norm, gqa_k_norm, gqa_wo, mla_w_dq, mla_q_norm, mla_w_uq, mla_w_dkv, mla_kv_norm, mla_w_ukv, mla_wo, final_g):
    assert x.shape == (BATCH, SEQ, D) and ctx.shape == (BATCH, CTX_LEN, D)
    cond = jnp.concatenate(
        [c, c_ctx[None], jnp.zeros((MOD_ROWS - BATCH - 1, D), F32)], axis=0)
    mods = _modulation_tables(cond, ada_w, ada_b).reshape(DEPTH, MOD_ROWS, N_MOD, D)

    gqa_cos, gqa_sin = _rope_tables(GQA_HEAD_DIM)
    lane_half = jnp.arange(LANES) // (LANES // 2)
    half_ones = jnp.tile((lane_half[:, None] == lane_half[None, :]).astype(BF16), (2, 1))
    mla_cos, mla_sin = _rope_tables(MLA_ROPE)

    ffn_w1, ffn_w3, ffn_w2 = ffn_w1.astype(BF16), ffn_w3.astype(BF16), ffn_w2.astype(BF16)
    conv_w_in, conv_w_out = conv_w_in.astype(BF16), conv_w_out.astype(BF16)
    gqa_wo = gqa_wo.astype(BF16)
    mla_wo = mla_wo.astype(BF16)

    xs = None
    for i in range(DEPTH):
        kind, j = i % 3, i // 3
        mod = mods[i]
        last = i == DEPTH - 1
        n_rows = N_LAT if last else N_ALL
        g1 = norm1_g[i][None]
        ffn = (norm2_g[i][None], ffn_w1, ffn_w3, ffn_w2, i)
        if kind == 0:
            if i == 0:
                b, u, xs = _conv_in_first(x.reshape(N_LAT, D), ctx.reshape(N_CTX, D), mod, g1, conv_w_in, j)
            else:
                b, u = _conv_in(xs, mod, g1, conv_w_in, j, n_rows)
            xs = _conv_tail(b, u, conv_w[j], conv_w_out, j, xs, mod, *ffn, final_g[None], n_rows, last)
        elif kind == 1:
            w_qkv = jnp.concatenate([_gqa_slab_pairs(gqa_wq[j].astype(BF16), GQA_KV_HEADS, GQA_GROUP),
                                     _gqa_slab_pairs(gqa_wk[j].astype(BF16), GQA_KV_HEADS, 1),
                                     gqa_wv[j].astype(BF16)], axis=-1)
            q_gain = _gqa_slab_pairs(jnp.tile(gqa_q_norm[j], GQA_KV_HEADS)[None], GQA_KV_HEADS, 1)
            k_gain = _gqa_slab_pairs(jnp.tile(gqa_k_norm[j], GQA_KV_HEADS)[None], GQA_KV_HEADS, 1)
            q_gain = q_gain * (GQA_HEAD_DIM ** -0.5 * LOG2E)
            score_bound = 1.02 * GQA_HEAD_DIM * jnp.max(jnp.abs(q_gain)) * jnp.max(jnp.abs(k_gain))
            q, k, vt = _gqa_proj(xs, mod, g1, w_qkv, q_gain, k_gain, half_ones, gqa_cos, gqa_sin)
            a = _attention(q, k, vt, score_bound, kv_heads=GQA_KV_HEADS, group=GQA_GROUP,
                           dq=GQA_QK, dv=GQA_HEAD_DIM, heads_per_step=1, shared_k=True)
            xs = _attn_tail(*a, gqa_wo, j, xs, mod, *ffn)
        else:
            w_dq, w_uq, w_dkv, w_ukv = _mla_weights(mla_w_dq[j], mla_w_uq[j], mla_w_dkv[j], mla_w_ukv[j])
            q_gain = mla_q_norm[j][None] * ((MLA_NOPE + MLA_ROPE) ** -0.5 * LOG2E)
            q, k, vt = _mla_proj(xs, mod, g1, w_dq, q_gain, w_uq, w_dkv,
                                 mla_kv_norm[j][None], w_ukv, mla_cos, mla_sin)
            a = _attention(q, k, vt, None, kv_heads=MLA_HEADS, group=1, dq=MLA_QK, dv=MLA_V, heads_per_step=4,
                           shared_k=False)
            xs = _attn_tail(*a, mla_wo, j, xs, mod, *ffn)

    assert (DEPTH - 1) % 3 == 0
    return xs.reshape(BATCH, SEQ, D)
```

```python
import functools

import jax
import jax.numpy as jnp
import numpy as np
from jax import lax
from jax.experimental import pallas as pl
from jax.experimental.pallas import tpu as pltpu

D = 1024
BATCH = 8
SEQ = 2048
DEPTH = 4
GRID_W = 64
CTX_LEN = 256
N_MOD = 6
FFN_HIDDEN = 2816
ROPE_THETA = 10000.0
EPS = 1e-6
GQA_HEAD_DIM = 128
GQA_HEADS = 8
GQA_KV_HEADS = 2
GQA_GROUP = 4
MLA_HEADS = 8
MLA_NOPE = 128
MLA_ROPE = 64
MLA_V = 128
MLA_KV_RANK = 256
MLA_Q_RANK = 768
MLA_QK = 256
GQA_QK = 256
LANES = 128
SUBLANES = 8
VT_ROWS = 128

N_LAT = BATCH * SEQ
N_CTX = BATCH * CTX_LEN
N_ALL = N_LAT + N_CTX
MOD_ROWS = 16

TM = 512
TP = 1024
TQ = 1024
HALO = 8
VMEM_LIMIT = 56 * 1024 * 1024
LOG2E = 1.4426950408889634
MAX_BOUND_SHIFT = 50.0

F32 = jnp.float32
BF16 = jnp.bfloat16


def _params(n_axes):
    return pltpu.CompilerParams(
        dimension_semantics=("arbitrary",) * n_axes, vmem_limit_bytes=VMEM_LIMIT)


def _resident(shape, layer=None):
    nd = len(shape)
    if layer is None:
        return pl.BlockSpec(shape, lambda *_: (0,) * nd, pipeline_mode=pl.Buffered(1))
    return pl.BlockSpec((1,) + tuple(shape), lambda *_: (layer,) + (0,) * nd, pipeline_mode=pl.Buffered(1))


def _w(ref):
    return ref[0] if len(ref.shape) == 3 else ref[...]


def _mod_spec(tm):
    return pl.BlockSpec((1, N_MOD, D), lambda i: (jnp.minimum(i * tm // SEQ, BATCH), 0, 0))


def _row_spec(tm, width):
    return pl.BlockSpec((tm, width), lambda i: (i, 0))


def _col_spec(height, tm):
    return pl.BlockSpec((height, tm), lambda i: (0, i))


def _dot(a, b):
    return jnp.dot(a, b, preferred_element_type=F32)


def _modulated_norm(x, gain, shift, scale):
    r = lax.rsqrt(jnp.mean(x * x, axis=-1, keepdims=True) + EPS)
    return (x * r) * (gain * (1.0 + scale)) + shift


def _head_norm(x, gain):
    r = lax.rsqrt(jnp.mean(x * x, axis=-1, keepdims=True) + EPS)
    return (x * r) * gain


def _store_vt(vt_ref, v, n_heads):
    for j in range(n_heads):
        vt_ref[j * VT_ROWS:(j + 1) * VT_ROWS, :] = v[:, j * VT_ROWS:(j + 1) * VT_ROWS].T.astype(BF16)


def _rope(a, b, cos, sin):
    return a * cos - b * sin, a * sin + b * cos


def _lane_mask(lo, hi):
    lane = lax.broadcasted_iota(jnp.int32, (1, LANES), 1)
    return (lane >= lo) & (lane < hi)


def _mod_kernel(cond_ref, w_ref, b_ref, o_ref):
    c = cond_ref[...]
    act = (c * jax.nn.sigmoid(c)).astype(BF16)
    o_ref[0] = _dot(act, w_ref[0].astype(BF16)) + b_ref[0]


def _modulation_tables(cond, ada_w, ada_b):
    tn = 1536
    return pl.pallas_call(
        _mod_kernel,
        grid=(DEPTH, N_MOD * D // tn),
        in_specs=[
            pl.BlockSpec((MOD_ROWS, D), lambda l, j: (0, 0)),
            pl.BlockSpec((1, D, tn), lambda l, j: (l, 0, j)),
            pl.BlockSpec((1, 1, tn), lambda l, j: (l, 0, j)),
        ],
        out_specs=pl.BlockSpec((1, MOD_ROWS, tn), lambda l, j: (l, 0, j)),
        out_shape=jax.ShapeDtypeStruct((DEPTH, MOD_ROWS, N_MOD * D), F32),
        compiler_params=_params(2),
        name="modulation",
    )(cond, ada_w, ada_b.reshape(DEPTH, 1, N_MOD * D))


def _conv_in_compute(x, mod_ref, g_ref, w_ref, b_ref, u_ref):
    m = mod_ref[0]
    h = _modulated_norm(x, g_ref[...], m[0:1], m[1:2]).astype(BF16)
    y = _dot(h, _w(w_ref))
    b_ref[...] = y[:, :D].astype(BF16)
    u_ref[...] = y[:, D:2 * D] * y[:, 2 * D:]


def _conv_in_kernel(x_ref, mod_ref, g_ref, w_ref, b_ref, u_ref):
    _conv_in_compute(x_ref[...], mod_ref, g_ref, w_ref, b_ref, u_ref)


def _conv_in_first_kernel(lat_ref, ctx_ref, mod_ref, g_ref, w_ref, b_ref, u_ref, xs_ref):
    x = jnp.where(pl.program_id(0) < N_LAT // TP, lat_ref[...], ctx_ref[...])
    xs_ref[...] = x
    _conv_in_compute(x, mod_ref, g_ref, w_ref, b_ref, u_ref)


def _conv_in(x, mod, gain, w_in, layer, n_rows):
    return pl.pallas_call(
        _conv_in_kernel,
        grid=(n_rows // TP,),
        in_specs=[_row_spec(TP, D), _mod_spec(TP), _resident((1, D)), _resident((D, 3 * D), layer)],
        out_specs=[_row_spec(TP, D), _row_spec(TP, D)],
        out_shape=[jax.ShapeDtypeStruct((n_rows, D), BF16), jax.ShapeDtypeStruct((n_rows, D), F32)],
        compiler_params=_params(1),
        name="conv_in",
    )(x, mod, gain, w_in)


def _conv_in_first(lat, ctx, mod, gain, w_in, layer):
    lat_tiles = N_LAT // TP
    return pl.pallas_call(
        _conv_in_first_kernel,
        grid=(N_ALL // TP,),
        in_specs=[
            pl.BlockSpec((TP, D), lambda i: (jnp.minimum(i, lat_tiles - 1), 0)),
            pl.BlockSpec((TP, D), lambda i: (jnp.maximum(i - lat_tiles, 0), 0)),
            _mod_spec(TP), _resident((1, D)), _resident((D, 3 * D), layer),
        ],
        out_specs=[_row_spec(TP, D), _row_spec(TP, D), _row_spec(TP, D)],
        out_shape=[jax.ShapeDtypeStruct((N_ALL, D), BF16), jax.ShapeDtypeStruct((N_ALL, D), F32),
                   jax.ShapeDtypeStruct((N_ALL, D), F32)],
        compiler_params=_params(1),
        name="conv_in_first",
    )(lat, ctx, mod, gain, w_in)


def _conv_gate(b_ref, u_ref, up_ref, un_ref, cw_ref):
    tm = u_ref.shape[0]
    u = u_ref[...]
    local = lax.broadcasted_iota(jnp.int32, (tm, 1), 0)
    row = local + pl.program_id(0) * tm
    in_ctx = row >= N_LAT
    first = ((row & (CTX_LEN - 1)) == 0) & (in_ctx | ((row & (SEQ - 1)) == 0))
    last = ((row & (CTX_LEN - 1)) == CTX_LEN - 1) & (in_ctx | ((row & (SEQ - 1)) == SEQ - 1))
    prev = jnp.where(local == 0, up_ref[HALO - 1:HALO, :], pltpu.roll(u, 1, axis=0))
    prev = jnp.where(first, 0.0, prev)
    nxt = jnp.where(local == tm - 1, un_ref[0:1, :], pltpu.roll(u, tm - 1, axis=0))
    nxt = jnp.where(last, 0.0, nxt)
    cw = cw_ref[...]
    z = prev * cw[0:1] + u * cw[1:2] + nxt * cw[2:3]
    return (b_ref[...].astype(F32) * z).astype(BF16)


def _gqa_proj_kernel(x_ref, mod_ref, g_ref, w_ref, qg_ref, kg_ref, ones_ref, cos_ref, sin_ref,
                     q_ref, k_ref, vt_ref):
    m = mod_ref[0]
    h = _modulated_norm(x_ref[...], g_ref[...], m[0:1], m[1:2]).astype(BF16)
    cos = cos_ref[...]
    sin = sin_ref[...]
    ones2 = ones_ref[...]

    def normed_rotated(y, gain):
        a, b = y[:, :LANES], y[:, LANES:]
        ss = a * a + b * b
        hi = ss.astype(BF16)
        lo = (ss - hi.astype(F32)).astype(BF16)
        tot = _dot(jnp.concatenate([hi, lo], axis=1), ones2)
        r = lax.rsqrt(tot * (1.0 / GQA_HEAD_DIM) + EPS)
        return _rope(a * r * gain[:, :LANES], b * r * gain[:, LANES:], cos, sin)

    wide = 2 * GQA_QK
    kv0 = _lane_mask(0, LANES // 2)
    for gg in range(GQA_GROUP // 2):
        y = _dot(h, w_ref[:, gg * wide:(gg + 1) * wide])
        for t in range(2):
            g = 2 * gg + t
            o1, o2 = normed_rotated(y[:, t * GQA_QK:(t + 1) * GQA_QK], qg_ref[...])
            for kv, keep in ((0, kv0), (1, jnp.logical_not(kv0))):
                c0 = (kv * GQA_GROUP + g) * GQA_QK
                q_ref[:, c0:c0 + LANES] = jnp.where(keep, o1, 0.0).astype(BF16)
                q_ref[:, c0 + LANES:c0 + GQA_QK] = jnp.where(keep, o2, 0.0).astype(BF16)
    y = _dot(h, w_ref[:, GQA_GROUP * GQA_QK:])
    o1, o2 = normed_rotated(y[:, :GQA_QK], kg_ref[...])
    k_ref[:, :LANES] = o1.astype(BF16)
    k_ref[:, LANES:] = o2.astype(BF16)
    _store_vt(vt_ref, y[:, GQA_QK:], GQA_KV_HEADS)


def _rope_spec(tm):
    per_seq = SEQ // tm
    return pl.BlockSpec(
        (tm, LANES), lambda i: (jnp.where(i < N_LAT // tm, i % per_seq, per_seq), 0))


def _gqa_proj(x, mod, gain, w_qkv, q_gain, k_gain, half_ones, cos, sin):
    kvw = GQA_KV_HEADS * GQA_HEAD_DIM
    qw = GQA_HEADS * GQA_QK
    return pl.pallas_call(
        _gqa_proj_kernel,
        grid=(N_ALL // TP,),
        in_specs=[
            _row_spec(TP, D), _mod_spec(TP), _resident((1, D)),
            _resident((D, D + 2 * kvw)),
            _resident((1, GQA_QK)), _resident((1, GQA_QK)), _resident((2 * LANES, LANES)),
            _rope_spec(TP), _rope_spec(TP),
        ],
        out_specs=[_row_spec(TP, qw), _row_spec(TP, GQA_QK), _col_spec(GQA_KV_HEADS * VT_ROWS, TP)],
        out_shape=[jax.ShapeDtypeStruct((N_ALL, qw), BF16),
                   jax.ShapeDtypeStruct((N_ALL, GQA_QK), BF16),
                   jax.ShapeDtypeStruct((GQA_KV_HEADS * VT_ROWS, N_ALL), BF16)],
        compiler_params=_params(1),
        name="gqa_proj",
    )(x, mod, gain, w_qkv, q_gain, k_gain, half_ones, cos, sin)


def _mla_proj_kernel(x_ref, mod_ref, g_ref, wdq_ref, qg_ref, wuq_ref, wdkv_ref, kvg_ref, wukv_ref,
                     ones_ref, cos_ref, sin_ref, q_ref, k_ref, vt_ref, qn_ref, kn_ref):
    m = mod_ref[0]
    h = _modulated_norm(x_ref[...], g_ref[...], m[0:1], m[1:2]).astype(BF16)
    cos = cos_ref[...]
    sin = sin_ref[...]
    half = MLA_ROPE // 2
    cq = _head_norm(_dot(h, wdq_ref[...]), qg_ref[...]).astype(BF16)
    q = _dot(cq, wuq_ref[...])
    ckv_pe = _dot(h, wdkv_ref[...])
    ckv = _head_norm(ckv_pe[:, :MLA_KV_RANK], kvg_ref[...]).astype(BF16)
    ko1, ko2 = _rope(ckv_pe[:, MLA_KV_RANK:MLA_KV_RANK + LANES], ckv_pe[:, MLA_KV_RANK + LANES:], cos, sin)
    k_pe = jnp.where(_lane_mask(0, 2 * half), ko1, ko2)
    k_pe_sq = k_pe * k_pe
    k_pe = k_pe.astype(BF16)
    kv = _dot(ckv, wukv_ref[...])
    ones2 = ones_ref[...]

    def max_sq_norm(sq_pair):
        rows = _dot(jnp.concatenate(sq_pair, axis=1).astype(BF16), ones2)
        return jnp.max(rows, axis=0, keepdims=True)

    q_norms, k_norms = [], []
    for pp in range(MLA_HEADS // 2):
        p0 = pp * 2 * MLA_QK
        o1, o2 = _rope(q[:, p0 + 2 * MLA_NOPE:p0 + 2 * MLA_NOPE + LANES], q[:, p0 + 2 * MLA_NOPE + LANES:p0 + 2 * MLA_QK],
                       cos, sin)
        q_sq, k_sq = [], []
        for t in range(2):
            j = 2 * pp + t
            c0 = j * MLA_QK
            pe = jnp.where(_lane_mask(t * half, (t + 1) * half), o1,
                           jnp.where(_lane_mask(2 * half + t * half, 2 * half + (t + 1) * half), o2, 0.0))
            q_nope = q[:, p0 + t * MLA_NOPE:p0 + (t + 1) * MLA_NOPE]
            k_nope = kv[:, j * MLA_NOPE:(j + 1) * MLA_NOPE]
            q_ref[:, c0:c0 + MLA_NOPE] = q_nope.astype(BF16)
            q_ref[:, c0 + MLA_NOPE:c0 + MLA_QK] = pe.astype(BF16)
            k_ref[:, c0:c0 + MLA_NOPE] = k_nope.astype(BF16)
            k_ref[:, c0 + MLA_NOPE:c0 + MLA_QK] = k_pe
            q_sq.append(q_nope * q_nope + pe * pe)
            k_sq.append(k_nope * k_nope + k_pe_sq)
        q_norms.append(max_sq_norm(q_sq))
        k_norms.append(max_sq_norm(k_sq))
    qn_ref[0] = jnp.broadcast_to(jnp.concatenate(q_norms, axis=1), qn_ref.shape[1:])
    kn_ref[0] = jnp.broadcast_to(jnp.concatenate(k_norms, axis=1), kn_ref.shape[1:])
    _store_vt(vt_ref, kv[:, MLA_HEADS * MLA_NOPE:], MLA_HEADS)


def _mla_proj(x, mod, gain, w_dq, q_gain, w_uq, w_dkv, kv_gain, w_ukv, pair_ones, cos, sin):
    qkw = MLA_HEADS * MLA_QK
    n_tiles = N_ALL // TP
    norm_spec = pl.BlockSpec((1, SUBLANES, MLA_HEADS * LANES), lambda i: (i, 0, 0))
    norm_shape = jax.ShapeDtypeStruct((n_tiles, SUBLANES, MLA_HEADS * LANES), F32)
    return pl.pallas_call(
        _mla_proj_kernel,
        grid=(N_ALL // TP,),
        in_specs=[
            _row_spec(TP, D), _mod_spec(TP), _resident((1, D)),
            _resident((D, MLA_Q_RANK)), _resident((1, MLA_Q_RANK)), _resident((MLA_Q_RANK, qkw)),
            _resident((D, MLA_KV_RANK + 2 * LANES)), _resident((1, MLA_KV_RANK)),
            _resident((MLA_KV_RANK, MLA_HEADS * (MLA_NOPE + MLA_V))),
            _resident((2 * LANES, 2 * LANES)),
            _rope_spec(TP), _rope_spec(TP),
        ],
        out_specs=[_row_spec(TP, qkw), _row_spec(TP, qkw), _col_spec(MLA_HEADS * VT_ROWS, TP),
                   norm_spec, norm_spec],
        out_shape=[jax.ShapeDtypeStruct((N_ALL, qkw), BF16),
                   jax.ShapeDtypeStruct((N_ALL, qkw), BF16),
                   jax.ShapeDtypeStruct((MLA_HEADS * VT_ROWS, N_ALL), BF16),
                   norm_shape, norm_shape],
        compiler_params=_params(1),
        name="mla_proj",
    )(x, mod, gain, w_dq, q_gain, w_uq, w_dkv, kv_gain, w_ukv, pair_ones, cos, sin)


def _score_bounds(qn, kn):
    qn = qn[:, 0, ::LANES]
    kn = kn[:, 0, ::LANES]
    per_seq = SEQ // TP
    q_lat = jnp.max(qn[:N_LAT // TP].reshape(BATCH, per_seq, -1), axis=1)
    k_lat = jnp.max(kn[:N_LAT // TP].reshape(BATCH, per_seq, -1), axis=1)
    ctx_tile = N_LAT // TP + (jnp.arange(BATCH) * CTX_LEN) // TP
    k_all = jnp.maximum(k_lat, kn[ctx_tile])
    return 1.02 * jnp.sqrt(jnp.max(q_lat * k_all, axis=1))


def _nt_dot(a, b):
    return lax.dot_general(a, b, (((1,), (1,)), ((), ())), preferred_element_type=F32)


def _attn_scores(q, k_parts):
    return [_nt_dot(k, q) for k in k_parts]


def _attn_kernel(bound_ref, q_ref, qc_ref, kl_ref, kc_ref, vtl_ref, vtc_ref, o_ref, oc_ref,
                 *, heads, group, dq, dv, shared_k):
    n = heads * group

    def k_cols(qh):
        kv = 0 if shared_k else qh // group
        return slice(kv * dq, (kv + 1) * dq)

    def v_rows(qh):
        return slice(qh // group * VT_ROWS, (qh // group + 1) * VT_ROWS)

    def run(queries, out_ref, k_refs, vt_refs, shift=None):
        def scores(qh):
            return _attn_scores(queries[:, qh * dq:(qh + 1) * dq], [k[:, k_cols(qh)] for k in k_refs])

        ahead = 2 if shift is None else 1
        pending = [scores(qh) for qh in range(min(ahead, n))]
        for qh in range(n):
            cur = pending.pop(0)
            if shift is None:
                mx = functools.reduce(jnp.maximum, [jnp.max(s, axis=0, keepdims=True) for s in cur])
            else:
                mx = shift
            probs = [jnp.exp2(s - mx) for s in cur]
            den = sum(jnp.sum(p, axis=0, keepdims=True) for p in probs)
            probs = [p.astype(BF16) for p in probs]
            if qh + ahead < n:
                pending.append(scores(qh + ahead))
            acc = sum(_dot(vt[v_rows(qh), :], p) for vt, p in zip(vt_refs, probs))
            out_ref[:, qh * dv:(qh + 1) * dv] = (acc / den).T.astype(BF16)

    lat = (q_ref, o_ref, [kl_ref, kc_ref], [vtl_ref, vtc_ref])
    bound = bound_ref[0, pl.program_id(0)]

    @pl.when(bound <= MAX_BOUND_SHIFT)
    def _():
        run(*lat, shift=bound)

    @pl.when(jnp.logical_not(bound <= MAX_BOUND_SHIFT))
    def _():
        run(*lat)

    @pl.when(pl.program_id(2) == 0)
    def _():
        run(qc_ref, oc_ref, [kc_ref], [vtc_ref])


def _attention(q, k, vt, score_bounds, *, kv_heads, group, dq, dv, heads_per_step, shared_k):
    hp = heads_per_step
    lat_tiles = SEQ // TQ
    ctx_block0 = N_LAT // CTX_LEN
    kw = dq if shared_k else hp * dq
    k_col = (lambda h: 0) if shared_k else (lambda h: h)
    width = kv_heads * group * dv
    return pl.pallas_call(
        functools.partial(_attn_kernel, heads=hp, group=group, dq=dq, dv=dv, shared_k=shared_k),
        grid=(BATCH, kv_heads // hp, lat_tiles),
        in_specs=[
            pl.BlockSpec(memory_space=pltpu.SMEM),
            pl.BlockSpec((TQ, hp * group * dq), lambda b, h, t: (b * lat_tiles + t, h)),
            pl.BlockSpec((CTX_LEN, hp * group * dq), lambda b, h, t: (ctx_block0 + b, h)),
            pl.BlockSpec((SEQ, kw), lambda b, h, t: (b, k_col(h))),
            pl.BlockSpec((CTX_LEN, kw), lambda b, h, t: (ctx_block0 + b, k_col(h))),
            pl.BlockSpec((hp * VT_ROWS, SEQ), lambda b, h, t: (h, b)),
            pl.BlockSpec((hp * VT_ROWS, CTX_LEN), lambda b, h, t: (h, ctx_block0 + b)),
        ],
        out_specs=[pl.BlockSpec((TQ, hp * group * dv), lambda b, h, t: (b * lat_tiles + t, h)),
                   pl.BlockSpec((CTX_LEN, hp * group * dv), lambda b, h, t: (b, h))],
        out_shape=[jax.ShapeDtypeStruct((N_LAT, width), BF16), jax.ShapeDtypeStruct((N_CTX, width), BF16)],
        compiler_params=_params(3),
        name="attention",
    )(score_bounds.reshape(1, BATCH).astype(F32), q, q, k, k, vt, vt)


def _ffn_tail(x1, m, g2_ref, w1_ref, w3_ref, w2_ref):
    h = _modulated_norm(x1, g2_ref[...], m[3:4], m[4:5]).astype(BF16)
    a = _dot(h, _w(w1_ref))
    b = _dot(h, _w(w3_ref))
    gated = (a * jax.nn.sigmoid(a) * b).astype(BF16)
    return x1 + m[5:6] * _dot(gated, _w(w2_ref))


def _attn_tail_kernel(al_ref, ac_ref, wo_ref, x_ref, mod_ref, g2_ref, w1_ref, w3_ref, w2_ref, o_ref):
    m = mod_ref[0]
    a = jnp.where(pl.program_id(0) < N_LAT // TM, al_ref[...], ac_ref[...])
    x1 = x_ref[...] + m[2:3] * _dot(a, _w(wo_ref))
    o_ref[...] = _ffn_tail(x1, m, g2_ref, w1_ref, w3_ref, w2_ref)


def _conv_tail_kernel(b_ref, u_ref, up_ref, un_ref, cw_ref, wo_ref, x_ref, mod_ref,
                      g2_ref, w1_ref, w3_ref, w2_ref, fg_ref, o_ref, *, final):
    m = mod_ref[0]
    x1 = x_ref[...] + m[2:3] * _dot(_conv_gate(b_ref, u_ref, up_ref, un_ref, cw_ref), _w(wo_ref))
    x2 = _ffn_tail(x1, m, g2_ref, w1_ref, w3_ref, w2_ref)
    o_ref[...] = _head_norm(x2, fg_ref[...]) if final else x2


def _ffn_specs(layer):
    return [_mod_spec(TM), _resident((1, D)), _resident((D, FFN_HIDDEN), layer),
            _resident((D, FFN_HIDDEN), layer), _resident((FFN_HIDDEN, D), layer)]


def _attn_tail(a_lat, a_ctx, w_o, mixer_layer, x, mod, g2, w1, w3, w2, layer):
    lat_tiles = N_LAT // TM
    return pl.pallas_call(
        _attn_tail_kernel,
        grid=(N_ALL // TM,),
        in_specs=[
            pl.BlockSpec((TM, D), lambda i: (jnp.minimum(i, lat_tiles - 1), 0)),
            pl.BlockSpec((TM, D), lambda i: (jnp.maximum(i - lat_tiles, 0), 0)),
            _resident((D, D), mixer_layer), _row_spec(TM, D),
        ] + _ffn_specs(layer),
        out_specs=_row_spec(TM, D),
        out_shape=jax.ShapeDtypeStruct((N_ALL, D), F32),
        compiler_params=_params(1),
        name="attn_tail",
    )(a_lat, a_ctx, w_o, x, mod, g2, w1, w3, w2)


def _conv_tail(b, u, conv_w, w_out, mixer_layer, x, mod, g2, w1, w3, w2, layer, final_gain, n_rows, final):
    per = TM // HALO
    n_halo = n_rows // HALO
    return pl.pallas_call(
        functools.partial(_conv_tail_kernel, final=final),
        grid=(n_rows // TM,),
        in_specs=[
            _row_spec(TM, D), _row_spec(TM, D),
            pl.BlockSpec((HALO, D), lambda i: (jnp.maximum(i * per - 1, 0), 0)),
            pl.BlockSpec((HALO, D), lambda i: (jnp.minimum((i + 1) * per, n_halo - 1), 0)),
            _resident((3, D)), _resident((D, D), mixer_layer), _row_spec(TM, D),
        ] + _ffn_specs(layer) + [_resident((1, D))],
        out_specs=_row_spec(TM, D),
        out_shape=jax.ShapeDtypeStruct((n_rows, D), F32),
        compiler_params=_params(1),
        name="conv_tail",
    )(b, u, u, u, conv_w, w_out, x, mod, g2, w1, w3, w2, final_gain)


def _axial_angles(rot_dim):
    n = rot_dim // 4
    rows = jnp.repeat(jnp.arange(SEQ // GRID_W, dtype=F32), GRID_W)
    cols = jnp.tile(jnp.arange(GRID_W, dtype=F32), SEQ // GRID_W)
    freqs = ROPE_THETA ** (-jnp.arange(n, dtype=F32) / n)
    return jnp.concatenate([rows[:, None] * freqs, cols[:, None] * freqs], axis=-1)


def _rope_tables(rot_dim):
    ang = _axial_angles(rot_dim)
    copies = LANES // (rot_dim // 2)
    cos_t = jnp.concatenate([jnp.tile(jnp.cos(ang), (1, copies)), jnp.ones((TP, LANES), F32)], axis=0)
    sin_t = jnp.concatenate([jnp.tile(jnp.sin(ang), (1, copies)), jnp.zeros((TP, LANES), F32)], axis=0)
    return cos_t, sin_t


def _gqa_slab_pairs(w, n_kv, n_group):
    half = GQA_HEAD_DIM // 2
    w = w.reshape(w.shape[0], n_kv, n_group, 2, half)
    return jnp.transpose(w, (0, 2, 3, 1, 4)).reshape(w.shape[0], n_group * GQA_QK)


def _mla_weights(w_dq, w_uq, w_dkv, w_ukv):
    half = MLA_ROPE // 2
    qk = MLA_NOPE + MLA_ROPE
    uq = w_uq.reshape(MLA_Q_RANK, MLA_HEADS // 2, 2, qk)
    nope = uq[..., :MLA_NOPE].reshape(MLA_Q_RANK, MLA_HEADS // 2, 2 * MLA_NOPE)
    x1 = uq[..., MLA_NOPE:MLA_NOPE + half].reshape(MLA_Q_RANK, MLA_HEADS // 2, 2 * half)
    x2 = uq[..., MLA_NOPE + half:].reshape(MLA_Q_RANK, MLA_HEADS // 2, 2 * half)
    uq = jnp.concatenate([nope, x1, x1, x2, x2], axis=-1).reshape(MLA_Q_RANK, MLA_HEADS * MLA_QK)
    k1 = w_dkv[:, MLA_KV_RANK:MLA_KV_RANK + half]
    k2 = w_dkv[:, MLA_KV_RANK + half:]
    dkv = jnp.concatenate([w_dkv[:, :MLA_KV_RANK]] + [k1] * 4 + [k2] * 4, axis=-1)
    ukv = w_ukv.reshape(MLA_KV_RANK, MLA_HEADS, MLA_NOPE + MLA_V)
    ukv = jnp.concatenate([ukv[:, :, :MLA_NOPE].reshape(MLA_KV_RANK, -1),
                           ukv[:, :, MLA_NOPE:].reshape(MLA_KV_RANK, -1)], axis=-1)
    return w_dq.astype(BF16), uq.astype(BF16), dkv.astype(BF16), ukv.astype(BF16)


def kernel(x, c, ctx, c_ctx, ada_w, ada_b, norm1_g, norm2_g, ffn_w1, ffn_w3, ffn_w2, conv_w_in, conv_w, conv_w_out, gqa_wq, gqa_wk, gqa_wv, gqa_q_norm, gqa_k_norm, gqa_wo, mla_w_dq, mla_q_norm, mla_w_uq, mla_w_dkv, mla_kv_norm, mla_w_ukv, mla_wo, final_g):
    assert x.shape == (BATCH, SEQ, D) and ctx.shape == (BATCH, CTX_LEN, D)
    cond = jnp.concatenate(
        [c, c_ctx[None], jnp.zeros((MOD_ROWS - BATCH - 1, D), F32)], axis=0)
    mods = _modulation_tables(cond, ada_w, ada_b).reshape(DEPTH, MOD_ROWS, N_MOD, D)

    gqa_cos, gqa_sin = _rope_tables(GQA_HEAD_DIM)
    lane_half = jnp.arange(LANES) // (LANES // 2)
    half_ones = jnp.tile((lane_half[:, None] == lane_half[None, :]).astype(BF16), (2, 1))
    lane_head = jnp.arange(2 * LANES) // LANES
    pair_ones = (lane_head[:, None] == lane_head[None, :]).astype(BF16)
    mla_cos, mla_sin = _rope_tables(MLA_ROPE)

    ffn_w1, ffn_w3, ffn_w2 = ffn_w1.astype(BF16), ffn_w3.astype(BF16), ffn_w2.astype(BF16)
    conv_w_in, conv_w_out = conv_w_in.astype(BF16), conv_w_out.astype(BF16)
    gqa_wo = gqa_wo.astype(BF16)
    mla_wo = mla_wo.astype(BF16)

    xs = None
    for i in range(DEPTH):
        kind, j = i % 3, i // 3
        mod = mods[i]
        last = i == DEPTH - 1
        n_rows = N_LAT if last else N_ALL
        g1 = norm1_g[i][None]
        ffn = (norm2_g[i][None], ffn_w1, ffn_w3, ffn_w2, i)
        if kind == 0:
            if i == 0:
                b, u, xs = _conv_in_first(x.reshape(N_LAT, D), ctx.reshape(N_CTX, D), mod, g1, conv_w_in, j)
            else:
                b, u = _conv_in(xs, mod, g1, conv_w_in, j, n_rows)
            xs = _conv_tail(b, u, conv_w[j], conv_w_out, j, xs, mod, *ffn, final_g[None], n_rows, last)
        elif kind == 1:
            w_qkv = jnp.concatenate([_gqa_slab_pairs(gqa_wq[j].astype(BF16), GQA_KV_HEADS, GQA_GROUP),
                                     _gqa_slab_pairs(gqa_wk[j].astype(BF16), GQA_KV_HEADS, 1),
                                     gqa_wv[j].astype(BF16)], axis=-1)
            q_gain = _gqa_slab_pairs(jnp.tile(gqa_q_norm[j], GQA_KV_HEADS)[None], GQA_KV_HEADS, 1)
            k_gain = _gqa_slab_pairs(jnp.tile(gqa_k_norm[j], GQA_KV_HEADS)[None], GQA_KV_HEADS, 1)
            q_gain = q_gain * (GQA_HEAD_DIM ** -0.5 * LOG2E)
            bound = 1.02 * GQA_HEAD_DIM * jnp.max(jnp.abs(q_gain)) * jnp.max(jnp.abs(k_gain))
            q, k, vt = _gqa_proj(xs, mod, g1, w_qkv, q_gain, k_gain, half_ones, gqa_cos, gqa_sin)
            a = _attention(q, k, vt, jnp.full((BATCH,), bound), kv_heads=GQA_KV_HEADS, group=GQA_GROUP,
                           dq=GQA_QK, dv=GQA_HEAD_DIM, heads_per_step=1, shared_k=True)
            xs = _attn_tail(*a, gqa_wo, j, xs, mod, *ffn)
        else:
            w_dq, w_uq, w_dkv, w_ukv = _mla_weights(mla_w_dq[j], mla_w_uq[j], mla_w_dkv[j], mla_w_ukv[j])
            q_gain = mla_q_norm[j][None] * ((MLA_NOPE + MLA_ROPE) ** -0.5 * LOG2E)
            q, k, vt, qn, kn = _mla_proj(xs, mod, g1, w_dq, q_gain, w_uq, w_dkv,
                                         mla_kv_norm[j][None], w_ukv, pair_ones, mla_cos, mla_sin)
            a = _attention(q, k, vt, _score_bounds(qn, kn), kv_heads=MLA_HEADS, group=1, dq=MLA_QK, dv=MLA_V,
                           heads_per_step=4,
                           shared_k=False)
            xs = _attn_tail(*a, mla_wo, j, xs, mod, *ffn)

    assert (DEPTH - 1) % 3 == 0
    return xs.reshape(BATCH, SEQ, D)
```

```python
import functools

import jax
import jax.numpy as jnp
import numpy as np
from jax import lax
from jax.experimental import pallas as pl
from jax.experimental.pallas import tpu as pltpu

D = 1024
BATCH = 8
SEQ = 2048
DEPTH = 4
GRID_W = 64
CTX_LEN = 256
N_MOD = 6
FFN_HIDDEN = 2816
ROPE_THETA = 10000.0
EPS = 1e-6
GQA_HEAD_DIM = 128
GQA_HEADS = 8
GQA_KV_HEADS = 2
GQA_GROUP = 4
MLA_HEADS = 8
MLA_NOPE = 128
MLA_ROPE = 64
MLA_V = 128
MLA_KV_RANK = 256
MLA_Q_RANK = 768
MLA_QK = 256
GQA_QK = 256
LANES = 128
SUBLANES = 8
VT_ROWS = 128

N_LAT = BATCH * SEQ
N_CTX = BATCH * CTX_LEN
N_ALL = N_LAT + N_CTX
MOD_ROWS = 16

TM = 512
TP = 1024
TQ = 1024
HALO = 8
CONV_COLS = 256
CAST_CHUNKS = 16
VMEM_LIMIT = 56 * 1024 * 1024
LOG2E = 1.4426950408889634
MAX_BOUND_SHIFT = 50.0

F32 = jnp.float32
BF16 = jnp.bfloat16


def _params(n_axes):
    return pltpu.CompilerParams(
        dimension_semantics=("arbitrary",) * n_axes, vmem_limit_bytes=VMEM_LIMIT)


def _resident(shape, layer=None):
    nd = len(shape)
    if layer is None:
        return pl.BlockSpec(shape, lambda *_: (0,) * nd, pipeline_mode=pl.Buffered(1))
    return pl.BlockSpec((1,) + tuple(shape), lambda *_: (layer,) + (0,) * nd, pipeline_mode=pl.Buffered(1))


def _w(ref):
    return ref[0] if len(ref.shape) == 3 else ref[...]


def _ffn_cast_riders(ffn_w, layer, chunk_of):
    in_specs, out_specs, out_shapes = [], [], []
    for w in ffn_w:
        _, rows, cols = w.shape
        r = rows // CAST_CHUNKS
        in_specs.append(pl.BlockSpec((1, r, cols), lambda *g: (layer, chunk_of(*g), 0)))
        out_specs.append(pl.BlockSpec((r, cols), lambda *g: (chunk_of(*g), 0)))
        out_shapes.append(jax.ShapeDtypeStruct((rows, cols), BF16))
    return in_specs, out_specs, out_shapes


def _with_cast_riders(body, n_in, n_out):
    def kern(*refs):
        n_riders = (len(refs) - n_in - n_out) // 2
        ins, rider_in = refs[:n_in], refs[n_in:n_in + n_riders]
        outs = refs[n_in + n_riders:n_in + n_riders + n_out]
        rider_out = refs[n_in + n_riders + n_out:]
        body(*ins, *outs)
        for src, dst in zip(rider_in, rider_out):
            dst[...] = src[0].astype(dst.dtype)
    return kern


def _mod_spec(tm):
    return pl.BlockSpec((1, N_MOD, D), lambda i: (jnp.minimum(i * tm // SEQ, BATCH), 0, 0))


def _row_spec(tm, width):
    return pl.BlockSpec((tm, width), lambda i: (i, 0))


def _col_spec(height, tm):
    return pl.BlockSpec((height, tm), lambda i: (0, i))


def _dot(a, b):
    return jnp.dot(a, b, preferred_element_type=F32)


def _modulated_norm(x, gain, shift, scale):
    r = lax.rsqrt(jnp.mean(x * x, axis=-1, keepdims=True) + EPS)
    return (x * r) * (gain * (1.0 + scale)) + shift


def _head_norm(x, gain):
    r = lax.rsqrt(jnp.mean(x * x, axis=-1, keepdims=True) + EPS)
    return (x * r) * gain


def _store_vt(vt_ref, v, n_heads):
    for j in range(n_heads):
        vt_ref[j * VT_ROWS:(j + 1) * VT_ROWS, :] = v[:, j * VT_ROWS:(j + 1) * VT_ROWS].T.astype(BF16)


def _rope(a, b, cos, sin):
    return a * cos - b * sin, a * sin + b * cos


def _lane_mask(lo, hi):
    lane = lax.broadcasted_iota(jnp.int32, (1, LANES), 1)
    return (lane >= lo) & (lane < hi)


def _mod_kernel(cond_ref, w_ref, b_ref, o_ref):
    c = cond_ref[...]
    act = (c * jax.nn.sigmoid(c)).astype(BF16)
    o_ref[0] = _dot(act, w_ref[0].astype(BF16)) + b_ref[0]


def _modulation_tables(cond, ada_w, ada_b):
    tn = 1536
    return pl.pallas_call(
        _mod_kernel,
        grid=(DEPTH, N_MOD * D // tn),
        in_specs=[
            pl.BlockSpec((MOD_ROWS, D), lambda l, j: (0, 0)),
            pl.BlockSpec((1, D, tn), lambda l, j: (l, 0, j)),
            pl.BlockSpec((1, 1, tn), lambda l, j: (l, 0, j)),
        ],
        out_specs=pl.BlockSpec((1, MOD_ROWS, tn), lambda l, j: (l, 0, j)),
        out_shape=jax.ShapeDtypeStruct((DEPTH, MOD_ROWS, N_MOD * D), F32),
        compiler_params=_params(2),
        name="modulation",
    )(cond, ada_w, ada_b.reshape(DEPTH, 1, N_MOD * D))


def _conv_in_compute(x, mod_ref, g_ref, w_ref, b_ref, u_ref):
    m = mod_ref[0]
    h = _modulated_norm(x, g_ref[...], m[0:1], m[1:2]).astype(BF16)
    y = _dot(h, _w(w_ref))
    b_ref[...] = y[:, :D].astype(BF16)
    u_ref[...] = y[:, D:2 * D] * y[:, 2 * D:]


def _conv_in_kernel(x_ref, mod_ref, g_ref, w_ref, b_ref, u_ref):
    _conv_in_compute(x_ref[...], mod_ref, g_ref, w_ref, b_ref, u_ref)


def _conv_in_first_kernel(lat_ref, ctx_ref, mod_ref, g_ref, w_ref, b_ref, u_ref, xs_ref):
    x = jnp.where(pl.program_id(0) < N_LAT // TP, lat_ref[...], ctx_ref[...])
    xs_ref[...] = x
    _conv_in_compute(x, mod_ref, g_ref, w_ref, b_ref, u_ref)


def _conv_in(x, mod, gain, w_in, layer, n_rows, ffn_w, ffn_layer):
    r_in, r_out, r_shape = _ffn_cast_riders(ffn_w, ffn_layer, lambda i: jnp.minimum(i, CAST_CHUNKS - 1))
    assert n_rows // TP >= CAST_CHUNKS
    return pl.pallas_call(
        _with_cast_riders(_conv_in_kernel, 4, 2),
        grid=(n_rows // TP,),
        in_specs=[_row_spec(TP, D), _mod_spec(TP), _resident((1, D)), _resident((D, 3 * D), layer)] + r_in,
        out_specs=[_row_spec(TP, D), _row_spec(TP, D)] + r_out,
        out_shape=[jax.ShapeDtypeStruct((n_rows, D), BF16), jax.ShapeDtypeStruct((n_rows, D), F32)] + r_shape,
        compiler_params=_params(1),
        name="conv_in",
    )(x, mod, gain, w_in, *ffn_w)


def _conv_in_first(lat, ctx, mod, gain, w_in, layer, ffn_w, ffn_layer):
    lat_tiles = N_LAT // TP
    r_in, r_out, r_shape = _ffn_cast_riders(ffn_w, ffn_layer, lambda i: jnp.minimum(i, CAST_CHUNKS - 1))
    assert N_ALL // TP >= CAST_CHUNKS
    return pl.pallas_call(
        _with_cast_riders(_conv_in_first_kernel, 5, 3),
        grid=(N_ALL // TP,),
        in_specs=[
            pl.BlockSpec((TP, D), lambda i: (jnp.minimum(i, lat_tiles - 1), 0)),
            pl.BlockSpec((TP, D), lambda i: (jnp.maximum(i - lat_tiles, 0), 0)),
            _mod_spec(TP), _resident((1, D)), _resident((D, 3 * D), layer),
        ] + r_in,
        out_specs=[_row_spec(TP, D), _row_spec(TP, D), _row_spec(TP, D)] + r_out,
        out_shape=[jax.ShapeDtypeStruct((N_ALL, D), BF16), jax.ShapeDtypeStruct((N_ALL, D), F32),
                   jax.ShapeDtypeStruct((N_ALL, D), F32)] + r_shape,
        compiler_params=_params(1),
        name="conv_in_first",
    )(lat, ctx, mod, gain, w_in, *ffn_w)


def _conv_gate(b_ref, u_ref, up_ref, un_ref, cw_ref, cols):
    tm = u_ref.shape[0]
    u = u_ref[:, cols]
    local = lax.broadcasted_iota(jnp.int32, (tm, 1), 0)
    row = local + pl.program_id(0) * tm
    in_ctx = row >= N_LAT
    first = ((row & (CTX_LEN - 1)) == 0) & (in_ctx | ((row & (SEQ - 1)) == 0))
    last = ((row & (CTX_LEN - 1)) == CTX_LEN - 1) & (in_ctx | ((row & (SEQ - 1)) == SEQ - 1))
    prev = jnp.where(local == 0, up_ref[HALO - 1:HALO, cols], pltpu.roll(u, 1, axis=0))
    prev = jnp.where(first, 0.0, prev)
    nxt = jnp.where(local == tm - 1, un_ref[0:1, cols], pltpu.roll(u, tm - 1, axis=0))
    nxt = jnp.where(last, 0.0, nxt)
    cw = cw_ref[:, cols]
    z = prev * cw[0:1] + u * cw[1:2] + nxt * cw[2:3]
    return (b_ref[:, cols].astype(F32) * z).astype(BF16)


def _conv_mix(b_ref, u_ref, up_ref, un_ref, cw_ref, wo_ref):
    wo = wo_ref.at[0] if len(wo_ref.shape) == 3 else wo_ref
    acc = None
    for c0 in range(0, D, CONV_COLS):
        cols = slice(c0, c0 + CONV_COLS)
        part = _dot(_conv_gate(b_ref, u_ref, up_ref, un_ref, cw_ref, cols), wo[cols, :])
        acc = part if acc is None else acc + part
    return acc


def _gqa_proj_kernel(x_ref, mod_ref, g_ref, w_ref, qg_ref, kg_ref, ones_ref, cos_ref, sin_ref,
                     q_ref, k_ref, vt_ref):
    m = mod_ref[0]
    h = _modulated_norm(x_ref[...], g_ref[...], m[0:1], m[1:2]).astype(BF16)
    cos = cos_ref[...]
    sin = sin_ref[...]
    ones2 = ones_ref[...]

    def normed_rotated(y, gain):
        a, b = y[:, :LANES], y[:, LANES:]
        ss = a * a + b * b
        hi = ss.astype(BF16)
        lo = (ss - hi.astype(F32)).astype(BF16)
        tot = _dot(jnp.concatenate([hi, lo], axis=1), ones2)
        r = lax.rsqrt(tot * (1.0 / GQA_HEAD_DIM) + EPS)
        return _rope(a * r * gain[:, :LANES], b * r * gain[:, LANES:], cos, sin)

    wide = 2 * GQA_QK
    kv0 = _lane_mask(0, LANES // 2)
    for gg in range(GQA_GROUP // 2):
        y = _dot(h, w_ref[:, gg * wide:(gg + 1) * wide])
        for t in range(2):
            g = 2 * gg + t
            o1, o2 = normed_rotated(y[:, t * GQA_QK:(t + 1) * GQA_QK], qg_ref[...])
            for kv, keep in ((0, kv0), (1, jnp.logical_not(kv0))):
                c0 = (kv * GQA_GROUP + g) * GQA_QK
                q_ref[:, c0:c0 + LANES] = jnp.where(keep, o1, 0.0).astype(BF16)
                q_ref[:, c0 + LANES:c0 + GQA_QK] = jnp.where(keep, o2, 0.0).astype(BF16)
    y = _dot(h, w_ref[:, GQA_GROUP * GQA_QK:])
    o1, o2 = normed_rotated(y[:, :GQA_QK], kg_ref[...])
    k_ref[:, :LANES] = o1.astype(BF16)
    k_ref[:, LANES:] = o2.astype(BF16)
    _store_vt(vt_ref, y[:, GQA_QK:], GQA_KV_HEADS)


def _rope_spec(tm):
    per_seq = SEQ // tm
    return pl.BlockSpec(
        (tm, LANES), lambda i: (jnp.where(i < N_LAT // tm, i % per_seq, per_seq), 0))


def _gqa_proj(x, mod, gain, w_qkv, q_gain, k_gain, half_ones, cos, sin):
    kvw = GQA_KV_HEADS * GQA_HEAD_DIM
    qw = GQA_HEADS * GQA_QK
    return pl.pallas_call(
        _gqa_proj_kernel,
        grid=(N_ALL // TP,),
        in_specs=[
            _row_spec(TP, D), _mod_spec(TP), _resident((1, D)),
            _resident((D, D + 2 * kvw)),
            _resident((1, GQA_QK)), _resident((1, GQA_QK)), _resident((2 * LANES, LANES)),
            _rope_spec(TP), _rope_spec(TP),
        ],
        out_specs=[_row_spec(TP, qw), _row_spec(TP, GQA_QK), _col_spec(GQA_KV_HEADS * VT_ROWS, TP)],
        out_shape=[jax.ShapeDtypeStruct((N_ALL, qw), BF16),
                   jax.ShapeDtypeStruct((N_ALL, GQA_QK), BF16),
                   jax.ShapeDtypeStruct((GQA_KV_HEADS * VT_ROWS, N_ALL), BF16)],
        compiler_params=_params(1),
        name="gqa_proj",
    )(x, mod, gain, w_qkv, q_gain, k_gain, half_ones, cos, sin)


def _mla_proj_kernel(x_ref, mod_ref, g_ref, wdq_ref, qg_ref, wuq_ref, wdkv_ref, kvg_ref, wukv_ref,
                     ones_ref, cos_ref, sin_ref, q_ref, k_ref, vt_ref, qn_ref, kn_ref):
    m = mod_ref[0]
    h = _modulated_norm(x_ref[...], g_ref[...], m[0:1], m[1:2]).astype(BF16)
    cos = cos_ref[...]
    sin = sin_ref[...]
    half = MLA_ROPE // 2
    cq = _head_norm(_dot(h, wdq_ref[...]), qg_ref[...]).astype(BF16)
    q = _dot(cq, wuq_ref[...])
    ckv_pe = _dot(h, wdkv_ref[...])
    ckv = _head_norm(ckv_pe[:, :MLA_KV_RANK], kvg_ref[...]).astype(BF16)
    ko1, ko2 = _rope(ckv_pe[:, MLA_KV_RANK:MLA_KV_RANK + LANES], ckv_pe[:, MLA_KV_RANK + LANES:], cos, sin)
    k_pe = jnp.where(_lane_mask(0, 2 * half), ko1, ko2)
    k_pe_sq = k_pe * k_pe
    k_pe = k_pe.astype(BF16)
    kv = _dot(ckv, wukv_ref[...])
    ones2 = ones_ref[...]

    def max_sq_norm(sq_pair):
        rows = _dot(jnp.concatenate(sq_pair, axis=1).astype(BF16), ones2)
        return jnp.max(rows, axis=0, keepdims=True)

    q_norms, k_norms = [], []
    for pp in range(MLA_HEADS // 2):
        p0 = pp * 2 * MLA_QK
        o1, o2 = _rope(q[:, p0 + 2 * MLA_NOPE:p0 + 2 * MLA_NOPE + LANES], q[:, p0 + 2 * MLA_NOPE + LANES:p0 + 2 * MLA_QK],
                       cos, sin)
        q_sq, k_sq = [], []
        for t in range(2):
            j = 2 * pp + t
            c0 = j * MLA_QK
            pe = jnp.where(_lane_mask(t * half, (t + 1) * half), o1,
                           jnp.where(_lane_mask(2 * half + t * half, 2 * half + (t + 1) * half), o2, 0.0))
            q_nope = q[:, p0 + t * MLA_NOPE:p0 + (t + 1) * MLA_NOPE]
            k_nope = kv[:, j * MLA_NOPE:(j + 1) * MLA_NOPE]
            q_ref[:, c0:c0 + MLA_NOPE] = q_nope.astype(BF16)
            q_ref[:, c0 + MLA_NOPE:c0 + MLA_QK] = pe.astype(BF16)
            k_ref[:, c0:c0 + MLA_NOPE] = k_nope.astype(BF16)
            k_ref[:, c0 + MLA_NOPE:c0 + MLA_QK] = k_pe
            q_sq.append(q_nope * q_nope + pe * pe)
            k_sq.append(k_nope * k_nope + k_pe_sq)
        q_norms.append(max_sq_norm(q_sq))
        k_norms.append(max_sq_norm(k_sq))
    qn_ref[0] = jnp.broadcast_to(jnp.concatenate(q_norms, axis=1), qn_ref.shape[1:])
    kn_ref[0] = jnp.broadcast_to(jnp.concatenate(k_norms, axis=1), kn_ref.shape[1:])
    _store_vt(vt_ref, kv[:, MLA_HEADS * MLA_NOPE:], MLA_HEADS)


def _mla_proj(x, mod, gain, w_dq, q_gain, w_uq, w_dkv, kv_gain, w_ukv, pair_ones, cos, sin):
    qkw = MLA_HEADS * MLA_QK
    n_tiles = N_ALL // TP
    norm_spec = pl.BlockSpec((1, SUBLANES, MLA_HEADS * LANES), lambda i: (i, 0, 0))
    norm_shape = jax.ShapeDtypeStruct((n_tiles, SUBLANES, MLA_HEADS * LANES), F32)
    return pl.pallas_call(
        _mla_proj_kernel,
        grid=(N_ALL // TP,),
        in_specs=[
            _row_spec(TP, D), _mod_spec(TP), _resident((1, D)),
            _resident((D, MLA_Q_RANK)), _resident((1, MLA_Q_RANK)), _resident((MLA_Q_RANK, qkw)),
            _resident((D, MLA_KV_RANK + 2 * LANES)), _resident((1, MLA_KV_RANK)),
            _resident((MLA_KV_RANK, MLA_HEADS * (MLA_NOPE + MLA_V))),
            _resident((2 * LANES, 2 * LANES)),
            _rope_spec(TP), _rope_spec(TP),
        ],
        out_specs=[_row_spec(TP, qkw), _row_spec(TP, qkw), _col_spec(MLA_HEADS * VT_ROWS, TP),
                   norm_spec, norm_spec],
        out_shape=[jax.ShapeDtypeStruct((N_ALL, qkw), BF16),
                   jax.ShapeDtypeStruct((N_ALL, qkw), BF16),
                   jax.ShapeDtypeStruct((MLA_HEADS * VT_ROWS, N_ALL), BF16),
                   norm_shape, norm_shape],
        compiler_params=_params(1),
        name="mla_proj",
    )(x, mod, gain, w_dq, q_gain, w_uq, w_dkv, kv_gain, w_ukv, pair_ones, cos, sin)


def _score_bounds(qn, kn):
    qn = qn[:, 0, ::LANES]
    kn = kn[:, 0, ::LANES]
    per_seq = SEQ // TP
    q_lat = jnp.max(qn[:N_LAT // TP].reshape(BATCH, per_seq, -1), axis=1)
    k_lat = jnp.max(kn[:N_LAT // TP].reshape(BATCH, per_seq, -1), axis=1)
    ctx_tile = N_LAT // TP + (jnp.arange(BATCH) * CTX_LEN) // TP
    k_all = jnp.maximum(k_lat, kn[ctx_tile])
    return 1.02 * jnp.sqrt(jnp.max(q_lat * k_all, axis=1))


def _nt_dot(a, b):
    return lax.dot_general(a, b, (((1,), (1,)), ((), ())), preferred_element_type=F32)


def _attn_scores(q, k_parts):
    return [_nt_dot(k, q) for k in k_parts]


def _attn_kernel(bound_ref, q_ref, qc_ref, kl_ref, kc_ref, vtl_ref, vtc_ref, o_ref, oc_ref,
                 *, heads, group, dq, dv, shared_k):
    n = heads * group

    def k_cols(qh):
        kv = 0 if shared_k else qh // group
        return slice(kv * dq, (kv + 1) * dq)

    def v_rows(qh):
        return slice(qh // group * VT_ROWS, (qh // group + 1) * VT_ROWS)

    def run(queries, out_ref, k_refs, vt_refs, shift=None):
        def scores(qh):
            return _attn_scores(queries[:, qh * dq:(qh + 1) * dq], [k[:, k_cols(qh)] for k in k_refs])

        ahead = 2 if shift is None else 1
        pending = [scores(qh) for qh in range(min(ahead, n))]
        for qh in range(n):
            cur = pending.pop(0)
            if shift is None:
                mx = functools.reduce(jnp.maximum, [jnp.max(s, axis=0, keepdims=True) for s in cur])
            else:
                mx = shift
            probs = [jnp.exp2(s - mx) for s in cur]
            den = sum(jnp.sum(p, axis=0, keepdims=True) for p in probs)
            probs = [p.astype(BF16) for p in probs]
            if qh + ahead < n:
                pending.append(scores(qh + ahead))
            acc = sum(_dot(vt[v_rows(qh), :], p) for vt, p in zip(vt_refs, probs))
            out_ref[:, qh * dv:(qh + 1) * dv] = (acc / den).T.astype(BF16)

    lat = (q_ref, o_ref, [kl_ref, kc_ref], [vtl_ref, vtc_ref])
    bound = bound_ref[0, pl.program_id(0)]

    @pl.when(bound <= MAX_BOUND_SHIFT)
    def _():
        run(*lat, shift=bound)

    @pl.when(jnp.logical_not(bound <= MAX_BOUND_SHIFT))
    def _():
        run(*lat)

    @pl.when(pl.program_id(2) == 0)
    def _():
        run(qc_ref, oc_ref, [kc_ref], [vtc_ref])


def _attention(q, k, vt, score_bounds, ffn_w, ffn_layer, *, kv_heads, group, dq, dv, heads_per_step, shared_k):
    hp = heads_per_step
    lat_tiles = SEQ // TQ
    ctx_block0 = N_LAT // CTX_LEN
    kw = dq if shared_k else hp * dq
    k_col = (lambda h: 0) if shared_k else (lambda h: h)
    width = kv_heads * group * dv
    h_steps = kv_heads // hp
    n_steps = BATCH * h_steps * lat_tiles
    assert n_steps >= CAST_CHUNKS
    r_in, r_out, r_shape = _ffn_cast_riders(
        ffn_w, ffn_layer, lambda b, h, t: ((b * h_steps + h) * lat_tiles + t) * CAST_CHUNKS // n_steps)
    body = functools.partial(_attn_kernel, heads=hp, group=group, dq=dq, dv=dv, shared_k=shared_k)
    return pl.pallas_call(
        _with_cast_riders(body, 7, 2),
        grid=(BATCH, h_steps, lat_tiles),
        in_specs=[
            pl.BlockSpec(memory_space=pltpu.SMEM),
            pl.BlockSpec((TQ, hp * group * dq), lambda b, h, t: (b * lat_tiles + t, h)),
            pl.BlockSpec((CTX_LEN, hp * group * dq), lambda b, h, t: (ctx_block0 + b, h)),
            pl.BlockSpec((SEQ, kw), lambda b, h, t: (b, k_col(h))),
            pl.BlockSpec((CTX_LEN, kw), lambda b, h, t: (ctx_block0 + b, k_col(h))),
            pl.BlockSpec((hp * VT_ROWS, SEQ), lambda b, h, t: (h, b)),
            pl.BlockSpec((hp * VT_ROWS, CTX_LEN), lambda b, h, t: (h, ctx_block0 + b)),
        ] + r_in,
        out_specs=[pl.BlockSpec((TQ, hp * group * dv), lambda b, h, t: (b * lat_tiles + t, h)),
                   pl.BlockSpec((CTX_LEN, hp * group * dv), lambda b, h, t: (b, h))] + r_out,
        out_shape=[jax.ShapeDtypeStruct((N_LAT, width), BF16), jax.ShapeDtypeStruct((N_CTX, width), BF16)] + r_shape,
        compiler_params=_params(3),
        name="attention",
    )(score_bounds.reshape(1, BATCH).astype(F32), q, q, k, k, vt, vt, *ffn_w)


def _ffn_tail(x1, m, g2_ref, w1_ref, w3_ref, w2_ref):
    h = _modulated_norm(x1, g2_ref[...], m[3:4], m[4:5]).astype(BF16)
    a = _dot(h, _w(w1_ref))
    b = _dot(h, _w(w3_ref))
    gated = (a * jax.nn.sigmoid(a) * b).astype(BF16)
    return x1 + m[5:6] * _dot(gated, _w(w2_ref))


def _attn_tail_kernel(al_ref, ac_ref, wo_ref, x_ref, mod_ref, g2_ref, w1_ref, w3_ref, w2_ref, o_ref):
    m = mod_ref[0]
    a = jnp.where(pl.program_id(0) < N_LAT // TM, al_ref[...], ac_ref[...])
    x1 = x_ref[...] + m[2:3] * _dot(a, _w(wo_ref))
    o_ref[...] = _ffn_tail(x1, m, g2_ref, w1_ref, w3_ref, w2_ref)


def _conv_tail_kernel(b_ref, u_ref, up_ref, un_ref, cw_ref, wo_ref, x_ref, mod_ref,
                      g2_ref, w1_ref, w3_ref, w2_ref, fg_ref, o_ref, *, final):
    m = mod_ref[0]
    x1 = x_ref[...] + m[2:3] * _conv_mix(b_ref, u_ref, up_ref, un_ref, cw_ref, wo_ref)
    x2 = _ffn_tail(x1, m, g2_ref, w1_ref, w3_ref, w2_ref)
    o_ref[...] = _head_norm(x2, fg_ref[...]) if final else x2


def _ffn_specs():
    return [_mod_spec(TM), _resident((1, D)), _resident((D, FFN_HIDDEN)),
            _resident((D, FFN_HIDDEN)), _resident((FFN_HIDDEN, D))]


def _attn_tail(a_lat, a_ctx, w_o, mixer_layer, x, mod, g2, w1, w3, w2):
    lat_tiles = N_LAT // TM
    return pl.pallas_call(
        _attn_tail_kernel,
        grid=(N_ALL // TM,),
        in_specs=[
            pl.BlockSpec((TM, D), lambda i: (jnp.minimum(i, lat_tiles - 1), 0)),
            pl.BlockSpec((TM, D), lambda i: (jnp.maximum(i - lat_tiles, 0), 0)),
            _resident((D, D), mixer_layer), _row_spec(TM, D),
        ] + _ffn_specs(),
        out_specs=_row_spec(TM, D),
        out_shape=jax.ShapeDtypeStruct((N_ALL, D), F32),
        compiler_params=_params(1),
        name="attn_tail",
    )(a_lat, a_ctx, w_o, x, mod, g2, w1, w3, w2)


def _conv_tail(b, u, conv_w, w_out, mixer_layer, x, mod, g2, w1, w3, w2, final_gain, n_rows, final):
    per = TM // HALO
    n_halo = n_rows // HALO
    return pl.pallas_call(
        functools.partial(_conv_tail_kernel, final=final),
        grid=(n_rows // TM,),
        in_specs=[
            _row_spec(TM, D), _row_spec(TM, D),
            pl.BlockSpec((HALO, D), lambda i: (jnp.maximum(i * per - 1, 0), 0)),
            pl.BlockSpec((HALO, D), lambda i: (jnp.minimum((i + 1) * per, n_halo - 1), 0)),
            _resident((3, D)), _resident((D, D), mixer_layer), _row_spec(TM, D),
        ] + _ffn_specs() + [_resident((1, D))],
        out_specs=_row_spec(TM, D),
        out_shape=jax.ShapeDtypeStruct((n_rows, D), F32),
        compiler_params=_params(1),
        name="conv_tail",
    )(b, u, u, u, conv_w, w_out, x, mod, g2, w1, w3, w2, final_gain)


def _axial_angles(rot_dim):
    n = rot_dim // 4
    rows = jnp.repeat(jnp.arange(SEQ // GRID_W, dtype=F32), GRID_W)
    cols = jnp.tile(jnp.arange(GRID_W, dtype=F32), SEQ // GRID_W)
    freqs = ROPE_THETA ** (-jnp.arange(n, dtype=F32) / n)
    return jnp.concatenate([rows[:, None] * freqs, cols[:, None] * freqs], axis=-1)


def _rope_tables(rot_dim):
    ang = _axial_angles(rot_dim)
    copies = LANES // (rot_dim // 2)
    cos_t = jnp.concatenate([jnp.tile(jnp.cos(ang), (1, copies)), jnp.ones((TP, LANES), F32)], axis=0)
    sin_t = jnp.concatenate([jnp.tile(jnp.sin(ang), (1, copies)), jnp.zeros((TP, LANES), F32)], axis=0)
    return cos_t, sin_t


def _gqa_slab_pairs(w, n_kv, n_group):
    half = GQA_HEAD_DIM // 2
    w = w.reshape(w.shape[0], n_kv, n_group, 2, half)
    return jnp.transpose(w, (0, 2, 3, 1, 4)).reshape(w.shape[0], n_group * GQA_QK)


def _mla_weights(w_dq, w_uq, w_dkv, w_ukv):
    half = MLA_ROPE // 2
    qk = MLA_NOPE + MLA_ROPE
    uq = w_uq.reshape(MLA_Q_RANK, MLA_HEADS // 2, 2, qk)
    nope = uq[..., :MLA_NOPE].reshape(MLA_Q_RANK, MLA_HEADS // 2, 2 * MLA_NOPE)
    x1 = uq[..., MLA_NOPE:MLA_NOPE + half].reshape(MLA_Q_RANK, MLA_HEADS // 2, 2 * half)
    x2 = uq[..., MLA_NOPE + half:].reshape(MLA_Q_RANK, MLA_HEADS // 2, 2 * half)
    uq = jnp.concatenate([nope, x1, x1, x2, x2], axis=-1).reshape(MLA_Q_RANK, MLA_HEADS * MLA_QK)
    k1 = w_dkv[:, MLA_KV_RANK:MLA_KV_RANK + half]
    k2 = w_dkv[:, MLA_KV_RANK + half:]
    dkv = jnp.concatenate([w_dkv[:, :MLA_KV_RANK]] + [k1] * 4 + [k2] * 4, axis=-1)
    ukv = w_ukv.reshape(MLA_KV_RANK, MLA_HEADS, MLA_NOPE + MLA_V)
    ukv = jnp.concatenate([ukv[:, :, :MLA_NOPE].reshape(MLA_KV_RANK, -1),
                           ukv[:, :, MLA_NOPE:].reshape(MLA_KV_RANK, -1)], axis=-1)
    return w_dq.astype(BF16), uq.astype(BF16), dkv.astype(BF16), ukv.astype(BF16)


def kernel(x, c, ctx, c_ctx, ada_w, ada_b, norm1_g, norm2_g, ffn_w1, ffn_w3, ffn_w2, conv_w_in, conv_w, conv_w_out, gqa_wq, gqa_wk, gqa_wv, gqa_q_norm, gqa_k_norm, gqa_wo, mla_w_dq, mla_q_norm, mla_w_uq, mla_w_dkv, mla_kv_norm, mla_w_ukv, mla_wo, final_g):
    assert x.shape == (BATCH, SEQ, D) and ctx.shape == (BATCH, CTX_LEN, D)
    cond = jnp.concatenate(
        [c, c_ctx[None], jnp.zeros((MOD_ROWS - BATCH - 1, D), F32)], axis=0)
    mods = _modulation_tables(cond, ada_w, ada_b).reshape(DEPTH, MOD_ROWS, N_MOD, D)

    gqa_cos, gqa_sin = _rope_tables(GQA_HEAD_DIM)
    lane_half = jnp.arange(LANES) // (LANES // 2)
    half_ones = jnp.tile((lane_half[:, None] == lane_half[None, :]).astype(BF16), (2, 1))
    lane_head = jnp.arange(2 * LANES) // LANES
    pair_ones = (lane_head[:, None] == lane_head[None, :]).astype(BF16)
    mla_cos, mla_sin = _rope_tables(MLA_ROPE)

    ffn_w = (ffn_w1, ffn_w3, ffn_w2)
    conv_w_in, conv_w_out = conv_w_in.astype(BF16), conv_w_out.astype(BF16)
    gqa_wo = gqa_wo.astype(BF16)
    mla_wo = mla_wo.astype(BF16)

    xs = None
    for i in range(DEPTH):
        kind, j = i % 3, i // 3
        mod = mods[i]
        last = i == DEPTH - 1
        n_rows = N_LAT if last else N_ALL
        g1 = norm1_g[i][None]
        g2 = norm2_g[i][None]
        if kind == 0:
            if i == 0:
                b, u, xs, *ffn = _conv_in_first(x.reshape(N_LAT, D), ctx.reshape(N_CTX, D), mod, g1,
                                                conv_w_in, j, ffn_w, i)
            else:
                b, u, *ffn = _conv_in(xs, mod, g1, conv_w_in, j, n_rows, ffn_w, i)
            xs = _conv_tail(b, u, conv_w[j], conv_w_out, j, xs, mod, g2, *ffn, final_g[None], n_rows, last)
        elif kind == 1:
            w_qkv = jnp.concatenate([_gqa_slab_pairs(gqa_wq[j].astype(BF16), GQA_KV_HEADS, GQA_GROUP),
                                     _gqa_slab_pairs(gqa_wk[j].astype(BF16), GQA_KV_HEADS, 1),
                                     gqa_wv[j].astype(BF16)], axis=-1)
            q_gain = _gqa_slab_pairs(jnp.tile(gqa_q_norm[j], GQA_KV_HEADS)[None], GQA_KV_HEADS, 1)
            k_gain = _gqa_slab_pairs(jnp.tile(gqa_k_norm[j], GQA_KV_HEADS)[None], GQA_KV_HEADS, 1)
            q_gain = q_gain * (GQA_HEAD_DIM ** -0.5 * LOG2E)
            bound = 1.02 * GQA_HEAD_DIM * jnp.max(jnp.abs(q_gain)) * jnp.max(jnp.abs(k_gain))
            q, k, vt = _gqa_proj(xs, mod, g1, w_qkv, q_gain, k_gain, half_ones, gqa_cos, gqa_sin)
            a_lat, a_ctx, *ffn = _attention(q, k, vt, jnp.full((BATCH,), bound), ffn_w, i,
                                            kv_heads=GQA_KV_HEADS, group=GQA_GROUP, dq=GQA_QK, dv=GQA_HEAD_DIM,
                                            heads_per_step=1, shared_k=True)
            xs = _attn_tail(a_lat, a_ctx, gqa_wo, j, xs, mod, g2, *ffn)
        else:
            w_dq, w_uq, w_dkv, w_ukv = _mla_weights(mla_w_dq[j], mla_w_uq[j], mla_w_dkv[j], mla_w_ukv[j])
            q_gain = mla_q_norm[j][None] * ((MLA_NOPE + MLA_ROPE) ** -0.5 * LOG2E)
            q, k, vt, qn, kn = _mla_proj(xs, mod, g1, w_dq, q_gain, w_uq, w_dkv,
                                         mla_kv_norm[j][None], w_ukv, pair_ones, mla_cos, mla_sin)
            a_lat, a_ctx, *ffn = _attention(q, k, vt, _score_bounds(qn, kn), ffn_w, i,
                                            kv_heads=MLA_HEADS, group=1, dq=MLA_QK, dv=MLA_V,
                                            heads_per_step=4, shared_k=False)
            xs = _attn_tail(a_lat, a_ctx, mla_wo, j, xs, mod, g2, *ffn)

    assert (DEPTH - 1) % 3 == 0
    return xs.reshape(BATCH, SEQ, D)
```

```python
import functools

import jax
import jax.numpy as jnp
import numpy as np
from jax import lax
from jax.experimental import pallas as pl
from jax.experimental.pallas import tpu as pltpu

D = 1024
BATCH = 8
SEQ = 2048
DEPTH = 4
GRID_W = 64
CTX_LEN = 256
N_MOD = 6
FFN_HIDDEN = 2816
ROPE_THETA = 10000.0
EPS = 1e-6
GQA_HEAD_DIM = 128
GQA_HEADS = 8
GQA_KV_HEADS = 2
GQA_GROUP = 4
MLA_HEADS = 8
MLA_NOPE = 128
MLA_ROPE = 64
MLA_V = 128
MLA_KV_RANK = 256
MLA_Q_RANK = 768
MLA_QK = 256
GQA_QK = 256
LANES = 128
SUBLANES = 8
VT_ROWS = 128

N_LAT = BATCH * SEQ
N_CTX = BATCH * CTX_LEN
N_ALL = N_LAT + N_CTX
MOD_ROWS = 16

TM = 512
TP = 1024
TQ = 1024
HALO = 8
CONV_COLS = 256
CAST_CHUNKS = 16
VMEM_LIMIT = 56 * 1024 * 1024
LOG2E = 1.4426950408889634
MAX_BOUND_SHIFT = 50.0

F32 = jnp.float32
BF16 = jnp.bfloat16


def _params(n_axes):
    return pltpu.CompilerParams(
        dimension_semantics=("arbitrary",) * n_axes, vmem_limit_bytes=VMEM_LIMIT)


def _resident(shape, layer=None):
    nd = len(shape)
    if layer is None:
        return pl.BlockSpec(shape, lambda *_: (0,) * nd, pipeline_mode=pl.Buffered(1))
    return pl.BlockSpec((1,) + tuple(shape), lambda *_: (layer,) + (0,) * nd, pipeline_mode=pl.Buffered(1))


def _w(ref):
    return ref[0] if len(ref.shape) == 3 else ref[...]


def _ffn_cast_riders(ffn_w, layer, chunk_of):
    in_specs, out_specs, out_shapes = [], [], []
    for w in ffn_w:
        _, rows, cols = w.shape
        r = rows // CAST_CHUNKS
        in_specs.append(pl.BlockSpec((1, r, cols), lambda *g: (layer, chunk_of(*g), 0)))
        out_specs.append(pl.BlockSpec((r, cols), lambda *g: (chunk_of(*g), 0)))
        out_shapes.append(jax.ShapeDtypeStruct((rows, cols), BF16))
    return in_specs, out_specs, out_shapes


def _with_cast_riders(body, n_in, n_out):
    def kern(*refs):
        n_riders = (len(refs) - n_in - n_out) // 2
        ins, rider_in = refs[:n_in], refs[n_in:n_in + n_riders]
        outs = refs[n_in + n_riders:n_in + n_riders + n_out]
        rider_out = refs[n_in + n_riders + n_out:]
        body(*ins, *outs)
        for src, dst in zip(rider_in, rider_out):
            dst[...] = src[0].astype(dst.dtype)
    return kern


def _mod_spec(tm):
    return pl.BlockSpec((1, N_MOD, D), lambda i: (jnp.minimum(i * tm // SEQ, BATCH), 0, 0))


def _row_spec(tm, width):
    return pl.BlockSpec((tm, width), lambda i: (i, 0))


def _col_spec(height, tm):
    return pl.BlockSpec((height, tm), lambda i: (0, i))


def _dot(a, b):
    return jnp.dot(a, b, preferred_element_type=F32)


def _modulated_norm(x, gain, shift, scale):
    r = lax.rsqrt(jnp.mean(x * x, axis=-1, keepdims=True) + EPS)
    return (x * r) * (gain * (1.0 + scale)) + shift


def _head_norm(x, gain):
    r = lax.rsqrt(jnp.mean(x * x, axis=-1, keepdims=True) + EPS)
    return (x * r) * gain


def _store_vt(vt_ref, v, n_heads):
    for j in range(n_heads):
        vt_ref[j * VT_ROWS:(j + 1) * VT_ROWS, :] = v[:, j * VT_ROWS:(j + 1) * VT_ROWS].T.astype(BF16)


def _rope(a, b, cos, sin):
    return a * cos - b * sin, a * sin + b * cos


def _lane_mask(lo, hi):
    lane = lax.broadcasted_iota(jnp.int32, (1, LANES), 1)
    return (lane >= lo) & (lane < hi)


def _mod_kernel(cond_ref, w_ref, b_ref, o_ref):
    c = cond_ref[...]
    act = (c * jax.nn.sigmoid(c)).astype(BF16)
    o_ref[0] = _dot(act, w_ref[0].astype(BF16)) + b_ref[0]


def _modulation_tables(cond, ada_w, ada_b):
    tn = 1536
    return pl.pallas_call(
        _mod_kernel,
        grid=(DEPTH, N_MOD * D // tn),
        in_specs=[
            pl.BlockSpec((MOD_ROWS, D), lambda l, j: (0, 0)),
            pl.BlockSpec((1, D, tn), lambda l, j: (l, 0, j)),
            pl.BlockSpec((1, 1, tn), lambda l, j: (l, 0, j)),
        ],
        out_specs=pl.BlockSpec((1, MOD_ROWS, tn), lambda l, j: (l, 0, j)),
        out_shape=jax.ShapeDtypeStruct((DEPTH, MOD_ROWS, N_MOD * D), F32),
        compiler_params=_params(2),
        name="modulation",
    )(cond, ada_w, ada_b.reshape(DEPTH, 1, N_MOD * D))


def _conv_in_compute(x, mod_ref, g_ref, w_ref, b_ref, u_ref):
    m = mod_ref[0]
    h = _modulated_norm(x, g_ref[...], m[0:1], m[1:2]).astype(BF16)
    y = _dot(h, _w(w_ref))
    b_ref[...] = y[:, :D].astype(BF16)
    u_ref[...] = y[:, D:2 * D] * y[:, 2 * D:]


def _conv_in_kernel(x_ref, mod_ref, g_ref, w_ref, b_ref, u_ref):
    _conv_in_compute(x_ref[...], mod_ref, g_ref, w_ref, b_ref, u_ref)


def _conv_in_first_kernel(lat_ref, ctx_ref, mod_ref, g_ref, w_ref, b_ref, u_ref, xs_ref):
    x = jnp.where(pl.program_id(0) < N_LAT // TP, lat_ref[...], ctx_ref[...])
    xs_ref[...] = x
    _conv_in_compute(x, mod_ref, g_ref, w_ref, b_ref, u_ref)


def _conv_in(x, mod, gain, w_in, layer, n_rows, ffn_w, ffn_layer):
    r_in, r_out, r_shape = _ffn_cast_riders(ffn_w, ffn_layer, lambda i: jnp.minimum(i, CAST_CHUNKS - 1))
    assert n_rows // TP >= CAST_CHUNKS
    return pl.pallas_call(
        _with_cast_riders(_conv_in_kernel, 4, 2),
        grid=(n_rows // TP,),
        in_specs=[_row_spec(TP, D), _mod_spec(TP), _resident((1, D)), _resident((D, 3 * D), layer)] + r_in,
        out_specs=[_row_spec(TP, D), _row_spec(TP, D)] + r_out,
        out_shape=[jax.ShapeDtypeStruct((n_rows, D), BF16), jax.ShapeDtypeStruct((n_rows, D), F32)] + r_shape,
        compiler_params=_params(1),
        name="conv_in",
    )(x, mod, gain, w_in, *ffn_w)


def _conv_in_first(lat, ctx, mod, gain, w_in, layer, ffn_w, ffn_layer):
    lat_tiles = N_LAT // TP
    r_in, r_out, r_shape = _ffn_cast_riders(ffn_w, ffn_layer, lambda i: jnp.minimum(i, CAST_CHUNKS - 1))
    assert N_ALL // TP >= CAST_CHUNKS
    return pl.pallas_call(
        _with_cast_riders(_conv_in_first_kernel, 5, 3),
        grid=(N_ALL // TP,),
        in_specs=[
            pl.BlockSpec((TP, D), lambda i: (jnp.minimum(i, lat_tiles - 1), 0)),
            pl.BlockSpec((TP, D), lambda i: (jnp.maximum(i - lat_tiles, 0), 0)),
            _mod_spec(TP), _resident((1, D)), _resident((D, 3 * D), layer),
        ] + r_in,
        out_specs=[_row_spec(TP, D), _row_spec(TP, D), _row_spec(TP, D)] + r_out,
        out_shape=[jax.ShapeDtypeStruct((N_ALL, D), BF16), jax.ShapeDtypeStruct((N_ALL, D), F32),
                   jax.ShapeDtypeStruct((N_ALL, D), F32)] + r_shape,
        compiler_params=_params(1),
        name="conv_in_first",
    )(lat, ctx, mod, gain, w_in, *ffn_w)


def _conv_gate(b_ref, u_ref, up_ref, un_ref, cw_ref, cols):
    tm = u_ref.shape[0]
    u = u_ref[:, cols]
    local = lax.broadcasted_iota(jnp.int32, (tm, 1), 0)
    row = local + pl.program_id(0) * tm
    in_ctx = row >= N_LAT
    first = ((row & (CTX_LEN - 1)) == 0) & (in_ctx | ((row & (SEQ - 1)) == 0))
    last = ((row & (CTX_LEN - 1)) == CTX_LEN - 1) & (in_ctx | ((row & (SEQ - 1)) == SEQ - 1))
    prev = jnp.where(local == 0, up_ref[HALO - 1:HALO, cols], pltpu.roll(u, 1, axis=0))
    prev = jnp.where(first, 0.0, prev)
    nxt = jnp.where(local == tm - 1, un_ref[0:1, cols], pltpu.roll(u, tm - 1, axis=0))
    nxt = jnp.where(last, 0.0, nxt)
    cw = cw_ref[:, cols]
    z = prev * cw[0:1] + u * cw[1:2] + nxt * cw[2:3]
    return (b_ref[:, cols].astype(F32) * z).astype(BF16)


def _conv_mix(b_ref, u_ref, up_ref, un_ref, cw_ref, wo_ref):
    wo = wo_ref.at[0] if len(wo_ref.shape) == 3 else wo_ref
    acc = None
    for c0 in range(0, D, CONV_COLS):
        cols = slice(c0, c0 + CONV_COLS)
        part = _dot(_conv_gate(b_ref, u_ref, up_ref, un_ref, cw_ref, cols), wo[cols, :])
        acc = part if acc is None else acc + part
    return acc


def _gqa_proj_kernel(x_ref, mod_ref, g_ref, w_ref, qg_ref, kg_ref, ones_ref, cos_ref, sin_ref,
                     q_ref, k_ref, vt_ref):
    m = mod_ref[0]
    h = _modulated_norm(x_ref[...], g_ref[...], m[0:1], m[1:2]).astype(BF16)
    cos = cos_ref[...]
    sin = sin_ref[...]
    ones2 = ones_ref[...]

    def normed_rotated(y, gain):
        a, b = y[:, :LANES], y[:, LANES:]
        ss = a * a + b * b
        hi = ss.astype(BF16)
        lo = (ss - hi.astype(F32)).astype(BF16)
        tot = _dot(jnp.concatenate([hi, lo], axis=1), ones2)
        r = lax.rsqrt(tot * (1.0 / GQA_HEAD_DIM) + EPS)
        return _rope(a * r * gain[:, :LANES], b * r * gain[:, LANES:], cos, sin)

    wide = 2 * GQA_QK
    kv0 = _lane_mask(0, LANES // 2)
    for gg in range(GQA_GROUP // 2):
        y = _dot(h, w_ref[:, gg * wide:(gg + 1) * wide])
        for t in range(2):
            g = 2 * gg + t
            o1, o2 = normed_rotated(y[:, t * GQA_QK:(t + 1) * GQA_QK], qg_ref[...])
            for kv, keep in ((0, kv0), (1, jnp.logical_not(kv0))):
                c0 = (kv * GQA_GROUP + g) * GQA_QK
                q_ref[:, c0:c0 + LANES] = jnp.where(keep, o1, 0.0).astype(BF16)
                q_ref[:, c0 + LANES:c0 + GQA_QK] = jnp.where(keep, o2, 0.0).astype(BF16)
    y = _dot(h, w_ref[:, GQA_GROUP * GQA_QK:])
    o1, o2 = normed_rotated(y[:, :GQA_QK], kg_ref[...])
    k_ref[:, :LANES] = o1.astype(BF16)
    k_ref[:, LANES:] = o2.astype(BF16)
    _store_vt(vt_ref, y[:, GQA_QK:], GQA_KV_HEADS)


def _rope_spec(tm):
    per_seq = SEQ // tm
    return pl.BlockSpec(
        (tm, LANES), lambda i: (jnp.where(i < N_LAT // tm, i % per_seq, per_seq), 0))


def _gqa_proj(x, mod, gain, w_qkv, q_gain, k_gain, half_ones, cos, sin):
    kvw = GQA_KV_HEADS * GQA_HEAD_DIM
    qw = GQA_HEADS * GQA_QK
    return pl.pallas_call(
        _gqa_proj_kernel,
        grid=(N_ALL // TP,),
        in_specs=[
            _row_spec(TP, D), _mod_spec(TP), _resident((1, D)),
            _resident((D, D + 2 * kvw)),
            _resident((1, GQA_QK)), _resident((1, GQA_QK)), _resident((2 * LANES, LANES)),
            _rope_spec(TP), _rope_spec(TP),
        ],
        out_specs=[_row_spec(TP, qw), _row_spec(TP, GQA_QK), _col_spec(GQA_KV_HEADS * VT_ROWS, TP)],
        out_shape=[jax.ShapeDtypeStruct((N_ALL, qw), BF16),
                   jax.ShapeDtypeStruct((N_ALL, GQA_QK), BF16),
                   jax.ShapeDtypeStruct((GQA_KV_HEADS * VT_ROWS, N_ALL), BF16)],
        compiler_params=_params(1),
        name="gqa_proj",
    )(x, mod, gain, w_qkv, q_gain, k_gain, half_ones, cos, sin)


def _mla_proj_kernel(x_ref, mod_ref, g_ref, wdq_ref, qg_ref, wuq_ref, wdkv_ref, kvg_ref, wukv_ref,
                     ones_ref, cos_ref, sin_ref, q_ref, k_ref, vt_ref, qn_ref, kn_ref):
    m = mod_ref[0]
    h = _modulated_norm(x_ref[...], g_ref[...], m[0:1], m[1:2]).astype(BF16)
    cos = cos_ref[...]
    sin = sin_ref[...]
    half = MLA_ROPE // 2
    cq = _head_norm(_dot(h, wdq_ref[...]), qg_ref[...]).astype(BF16)
    q = _dot(cq, wuq_ref[...])
    ckv_pe = _dot(h, wdkv_ref[...])
    ckv = _head_norm(ckv_pe[:, :MLA_KV_RANK], kvg_ref[...]).astype(BF16)
    ko1, ko2 = _rope(ckv_pe[:, MLA_KV_RANK:MLA_KV_RANK + LANES], ckv_pe[:, MLA_KV_RANK + LANES:], cos, sin)
    k_pe = jnp.where(_lane_mask(0, 2 * half), ko1, ko2)
    k_pe_sq = k_pe * k_pe
    k_pe = k_pe.astype(BF16)
    kv = _dot(ckv, wukv_ref[...])
    ones2 = ones_ref[...]

    def max_sq_norm(sq_pair):
        sums = _nt_dot(ones2, jnp.concatenate(sq_pair, axis=1).astype(BF16))
        top = jnp.max(sums, axis=1, keepdims=True)
        lane = lax.broadcasted_iota(jnp.int32, (1, 2 * LANES), 1)
        return jnp.where(lane < LANES, top[0:1], top[1:2])

    q_norms, k_norms = [], []
    for pp in range(MLA_HEADS // 2):
        p0 = pp * 2 * MLA_QK
        o1, o2 = _rope(q[:, p0 + 2 * MLA_NOPE:p0 + 2 * MLA_NOPE + LANES], q[:, p0 + 2 * MLA_NOPE + LANES:p0 + 2 * MLA_QK],
                       cos, sin)
        q_sq, k_sq = [], []
        for t in range(2):
            j = 2 * pp + t
            c0 = j * MLA_QK
            pe = jnp.where(_lane_mask(t * half, (t + 1) * half), o1,
                           jnp.where(_lane_mask(2 * half + t * half, 2 * half + (t + 1) * half), o2, 0.0))
            q_nope = q[:, p0 + t * MLA_NOPE:p0 + (t + 1) * MLA_NOPE]
            k_nope = kv[:, j * MLA_NOPE:(j + 1) * MLA_NOPE]
            q_ref[:, c0:c0 + MLA_NOPE] = q_nope.astype(BF16)
            q_ref[:, c0 + MLA_NOPE:c0 + MLA_QK] = pe.astype(BF16)
            k_ref[:, c0:c0 + MLA_NOPE] = k_nope.astype(BF16)
            k_ref[:, c0 + MLA_NOPE:c0 + MLA_QK] = k_pe
            q_sq.append(q_nope * q_nope + pe * pe)
            k_sq.append(k_nope * k_nope + k_pe_sq)
        q_norms.append(max_sq_norm(q_sq))
        k_norms.append(max_sq_norm(k_sq))
    qn_ref[0] = jnp.broadcast_to(jnp.concatenate(q_norms, axis=1), qn_ref.shape[1:])
    kn_ref[0] = jnp.broadcast_to(jnp.concatenate(k_norms, axis=1), kn_ref.shape[1:])
    _store_vt(vt_ref, kv[:, MLA_HEADS * MLA_NOPE:], MLA_HEADS)


def _mla_proj(x, mod, gain, w_dq, q_gain, w_uq, w_dkv, kv_gain, w_ukv, pair_ones, cos, sin):
    qkw = MLA_HEADS * MLA_QK
    n_tiles = N_ALL // TP
    norm_spec = pl.BlockSpec((1, SUBLANES, MLA_HEADS * LANES), lambda i: (i, 0, 0))
    norm_shape = jax.ShapeDtypeStruct((n_tiles, SUBLANES, MLA_HEADS * LANES), F32)
    return pl.pallas_call(
        _mla_proj_kernel,
        grid=(N_ALL // TP,),
        in_specs=[
            _row_spec(TP, D), _mod_spec(TP), _resident((1, D)),
            _resident((D, MLA_Q_RANK)), _resident((1, MLA_Q_RANK)), _resident((MLA_Q_RANK, qkw)),
            _resident((D, MLA_KV_RANK + 2 * LANES)), _resident((1, MLA_KV_RANK)),
            _resident((MLA_KV_RANK, MLA_HEADS * (MLA_NOPE + MLA_V))),
            _resident((SUBLANES, 2 * LANES)),
            _rope_spec(TP), _rope_spec(TP),
        ],
        out_specs=[_row_spec(TP, qkw), _row_spec(TP, qkw), _col_spec(MLA_HEADS * VT_ROWS, TP),
                   norm_spec, norm_spec],
        out_shape=[jax.ShapeDtypeStruct((N_ALL, qkw), BF16),
                   jax.ShapeDtypeStruct((N_ALL, qkw), BF16),
                   jax.ShapeDtypeStruct((MLA_HEADS * VT_ROWS, N_ALL), BF16),
                   norm_shape, norm_shape],
        compiler_params=_params(1),
        name="mla_proj",
    )(x, mod, gain, w_dq, q_gain, w_uq, w_dkv, kv_gain, w_ukv, pair_ones, cos, sin)


def _score_bounds(qn, kn):
    qn = qn[:, 0, ::LANES]
    kn = kn[:, 0, ::LANES]
    per_seq = SEQ // TP
    q_lat = jnp.max(qn[:N_LAT // TP].reshape(BATCH, per_seq, -1), axis=1)
    k_lat = jnp.max(kn[:N_LAT // TP].reshape(BATCH, per_seq, -1), axis=1)
    ctx_tile = N_LAT // TP + (jnp.arange(BATCH) * CTX_LEN) // TP
    k_all = jnp.maximum(k_lat, kn[ctx_tile])
    return 1.02 * jnp.sqrt(jnp.max(q_lat * k_all, axis=1))


def _nt_dot(a, b):
    return lax.dot_general(a, b, (((1,), (1,)), ((), ())), preferred_element_type=F32)


def _attn_scores(q, k_parts):
    return [_nt_dot(k, q) for k in k_parts]


def _attn_kernel(bound_ref, q_ref, qc_ref, kl_ref, kc_ref, vtl_ref, vtc_ref, o_ref, oc_ref,
                 *, heads, group, dq, dv, shared_k):
    n = heads * group

    def k_cols(qh):
        kv = 0 if shared_k else qh // group
        return slice(kv * dq, (kv + 1) * dq)

    def v_rows(qh):
        return slice(qh // group * VT_ROWS, (qh // group + 1) * VT_ROWS)

    def run(queries, out_ref, k_refs, vt_refs, shift=None):
        def scores(qh):
            return _attn_scores(queries[:, qh * dq:(qh + 1) * dq], [k[:, k_cols(qh)] for k in k_refs])

        ahead = 2 if shift is None else 1
        pending = [scores(qh) for qh in range(min(ahead, n))]
        for qh in range(n):
            cur = pending.pop(0)
            if shift is None:
                mx = functools.reduce(jnp.maximum, [jnp.max(s, axis=0, keepdims=True) for s in cur])
            else:
                mx = shift
            probs = [jnp.exp2(s - mx) for s in cur]
            den = sum(jnp.sum(p, axis=0, keepdims=True) for p in probs)
            probs = [p.astype(BF16) for p in probs]
            if qh + ahead < n:
                pending.append(scores(qh + ahead))
            acc = sum(_dot(vt[v_rows(qh), :], p) for vt, p in zip(vt_refs, probs))
            out_ref[:, qh * dv:(qh + 1) * dv] = (acc / den).T.astype(BF16)

    lat = (q_ref, o_ref, [kl_ref, kc_ref], [vtl_ref, vtc_ref])
    bound = bound_ref[0, pl.program_id(0)]

    @pl.when(bound <= MAX_BOUND_SHIFT)
    def _():
        run(*lat, shift=bound)

    @pl.when(jnp.logical_not(bound <= MAX_BOUND_SHIFT))
    def _():
        run(*lat)

    @pl.when(pl.program_id(2) == 0)
    def _():
        run(qc_ref, oc_ref, [kc_ref], [vtc_ref])


def _attention(q, k, vt, score_bounds, ffn_w, ffn_layer, *, kv_heads, group, dq, dv, heads_per_step, shared_k):
    hp = heads_per_step
    lat_tiles = SEQ // TQ
    ctx_block0 = N_LAT // CTX_LEN
    kw = dq if shared_k else hp * dq
    k_col = (lambda h: 0) if shared_k else (lambda h: h)
    width = kv_heads * group * dv
    h_steps = kv_heads // hp
    n_steps = BATCH * h_steps * lat_tiles
    assert n_steps >= CAST_CHUNKS
    r_in, r_out, r_shape = _ffn_cast_riders(
        ffn_w, ffn_layer, lambda b, h, t: ((b * h_steps + h) * lat_tiles + t) * CAST_CHUNKS // n_steps)
    body = functools.partial(_attn_kernel, heads=hp, group=group, dq=dq, dv=dv, shared_k=shared_k)
    return pl.pallas_call(
        _with_cast_riders(body, 7, 2),
        grid=(BATCH, h_steps, lat_tiles),
        in_specs=[
            pl.BlockSpec(memory_space=pltpu.SMEM),
            pl.BlockSpec((TQ, hp * group * dq), lambda b, h, t: (b * lat_tiles + t, h)),
            pl.BlockSpec((CTX_LEN, hp * group * dq), lambda b, h, t: (ctx_block0 + b, h)),
            pl.BlockSpec((SEQ, kw), lambda b, h, t: (b, k_col(h))),
            pl.BlockSpec((CTX_LEN, kw), lambda b, h, t: (ctx_block0 + b, k_col(h))),
            pl.BlockSpec((hp * VT_ROWS, SEQ), lambda b, h, t: (h, b)),
            pl.BlockSpec((hp * VT_ROWS, CTX_LEN), lambda b, h, t: (h, ctx_block0 + b)),
        ] + r_in,
        out_specs=[pl.BlockSpec((TQ, hp * group * dv), lambda b, h, t: (b * lat_tiles + t, h)),
                   pl.BlockSpec((CTX_LEN, hp * group * dv), lambda b, h, t: (b, h))] + r_out,
        out_shape=[jax.ShapeDtypeStruct((N_LAT, width), BF16), jax.ShapeDtypeStruct((N_CTX, width), BF16)] + r_shape,
        compiler_params=_params(3),
        name="attention",
    )(score_bounds.reshape(1, BATCH).astype(F32), q, q, k, k, vt, vt, *ffn_w)


def _ffn_tail(x1, m, g2_ref, w1_ref, w3_ref, w2_ref):
    half = x1.shape[0] // 2
    xa, xb = x1[:half], x1[half:]

    def up(x):
        h = _modulated_norm(x, g2_ref[...], m[3:4], m[4:5]).astype(BF16)
        return _dot(h, _w(w1_ref)), _dot(h, _w(w3_ref))

    def down(x, ab):
        a, b = ab
        gated = (a * jax.nn.sigmoid(a) * b).astype(BF16)
        return x + m[5:6] * _dot(gated, _w(w2_ref))

    ab_a = up(xa)
    ab_b = up(xb)
    ya = down(xa, ab_a)
    yb = down(xb, ab_b)
    return jnp.concatenate([ya, yb], axis=0)


def _attn_tail_kernel(al_ref, ac_ref, wo_ref, x_ref, mod_ref, g2_ref, w1_ref, w3_ref, w2_ref, o_ref):
    m = mod_ref[0]
    a = jnp.where(pl.program_id(0) < N_LAT // TM, al_ref[...], ac_ref[...])
    x1 = x_ref[...] + m[2:3] * _dot(a, _w(wo_ref))
    o_ref[...] = _ffn_tail(x1, m, g2_ref, w1_ref, w3_ref, w2_ref)


def _conv_tail_kernel(b_ref, u_ref, up_ref, un_ref, cw_ref, wo_ref, x_ref, mod_ref,
                      g2_ref, w1_ref, w3_ref, w2_ref, fg_ref, o_ref, *, final):
    m = mod_ref[0]
    x1 = x_ref[...] + m[2:3] * _conv_mix(b_ref, u_ref, up_ref, un_ref, cw_ref, wo_ref)
    x2 = _ffn_tail(x1, m, g2_ref, w1_ref, w3_ref, w2_ref)
    o_ref[...] = _head_norm(x2, fg_ref[...]) if final else x2


def _ffn_specs():
    return [_mod_spec(TM), _resident((1, D)), _resident((D, FFN_HIDDEN)),
            _resident((D, FFN_HIDDEN)), _resident((FFN_HIDDEN, D))]


def _attn_tail(a_lat, a_ctx, w_o, mixer_layer, x, mod, g2, w1, w3, w2):
    lat_tiles = N_LAT // TM
    return pl.pallas_call(
        _attn_tail_kernel,
        grid=(N_ALL // TM,),
        in_specs=[
            pl.BlockSpec((TM, D), lambda i: (jnp.minimum(i, lat_tiles - 1), 0)),
            pl.BlockSpec((TM, D), lambda i: (jnp.maximum(i - lat_tiles, 0), 0)),
            _resident((D, D), mixer_layer), _row_spec(TM, D),
        ] + _ffn_specs(),
        out_specs=_row_spec(TM, D),
        out_shape=jax.ShapeDtypeStruct((N_ALL, D), F32),
        compiler_params=_params(1),
        name="attn_tail",
    )(a_lat, a_ctx, w_o, x, mod, g2, w1, w3, w2)


def _conv_tail(b, u, conv_w, w_out, mixer_layer, x, mod, g2, w1, w3, w2, final_gain, n_rows, final):
    per = TM // HALO
    n_halo = n_rows // HALO
    return pl.pallas_call(
        functools.partial(_conv_tail_kernel, final=final),
        grid=(n_rows // TM,),
        in_specs=[
            _row_spec(TM, D), _row_spec(TM, D),
            pl.BlockSpec((HALO, D), lambda i: (jnp.maximum(i * per - 1, 0), 0)),
            pl.BlockSpec((HALO, D), lambda i: (jnp.minimum((i + 1) * per, n_halo - 1), 0)),
            _resident((3, D)), _resident((D, D), mixer_layer), _row_spec(TM, D),
        ] + _ffn_specs() + [_resident((1, D))],
        out_specs=_row_spec(TM, D),
        out_shape=jax.ShapeDtypeStruct((n_rows, D), F32),
        compiler_params=_params(1),
        name="conv_tail",
    )(b, u, u, u, conv_w, w_out, x, mod, g2, w1, w3, w2, final_gain)


def _axial_angles(rot_dim):
    n = rot_dim // 4
    rows = jnp.repeat(jnp.arange(SEQ // GRID_W, dtype=F32), GRID_W)
    cols = jnp.tile(jnp.arange(GRID_W, dtype=F32), SEQ // GRID_W)
    freqs = ROPE_THETA ** (-jnp.arange(n, dtype=F32) / n)
    return jnp.concatenate([rows[:, None] * freqs, cols[:, None] * freqs], axis=-1)


def _rope_tables(rot_dim):
    ang = _axial_angles(rot_dim)
    copies = LANES // (rot_dim // 2)
    cos_t = jnp.concatenate([jnp.tile(jnp.cos(ang), (1, copies)), jnp.ones((TP, LANES), F32)], axis=0)
    sin_t = jnp.concatenate([jnp.tile(jnp.sin(ang), (1, copies)), jnp.zeros((TP, LANES), F32)], axis=0)
    return cos_t, sin_t


def _gqa_slab_pairs(w, n_kv, n_group):
    half = GQA_HEAD_DIM // 2
    w = w.reshape(w.shape[0], n_kv, n_group, 2, half)
    return jnp.transpose(w, (0, 2, 3, 1, 4)).reshape(w.shape[0], n_group * GQA_QK)


def _mla_weights(w_dq, w_uq, w_dkv, w_ukv):
    half = MLA_ROPE // 2
    qk = MLA_NOPE + MLA_ROPE
    w_dq, w_uq, w_dkv, w_ukv = (w.astype(BF16) for w in (w_dq, w_uq, w_dkv, w_ukv))
    uq = w_uq.reshape(MLA_Q_RANK, MLA_HEADS // 2, 2, qk)
    nope = uq[..., :MLA_NOPE].reshape(MLA_Q_RANK, MLA_HEADS // 2, 2 * MLA_NOPE)
    x1 = uq[..., MLA_NOPE:MLA_NOPE + half].reshape(MLA_Q_RANK, MLA_HEADS // 2, 2 * half)
    x2 = uq[..., MLA_NOPE + half:].reshape(MLA_Q_RANK, MLA_HEADS // 2, 2 * half)
    uq = jnp.concatenate([nope, x1, x1, x2, x2], axis=-1).reshape(MLA_Q_RANK, MLA_HEADS * MLA_QK)
    k1 = w_dkv[:, MLA_KV_RANK:MLA_KV_RANK + half]
    k2 = w_dkv[:, MLA_KV_RANK + half:]
    dkv = jnp.concatenate([w_dkv[:, :MLA_KV_RANK]] + [k1] * 4 + [k2] * 4, axis=-1)
    ukv = w_ukv.reshape(MLA_KV_RANK, MLA_HEADS, MLA_NOPE + MLA_V)
    ukv = jnp.concatenate([ukv[:, :, :MLA_NOPE].reshape(MLA_KV_RANK, -1),
                           ukv[:, :, MLA_NOPE:].reshape(MLA_KV_RANK, -1)], axis=-1)
    return w_dq, uq, dkv, ukv


def kernel(x, c, ctx, c_ctx, ada_w, ada_b, norm1_g, norm2_g, ffn_w1, ffn_w3, ffn_w2, conv_w_in, conv_w, conv_w_out, gqa_wq, gqa_wk, gqa_wv, gqa_q_norm, gqa_k_norm, gqa_wo, mla_w_dq, mla_q_norm, mla_w_uq, mla_w_dkv, mla_kv_norm, mla_w_ukv, mla_wo, final_g):
    assert x.shape == (BATCH, SEQ, D) and ctx.shape == (BATCH, CTX_LEN, D)
    cond = jnp.concatenate(
        [c, c_ctx[None], jnp.zeros((MOD_ROWS - BATCH - 1, D), F32)], axis=0)
    mods = _modulation_tables(cond, ada_w, ada_b).reshape(DEPTH, MOD_ROWS, N_MOD, D)

    gqa_cos, gqa_sin = _rope_tables(GQA_HEAD_DIM)
    lane_half = jnp.arange(LANES) // (LANES // 2)
    half_ones = jnp.tile((lane_half[:, None] == lane_half[None, :]).astype(BF16), (2, 1))
    lane_head = jnp.arange(2 * LANES) // LANES
    pair_ones = (jnp.arange(SUBLANES)[:, None] == lane_head[None, :]).astype(BF16)
    mla_cos, mla_sin = _rope_tables(MLA_ROPE)

    ffn_w = (ffn_w1, ffn_w3, ffn_w2)
    conv_w_in, conv_w_out = conv_w_in.astype(BF16), conv_w_out.astype(BF16)
    gqa_wo = gqa_wo.astype(BF16)
    mla_wo = mla_wo.astype(BF16)

    xs = None
    for i in range(DEPTH):
        kind, j = i % 3, i // 3
        mod = mods[i]
        last = i == DEPTH - 1
        n_rows = N_LAT if last else N_ALL
        g1 = norm1_g[i][None]
        g2 = norm2_g[i][None]
        if kind == 0:
            if i == 0:
                b, u, xs, *ffn = _conv_in_first(x.reshape(N_LAT, D), ctx.reshape(N_CTX, D), mod, g1,
                                                conv_w_in, j, ffn_w, i)
            else:
                b, u, *ffn = _conv_in(xs, mod, g1, conv_w_in, j, n_rows, ffn_w, i)
            xs = _conv_tail(b, u, conv_w[j], conv_w_out, j, xs, mod, g2, *ffn, final_g[None], n_rows, last)
        elif kind == 1:
            w_qkv = jnp.concatenate([_gqa_slab_pairs(gqa_wq[j].astype(BF16), GQA_KV_HEADS, GQA_GROUP),
                                     _gqa_slab_pairs(gqa_wk[j].astype(BF16), GQA_KV_HEADS, 1),
                                     gqa_wv[j].astype(BF16)], axis=-1)
            q_gain = _gqa_slab_pairs(jnp.tile(gqa_q_norm[j], GQA_KV_HEADS)[None], GQA_KV_HEADS, 1)
            k_gain = _gqa_slab_pairs(jnp.tile(gqa_k_norm[j], GQA_KV_HEADS)[None], GQA_KV_HEADS, 1)
            q_gain = q_gain * (GQA_HEAD_DIM ** -0.5 * LOG2E)
            bound = 1.02 * GQA_HEAD_DIM * jnp.max(jnp.abs(q_gain)) * jnp.max(jnp.abs(k_gain))
            q, k, vt = _gqa_proj(xs, mod, g1, w_qkv, q_gain, k_gain, half_ones, gqa_cos, gqa_sin)
            a_lat, a_ctx, *ffn = _attention(q, k, vt, jnp.full((BATCH,), bound), ffn_w, i,
                                            kv_heads=GQA_KV_HEADS, group=GQA_GROUP, dq=GQA_QK, dv=GQA_HEAD_DIM,
                                            heads_per_step=1, shared_k=True)
            xs = _attn_tail(a_lat, a_ctx, gqa_wo, j, xs, mod, g2, *ffn)
        else:
            w_dq, w_uq, w_dkv, w_ukv = _mla_weights(mla_w_dq[j], mla_w_uq[j], mla_w_dkv[j], mla_w_ukv[j])
            q_gain = mla_q_norm[j][None] * ((MLA_NOPE + MLA_ROPE) ** -0.5 * LOG2E)
            q, k, vt, qn, kn = _mla_proj(xs, mod, g1, w_dq, q_gain, w_uq, w_dkv,
                                         mla_kv_norm[j][None], w_ukv, pair_ones, mla_cos, mla_sin)
            a_lat, a_ctx, *ffn = _attention(q, k, vt, _score_bounds(qn, kn), ffn_w, i,
                                            kv_heads=MLA_HEADS, group=1, dq=MLA_QK, dv=MLA_V,
                                            heads_per_step=4, shared_k=False)
            xs = _attn_tail(a_lat, a_ctx, mla_wo, j, xs, mod, g2, *ffn)

    assert (DEPTH - 1) % 3 == 0
    return xs.reshape(BATCH, SEQ, D)
```

```python
import functools

import jax
import jax.numpy as jnp
import numpy as np
from jax import lax
from jax.experimental import pallas as pl
from jax.experimental.pallas import tpu as pltpu

D = 1024
BATCH = 8
SEQ = 2048
DEPTH = 4
GRID_W = 64
CTX_LEN = 256
N_MOD = 6
FFN_HIDDEN = 2816
ROPE_THETA = 10000.0
EPS = 1e-6
GQA_HEAD_DIM = 128
GQA_HEADS = 8
GQA_KV_HEADS = 2
GQA_GROUP = 4
MLA_HEADS = 8
MLA_NOPE = 128
MLA_ROPE = 64
MLA_V = 128
MLA_KV_RANK = 256
MLA_Q_RANK = 768
MLA_QK = 256
GQA_QK = 256
LANES = 128
SUBLANES = 8
VT_ROWS = 128

N_LAT = BATCH * SEQ
N_CTX = BATCH * CTX_LEN
N_ALL = N_LAT + N_CTX
MOD_ROWS = 16
N_CONV_LAYERS = (DEPTH + 2) // 3

TM = 512
TP = 1024
TQ = 1024
HALO = 8
CONV_COLS = 256
FFN_SPLITS = 2
CAST_CHUNKS = 16
VMEM_LIMIT = 56 * 1024 * 1024
LOG2E = 1.4426950408889634
MAX_BOUND_SHIFT = 50.0

F32 = jnp.float32
BF16 = jnp.bfloat16


def _params(n_axes):
    return pltpu.CompilerParams(
        dimension_semantics=("arbitrary",) * n_axes, vmem_limit_bytes=VMEM_LIMIT)


def _resident(shape, layer=None):
    nd = len(shape)
    if layer is None:
        return pl.BlockSpec(shape, lambda *_: (0,) * nd, pipeline_mode=pl.Buffered(1))
    return pl.BlockSpec((1,) + tuple(shape), lambda *_: (layer,) + (0,) * nd, pipeline_mode=pl.Buffered(1))


def _w(ref):
    return ref[0] if len(ref.shape) == 3 else ref[...]


def _cast_riders(stacks, chunk_of):
    in_specs, out_specs, out_shapes = [], [], []
    for w, layer in stacks:
        _, rows, cols = w.shape
        r = rows // CAST_CHUNKS
        in_specs.append(pl.BlockSpec((1, r, cols), lambda *g, layer=layer: (layer, chunk_of(*g), 0)))
        out_specs.append(pl.BlockSpec((r, cols), lambda *g: (chunk_of(*g), 0)))
        out_shapes.append(jax.ShapeDtypeStruct((rows, cols), BF16))
    return in_specs, out_specs, out_shapes


def _mod_rider(first_layer, n_layers, chunk_of):
    tn = N_MOD * D // CAST_CHUNKS
    assert first_layer % n_layers == 0
    blk = first_layer // n_layers
    in_specs = [pl.BlockSpec((MOD_ROWS, D), lambda *g: (0, 0)),
                pl.BlockSpec((n_layers, D, tn), lambda *g: (blk, 0, chunk_of(*g))),
                pl.BlockSpec((n_layers, 1, tn), lambda *g: (blk, 0, chunk_of(*g)))]
    out_spec = pl.BlockSpec((n_layers, MOD_ROWS, tn), lambda *g: (0, 0, chunk_of(*g)))
    return in_specs, out_spec, jax.ShapeDtypeStruct((n_layers, MOD_ROWS, N_MOD * D), F32)


def _with_riders(body, n_in, n_out, mod_rider=False):
    n_mod_in, n_mod_out = (3, 1) if mod_rider else (0, 0)

    def kern(*refs):
        n_casts = (len(refs) - n_in - n_out - n_mod_in - n_mod_out) // 2
        ins, cast_in = refs[:n_in], refs[n_in:n_in + n_casts]
        mod_in = refs[n_in + n_casts:n_in + n_casts + n_mod_in]
        o0 = n_in + n_casts + n_mod_in
        outs, cast_out = refs[o0:o0 + n_out], refs[o0 + n_out:o0 + n_out + n_casts]
        mod_out = refs[o0 + n_out + n_casts:]
        body(*ins, *outs)
        for src, dst in zip(cast_in, cast_out):
            dst[...] = src[0].astype(dst.dtype)
        if mod_rider:
            _mod_kernel(*mod_in, *mod_out)
    return kern


def _mod_spec(tm):
    return pl.BlockSpec((1, N_MOD, D), lambda i: (jnp.minimum(i * tm // SEQ, BATCH), 0, 0))


def _row_spec(tm, width):
    return pl.BlockSpec((tm, width), lambda i: (i, 0))


def _col_spec(height, tm):
    return pl.BlockSpec((height, tm), lambda i: (0, i))


def _dot(a, b):
    return jnp.dot(a, b, preferred_element_type=F32)


def _modulated_norm(x, gain, shift, scale):
    r = lax.rsqrt(jnp.mean(x * x, axis=-1, keepdims=True) + EPS)
    return (x * r) * (gain * (1.0 + scale)) + shift


def _head_norm(x, gain):
    r = lax.rsqrt(jnp.mean(x * x, axis=-1, keepdims=True) + EPS)
    return (x * r) * gain


def _store_vt(vt_ref, v, n_heads):
    for j in range(n_heads):
        vt_ref[j * VT_ROWS:(j + 1) * VT_ROWS, :] = v[:, j * VT_ROWS:(j + 1) * VT_ROWS].T.astype(BF16)


def _rope(a, b, cos, sin):
    return a * cos - b * sin, a * sin + b * cos


def _lane_mask(lo, hi):
    lane = lax.broadcasted_iota(jnp.int32, (1, LANES), 1)
    return (lane >= lo) & (lane < hi)


def _mod_kernel(cond_ref, w_ref, b_ref, o_ref):
    c = cond_ref[...]
    act = (c * jax.nn.sigmoid(c)).astype(BF16)
    for l in range(w_ref.shape[0]):
        o_ref[l] = _dot(act, w_ref[l].astype(BF16)) + b_ref[l]


def _modulation_tables(cond, ada_w, ada_b, n_layers):
    tn = 1536
    return pl.pallas_call(
        _mod_kernel,
        grid=(n_layers, N_MOD * D // tn),
        in_specs=[
            pl.BlockSpec((MOD_ROWS, D), lambda l, j: (0, 0)),
            pl.BlockSpec((1, D, tn), lambda l, j: (l, 0, j)),
            pl.BlockSpec((1, 1, tn), lambda l, j: (l, 0, j)),
        ],
        out_specs=pl.BlockSpec((1, MOD_ROWS, tn), lambda l, j: (l, 0, j)),
        out_shape=jax.ShapeDtypeStruct((n_layers, MOD_ROWS, N_MOD * D), F32),
        compiler_params=_params(2),
        name="modulation",
    )(cond, ada_w, ada_b)


def _conv_in_compute(x, mod_ref, g_ref, w_ref, b_ref, u_ref):
    m = mod_ref[0]
    h = _modulated_norm(x, g_ref[...], m[0:1], m[1:2]).astype(BF16)
    y = _dot(h, _w(w_ref))
    b_ref[...] = y[:, :D].astype(BF16)
    u_ref[...] = y[:, D:2 * D] * y[:, 2 * D:]


def _conv_in_kernel(x_ref, mod_ref, g_ref, w_ref, b_ref, u_ref):
    _conv_in_compute(x_ref[...], mod_ref, g_ref, w_ref, b_ref, u_ref)


def _conv_in_first_kernel(lat_ref, ctx_ref, mod_ref, g_ref, w_ref, b_ref, u_ref, xs_ref):
    x = jnp.where(pl.program_id(0) < N_LAT // TP, lat_ref[...], ctx_ref[...])
    xs_ref[...] = x
    _conv_in_compute(x, mod_ref, g_ref, w_ref, b_ref, u_ref)


def _conv_in(x, mod, gain, w_in, layer, n_rows, ffn_w, ffn_layer):
    r_in, r_out, r_shape = _cast_riders([(w, ffn_layer) for w in ffn_w], lambda i: jnp.minimum(i, CAST_CHUNKS - 1))
    assert n_rows // TP >= CAST_CHUNKS
    return pl.pallas_call(
        _with_riders(_conv_in_kernel, 4, 2),
        grid=(n_rows // TP,),
        in_specs=[_row_spec(TP, D), _mod_spec(TP), _resident((1, D)), _resident((D, 3 * D), layer)] + r_in,
        out_specs=[_row_spec(TP, D), _row_spec(TP, D)] + r_out,
        out_shape=[jax.ShapeDtypeStruct((n_rows, D), BF16), jax.ShapeDtypeStruct((n_rows, D), F32)] + r_shape,
        compiler_params=_params(1),
        name="conv_in",
    )(x, mod, gain, w_in, *ffn_w)


def _conv_in_first(lat, ctx, mod, gain, w_in, layer, ffn_w, ffn_layer):
    lat_tiles = N_LAT // TP
    r_in, r_out, r_shape = _cast_riders([(w, ffn_layer) for w in ffn_w], lambda i: jnp.minimum(i, CAST_CHUNKS - 1))
    assert N_ALL // TP >= CAST_CHUNKS
    return pl.pallas_call(
        _with_riders(_conv_in_first_kernel, 5, 3),
        grid=(N_ALL // TP,),
        in_specs=[
            pl.BlockSpec((TP, D), lambda i: (jnp.minimum(i, lat_tiles - 1), 0)),
            pl.BlockSpec((TP, D), lambda i: (jnp.maximum(i - lat_tiles, 0), 0)),
            _mod_spec(TP), _resident((1, D)), _resident((D, 3 * D), layer),
        ] + r_in,
        out_specs=[_row_spec(TP, D), _row_spec(TP, D), _row_spec(TP, D)] + r_out,
        out_shape=[jax.ShapeDtypeStruct((N_ALL, D), BF16), jax.ShapeDtypeStruct((N_ALL, D), F32),
                   jax.ShapeDtypeStruct((N_ALL, D), F32)] + r_shape,
        compiler_params=_params(1),
        name="conv_in_first",
    )(lat, ctx, mod, gain, w_in, *ffn_w)


def _conv_gate(b_ref, u_ref, up_ref, un_ref, cw_ref, cols):
    tm = u_ref.shape[0]
    u = u_ref[:, cols]
    local = lax.broadcasted_iota(jnp.int32, (tm, 1), 0)
    row = local + pl.program_id(0) * tm
    in_ctx = row >= N_LAT
    first = ((row & (CTX_LEN - 1)) == 0) & (in_ctx | ((row & (SEQ - 1)) == 0))
    last = ((row & (CTX_LEN - 1)) == CTX_LEN - 1) & (in_ctx | ((row & (SEQ - 1)) == SEQ - 1))
    prev = jnp.where(local == 0, up_ref[HALO - 1:HALO, cols], pltpu.roll(u, 1, axis=0))
    prev = jnp.where(first, 0.0, prev)
    nxt = jnp.where(local == tm - 1, un_ref[0:1, cols], pltpu.roll(u, tm - 1, axis=0))
    nxt = jnp.where(last, 0.0, nxt)
    cw = cw_ref[:, cols]
    z = prev * cw[0:1] + u * cw[1:2] + nxt * cw[2:3]
    return (b_ref[:, cols].astype(F32) * z).astype(BF16)


def _conv_mix(b_ref, u_ref, up_ref, un_ref, cw_ref, wo_ref):
    wo = wo_ref.at[0] if len(wo_ref.shape) == 3 else wo_ref
    acc = None
    for c0 in range(0, D, CONV_COLS):
        cols = slice(c0, c0 + CONV_COLS)
        part = _dot(_conv_gate(b_ref, u_ref, up_ref, un_ref, cw_ref, cols), wo[cols, :])
        acc = part if acc is None else acc + part
    return acc


def _gqa_proj_kernel(x_ref, mod_ref, g_ref, w_ref, qg_ref, kg_ref, ones_ref, cos_ref, sin_ref,
                     q_ref, k_ref, vt_ref):
    m = mod_ref[0]
    h = _modulated_norm(x_ref[...], g_ref[...], m[0:1], m[1:2]).astype(BF16)
    cos = cos_ref[...]
    sin = sin_ref[...]
    ones2 = ones_ref[...]

    def normed_rotated(y, gain):
        a, b = y[:, :LANES], y[:, LANES:]
        ss = a * a + b * b
        hi = ss.astype(BF16)
        lo = (ss - hi.astype(F32)).astype(BF16)
        tot = _dot(jnp.concatenate([hi, lo], axis=1), ones2)
        r = lax.rsqrt(tot * (1.0 / GQA_HEAD_DIM) + EPS)
        return _rope(a * r * gain[:, :LANES], b * r * gain[:, LANES:], cos, sin)

    wide = 2 * GQA_QK
    kv0 = _lane_mask(0, LANES // 2)
    for gg in range(GQA_GROUP // 2):
        y = _dot(h, w_ref[:, gg * wide:(gg + 1) * wide])
        for t in range(2):
            g = 2 * gg + t
            o1, o2 = normed_rotated(y[:, t * GQA_QK:(t + 1) * GQA_QK], qg_ref[...])
            for kv, keep in ((0, kv0), (1, jnp.logical_not(kv0))):
                c0 = (kv * GQA_GROUP + g) * GQA_QK
                q_ref[:, c0:c0 + LANES] = jnp.where(keep, o1, 0.0).astype(BF16)
                q_ref[:, c0 + LANES:c0 + GQA_QK] = jnp.where(keep, o2, 0.0).astype(BF16)
    y = _dot(h, w_ref[:, GQA_GROUP * GQA_QK:])
    o1, o2 = normed_rotated(y[:, :GQA_QK], kg_ref[...])
    k_ref[:, :LANES] = o1.astype(BF16)
    k_ref[:, LANES:] = o2.astype(BF16)
    _store_vt(vt_ref, y[:, GQA_QK:], GQA_KV_HEADS)


def _rope_spec(tm):
    per_seq = SEQ // tm
    return pl.BlockSpec(
        (tm, LANES), lambda i: (jnp.where(i < N_LAT // tm, i % per_seq, per_seq), 0))


def _gqa_proj(x, mod, gain, w_qkv, q_gain, k_gain, half_ones, cos, sin):
    kvw = GQA_KV_HEADS * GQA_HEAD_DIM
    qw = GQA_HEADS * GQA_QK
    return pl.pallas_call(
        _gqa_proj_kernel,
        grid=(N_ALL // TP,),
        in_specs=[
            _row_spec(TP, D), _mod_spec(TP), _resident((1, D)),
            _resident((D, D + 2 * kvw)),
            _resident((1, GQA_QK)), _resident((1, GQA_QK)), _resident((2 * LANES, LANES)),
            _rope_spec(TP), _rope_spec(TP),
        ],
        out_specs=[_row_spec(TP, qw), _row_spec(TP, GQA_QK), _col_spec(GQA_KV_HEADS * VT_ROWS, TP)],
        out_shape=[jax.ShapeDtypeStruct((N_ALL, qw), BF16),
                   jax.ShapeDtypeStruct((N_ALL, GQA_QK), BF16),
                   jax.ShapeDtypeStruct((GQA_KV_HEADS * VT_ROWS, N_ALL), BF16)],
        compiler_params=_params(1),
        name="gqa_proj",
    )(x, mod, gain, w_qkv, q_gain, k_gain, half_ones, cos, sin)


def _mla_proj_kernel(x_ref, mod_ref, g_ref, wdq_ref, qg_ref, wuq_ref, wdkv_ref, kvg_ref, wukv_ref,
                     ones_ref, cos_ref, sin_ref, q_ref, k_ref, vt_ref, qn_ref, kn_ref):
    m = mod_ref[0]
    h = _modulated_norm(x_ref[...], g_ref[...], m[0:1], m[1:2]).astype(BF16)
    cos = cos_ref[...]
    sin = sin_ref[...]
    half = MLA_ROPE // 2
    cq = _head_norm(_dot(h, wdq_ref[...]), qg_ref[...]).astype(BF16)
    q = _dot(cq, wuq_ref[...])
    ckv_pe = _dot(h, wdkv_ref[...])
    ckv = _head_norm(ckv_pe[:, :MLA_KV_RANK], kvg_ref[...]).astype(BF16)
    ko1, ko2 = _rope(ckv_pe[:, MLA_KV_RANK:MLA_KV_RANK + LANES], ckv_pe[:, MLA_KV_RANK + LANES:], cos, sin)
    k_pe = jnp.where(_lane_mask(0, 2 * half), ko1, ko2)
    k_pe_sq = k_pe * k_pe
    k_pe = k_pe.astype(BF16)
    kv = _dot(ckv, wukv_ref[...])
    ones2 = ones_ref[...]

    def max_sq_norm(sq_pair):
        sums = _nt_dot(ones2, jnp.concatenate(sq_pair, axis=1).astype(BF16))
        top = jnp.max(sums, axis=1, keepdims=True)
        lane = lax.broadcasted_iota(jnp.int32, (1, 2 * LANES), 1)
        return jnp.where(lane < LANES, top[0:1], top[1:2])

    q_norms, k_norms = [], []
    for pp in range(MLA_HEADS // 2):
        p0 = pp * 2 * MLA_QK
        o1, o2 = _rope(q[:, p0 + 2 * MLA_NOPE:p0 + 2 * MLA_NOPE + LANES], q[:, p0 + 2 * MLA_NOPE + LANES:p0 + 2 * MLA_QK],
                       cos, sin)
        q_sq, k_sq = [], []
        for t in range(2):
            j = 2 * pp + t
            c0 = j * MLA_QK
            pe = jnp.where(_lane_mask(t * half, (t + 1) * half), o1,
                           jnp.where(_lane_mask(2 * half + t * half, 2 * half + (t + 1) * half), o2, 0.0))
            q_nope = q[:, p0 + t * MLA_NOPE:p0 + (t + 1) * MLA_NOPE]
            k_nope = kv[:, j * MLA_NOPE:(j + 1) * MLA_NOPE]
            q_ref[:, c0:c0 + MLA_NOPE] = q_nope.astype(BF16)
            q_ref[:, c0 + MLA_NOPE:c0 + MLA_QK] = pe.astype(BF16)
            k_ref[:, c0:c0 + MLA_NOPE] = k_nope.astype(BF16)
            k_ref[:, c0 + MLA_NOPE:c0 + MLA_QK] = k_pe
            q_sq.append(q_nope * q_nope + pe * pe)
            k_sq.append(k_nope * k_nope + k_pe_sq)
        q_norms.append(max_sq_norm(q_sq))
        k_norms.append(max_sq_norm(k_sq))
    qn_ref[0] = jnp.broadcast_to(jnp.concatenate(q_norms, axis=1), qn_ref.shape[1:])
    kn_ref[0] = jnp.broadcast_to(jnp.concatenate(k_norms, axis=1), kn_ref.shape[1:])
    _store_vt(vt_ref, kv[:, MLA_HEADS * MLA_NOPE:], MLA_HEADS)


def _mla_proj(x, mod, gain, w_dq, q_gain, w_uq, w_dkv, kv_gain, w_ukv, pair_ones, cos, sin):
    qkw = MLA_HEADS * MLA_QK
    n_tiles = N_ALL // TP
    norm_spec = pl.BlockSpec((1, SUBLANES, MLA_HEADS * LANES), lambda i: (i, 0, 0))
    norm_shape = jax.ShapeDtypeStruct((n_tiles, SUBLANES, MLA_HEADS * LANES), F32)
    return pl.pallas_call(
        _mla_proj_kernel,
        grid=(N_ALL // TP,),
        in_specs=[
            _row_spec(TP, D), _mod_spec(TP), _resident((1, D)),
            _resident((D, MLA_Q_RANK)), _resident((1, MLA_Q_RANK)), _resident((MLA_Q_RANK, qkw)),
            _resident((D, MLA_KV_RANK + 2 * LANES)), _resident((1, MLA_KV_RANK)),
            _resident((MLA_KV_RANK, MLA_HEADS * (MLA_NOPE + MLA_V))),
            _resident((SUBLANES, 2 * LANES)),
            _rope_spec(TP), _rope_spec(TP),
        ],
        out_specs=[_row_spec(TP, qkw), _row_spec(TP, qkw), _col_spec(MLA_HEADS * VT_ROWS, TP),
                   norm_spec, norm_spec],
        out_shape=[jax.ShapeDtypeStruct((N_ALL, qkw), BF16),
                   jax.ShapeDtypeStruct((N_ALL, qkw), BF16),
                   jax.ShapeDtypeStruct((MLA_HEADS * VT_ROWS, N_ALL), BF16),
                   norm_shape, norm_shape],
        compiler_params=_params(1),
        name="mla_proj",
    )(x, mod, gain, w_dq, q_gain, w_uq, w_dkv, kv_gain, w_ukv, pair_ones, cos, sin)


def _score_bounds(qn, kn):
    qn = qn[:, 0, ::LANES]
    kn = kn[:, 0, ::LANES]
    per_seq = SEQ // TP
    q_lat = jnp.max(qn[:N_LAT // TP].reshape(BATCH, per_seq, -1), axis=1)
    k_lat = jnp.max(kn[:N_LAT // TP].reshape(BATCH, per_seq, -1), axis=1)
    ctx_tile = N_LAT // TP + (jnp.arange(BATCH) * CTX_LEN) // TP
    k_all = jnp.maximum(k_lat, kn[ctx_tile])
    return 1.02 * jnp.sqrt(jnp.max(q_lat * k_all, axis=1))


def _nt_dot(a, b):
    return lax.dot_general(a, b, (((1,), (1,)), ((), ())), preferred_element_type=F32)


def _attn_scores(q, k_parts):
    return [_nt_dot(k, q) for k in k_parts]


def _attn_kernel(bound_ref, q_ref, qc_ref, kl_ref, kc_ref, vtl_ref, vtc_ref, o_ref, oc_ref,
                 *, heads, group, dq, dv, shared_k):
    n = heads * group

    def k_cols(qh):
        kv = 0 if shared_k else qh // group
        return slice(kv * dq, (kv + 1) * dq)

    def v_rows(qh):
        return slice(qh // group * VT_ROWS, (qh // group + 1) * VT_ROWS)

    def run(queries, out_ref, k_refs, vt_refs, shift=None):
        def scores(qh):
            return _attn_scores(queries[:, qh * dq:(qh + 1) * dq], [k[:, k_cols(qh)] for k in k_refs])

        ahead = 2 if shift is None else 1
        pending = [scores(qh) for qh in range(min(ahead, n))]
        for qh in range(n):
            cur = pending.pop(0)
            if shift is None:
                mx = functools.reduce(jnp.maximum, [jnp.max(s, axis=0, keepdims=True) for s in cur])
            else:
                mx = shift
            probs = [jnp.exp2(s - mx) for s in cur]
            den = sum(jnp.sum(p, axis=0, keepdims=True) for p in probs)
            probs = [p.astype(BF16) for p in probs]
            if qh + ahead < n:
                pending.append(scores(qh + ahead))
            acc = sum(_dot(vt[v_rows(qh), :], p) for vt, p in zip(vt_refs, probs))
            out_ref[:, qh * dv:(qh + 1) * dv] = (acc / den).T.astype(BF16)

    lat = (q_ref, o_ref, [kl_ref, kc_ref], [vtl_ref, vtc_ref])
    bound = bound_ref[0, pl.program_id(0)]

    @pl.when(bound <= MAX_BOUND_SHIFT)
    def _():
        run(*lat, shift=bound)

    @pl.when(jnp.logical_not(bound <= MAX_BOUND_SHIFT))
    def _():
        run(*lat)

    @pl.when(pl.program_id(2) == 0)
    def _():
        run(qc_ref, oc_ref, [kc_ref], [vtc_ref])


def _attention(q, k, vt, score_bounds, casts, mod_layers, mod_operands, *, kv_heads, group, dq, dv,
               heads_per_step, shared_k):
    hp = heads_per_step
    lat_tiles = SEQ // TQ
    ctx_block0 = N_LAT // CTX_LEN
    kw = dq if shared_k else hp * dq
    k_col = (lambda h: 0) if shared_k else (lambda h: h)
    width = kv_heads * group * dv
    h_steps = kv_heads // hp
    n_steps = BATCH * h_steps * lat_tiles
    assert n_steps >= CAST_CHUNKS
    def chunk_of(b, h, t):
        return ((b * h_steps + h) * lat_tiles + t) * CAST_CHUNKS // n_steps

    r_in, r_out, r_shape = _cast_riders(casts, chunk_of)
    rider_operands = [w for w, _ in casts]
    if mod_layers is not None:
        m_in, m_out, m_shape = _mod_rider(*mod_layers, chunk_of)
        r_in, r_out, r_shape = r_in + m_in, r_out + [m_out], r_shape + [m_shape]
        rider_operands += list(mod_operands)
    body = functools.partial(_attn_kernel, heads=hp, group=group, dq=dq, dv=dv, shared_k=shared_k)
    return pl.pallas_call(
        _with_riders(body, 7, 2, mod_rider=mod_layers is not None),
        grid=(BATCH, h_steps, lat_tiles),
        in_specs=[
            pl.BlockSpec(memory_space=pltpu.SMEM),
            pl.BlockSpec((TQ, hp * group * dq), lambda b, h, t: (b * lat_tiles + t, h)),
            pl.BlockSpec((CTX_LEN, hp * group * dq), lambda b, h, t: (ctx_block0 + b, h)),
            pl.BlockSpec((SEQ, kw), lambda b, h, t: (b, k_col(h))),
            pl.BlockSpec((CTX_LEN, kw), lambda b, h, t: (ctx_block0 + b, k_col(h))),
            pl.BlockSpec((hp * VT_ROWS, SEQ), lambda b, h, t: (h, b)),
            pl.BlockSpec((hp * VT_ROWS, CTX_LEN), lambda b, h, t: (h, ctx_block0 + b)),
        ] + r_in,
        out_specs=[pl.BlockSpec((TQ, hp * group * dv), lambda b, h, t: (b * lat_tiles + t, h)),
                   pl.BlockSpec((CTX_LEN, hp * group * dv), lambda b, h, t: (b, h))] + r_out,
        out_shape=[jax.ShapeDtypeStruct((N_LAT, width), BF16), jax.ShapeDtypeStruct((N_CTX, width), BF16)] + r_shape,
        compiler_params=_params(3),
        name="attention",
    )(score_bounds.reshape(1, BATCH).astype(F32), q, q, k, k, vt, vt, *rider_operands)


def _ffn_tail(x1, m, g2_ref, w1_ref, w3_ref, w2_ref):
    rows = x1.shape[0] // FFN_SPLITS
    parts = [x1[i * rows:(i + 1) * rows] for i in range(FFN_SPLITS)]

    def up(x):
        h = _modulated_norm(x, g2_ref[...], m[3:4], m[4:5]).astype(BF16)
        return _dot(h, _w(w1_ref)), _dot(h, _w(w3_ref))

    def down(x, ab):
        a, b = ab
        gated = (a * jax.nn.sigmoid(a) * b).astype(BF16)
        return x + m[5:6] * _dot(gated, _w(w2_ref))

    outs = []
    nxt = up(parts[0])
    for i, x in enumerate(parts):
        cur = nxt
        if i + 1 < FFN_SPLITS:
            nxt = up(parts[i + 1])
        outs.append(down(x, cur))
    return jnp.concatenate(outs, axis=0)


def _attn_tail_kernel(al_ref, ac_ref, wo_ref, x_ref, mod_ref, g2_ref, w1_ref, w3_ref, w2_ref, o_ref):
    m = mod_ref[0]
    a = jnp.where(pl.program_id(0) < N_LAT // TM, al_ref[...], ac_ref[...])
    x1 = x_ref[...] + m[2:3] * _dot(a, _w(wo_ref))
    o_ref[...] = _ffn_tail(x1, m, g2_ref, w1_ref, w3_ref, w2_ref)


def _conv_tail_kernel(b_ref, u_ref, up_ref, un_ref, cw_ref, wo_ref, x_ref, mod_ref,
                      g2_ref, w1_ref, w3_ref, w2_ref, fg_ref, o_ref, *, final):
    m = mod_ref[0]
    x1 = x_ref[...] + m[2:3] * _conv_mix(b_ref, u_ref, up_ref, un_ref, cw_ref, wo_ref)
    x2 = _ffn_tail(x1, m, g2_ref, w1_ref, w3_ref, w2_ref)
    o_ref[...] = _head_norm(x2, fg_ref[...]) if final else x2


def _ffn_specs():
    return [_mod_spec(TM), _resident((1, D)), _resident((D, FFN_HIDDEN)),
            _resident((D, FFN_HIDDEN)), _resident((FFN_HIDDEN, D))]


def _attn_tail(a_lat, a_ctx, w_o, mixer_layer, x, mod, g2, w1, w3, w2):
    lat_tiles = N_LAT // TM
    return pl.pallas_call(
        _attn_tail_kernel,
        grid=(N_ALL // TM,),
        in_specs=[
            pl.BlockSpec((TM, D), lambda i: (jnp.minimum(i, lat_tiles - 1), 0)),
            pl.BlockSpec((TM, D), lambda i: (jnp.maximum(i - lat_tiles, 0), 0)),
            _resident((D, D), mixer_layer), _row_spec(TM, D),
        ] + _ffn_specs(),
        out_specs=_row_spec(TM, D),
        out_shape=jax.ShapeDtypeStruct((N_ALL, D), F32),
        compiler_params=_params(1),
        name="attn_tail",
    )(a_lat, a_ctx, w_o, x, mod, g2, w1, w3, w2)


def _conv_tail(b, u, conv_w, w_out, mixer_layer, x, mod, g2, w1, w3, w2, final_gain, n_rows, final):
    per = TM // HALO
    n_halo = n_rows // HALO
    return pl.pallas_call(
        functools.partial(_conv_tail_kernel, final=final),
        grid=(n_rows // TM,),
        in_specs=[
            _row_spec(TM, D), _row_spec(TM, D),
            pl.BlockSpec((HALO, D), lambda i: (jnp.maximum(i * per - 1, 0), 0)),
            pl.BlockSpec((HALO, D), lambda i: (jnp.minimum((i + 1) * per, n_halo - 1), 0)),
            _resident((3, D)), _resident((D, D), mixer_layer), _row_spec(TM, D),
        ] + _ffn_specs() + [_resident((1, D))],
        out_specs=_row_spec(TM, D),
        out_shape=jax.ShapeDtypeStruct((n_rows, D), F32),
        compiler_params=_params(1),
        name="conv_tail",
    )(b, u, u, u, conv_w, w_out, x, mod, g2, w1, w3, w2, final_gain)


def _axial_angles(rot_dim):
    n = rot_dim // 4
    rows = jnp.repeat(jnp.arange(SEQ // GRID_W, dtype=F32), GRID_W)
    cols = jnp.tile(jnp.arange(GRID_W, dtype=F32), SEQ // GRID_W)
    freqs = ROPE_THETA ** (-jnp.arange(n, dtype=F32) / n)
    return jnp.concatenate([rows[:, None] * freqs, cols[:, None] * freqs], axis=-1)


def _rope_tables(rot_dim):
    ang = _axial_angles(rot_dim)
    copies = LANES // (rot_dim // 2)
    cos_t = jnp.concatenate([jnp.tile(jnp.cos(ang), (1, copies)), jnp.ones((TP, LANES), F32)], axis=0)
    sin_t = jnp.concatenate([jnp.tile(jnp.sin(ang), (1, copies)), jnp.zeros((TP, LANES), F32)], axis=0)
    return cos_t, sin_t


def _gqa_slab_pairs(w, n_kv, n_group):
    half = GQA_HEAD_DIM // 2
    w = w.reshape(w.shape[0], n_kv, n_group, 2, half)
    return jnp.transpose(w, (0, 2, 3, 1, 4)).reshape(w.shape[0], n_group * GQA_QK)


def _mla_weights(w_dq, w_uq, w_dkv, w_ukv):
    half = MLA_ROPE // 2
    qk = MLA_NOPE + MLA_ROPE
    w_dq, w_uq, w_dkv, w_ukv = (w.astype(BF16) for w in (w_dq, w_uq, w_dkv, w_ukv))
    uq = w_uq.reshape(MLA_Q_RANK, MLA_HEADS // 2, 2, qk)
    nope = uq[..., :MLA_NOPE].reshape(MLA_Q_RANK, MLA_HEADS // 2, 2 * MLA_NOPE)
    x1 = uq[..., MLA_NOPE:MLA_NOPE + half].reshape(MLA_Q_RANK, MLA_HEADS // 2, 2 * half)
    x2 = uq[..., MLA_NOPE + half:].reshape(MLA_Q_RANK, MLA_HEADS // 2, 2 * half)
    uq = jnp.concatenate([nope, x1, x1, x2, x2], axis=-1).reshape(MLA_Q_RANK, MLA_HEADS * MLA_QK)
    k1 = w_dkv[:, MLA_KV_RANK:MLA_KV_RANK + half]
    k2 = w_dkv[:, MLA_KV_RANK + half:]
    dkv = jnp.concatenate([w_dkv[:, :MLA_KV_RANK]] + [k1] * 4 + [k2] * 4, axis=-1)
    ukv = w_ukv.reshape(MLA_KV_RANK, MLA_HEADS, MLA_NOPE + MLA_V)
    ukv = jnp.concatenate([ukv[:, :, :MLA_NOPE].reshape(MLA_KV_RANK, -1),
                           ukv[:, :, MLA_NOPE:].reshape(MLA_KV_RANK, -1)], axis=-1)
    return w_dq, uq, dkv, ukv


def kernel(x, c, ctx, c_ctx, ada_w, ada_b, norm1_g, norm2_g, ffn_w1, ffn_w3, ffn_w2, conv_w_in, conv_w, conv_w_out, gqa_wq, gqa_wk, gqa_wv, gqa_q_norm, gqa_k_norm, gqa_wo, mla_w_dq, mla_q_norm, mla_w_uq, mla_w_dkv, mla_kv_norm, mla_w_ukv, mla_wo, final_g):
    assert x.shape == (BATCH, SEQ, D) and ctx.shape == (BATCH, CTX_LEN, D)
    cond = jnp.concatenate(
        [c, c_ctx[None], jnp.zeros((MOD_ROWS - BATCH - 1, D), F32)], axis=0)
    mod_operands = (cond, ada_w, ada_b.reshape(DEPTH, 1, N_MOD * D))
    mods = list(_modulation_tables(*mod_operands, 2).reshape(2, MOD_ROWS, N_MOD, D))

    gqa_cos, gqa_sin = _rope_tables(GQA_HEAD_DIM)
    lane_half = jnp.arange(LANES) // (LANES // 2)
    half_ones = jnp.tile((lane_half[:, None] == lane_half[None, :]).astype(BF16), (2, 1))
    lane_head = jnp.arange(2 * LANES) // LANES
    pair_ones = (jnp.arange(SUBLANES)[:, None] == lane_head[None, :]).astype(BF16)
    mla_cos, mla_sin = _rope_tables(MLA_ROPE)

    ffn_w = (ffn_w1, ffn_w3, ffn_w2)
    conv_stacks = (conv_w_in, conv_w_out)
    conv_bf16 = {0: (conv_w_in[0].astype(BF16), conv_w_out[0].astype(BF16))}
    gqa_wo = gqa_wo.astype(BF16)
    mla_wo = mla_wo.astype(BF16)

    xs = None
    for i in range(DEPTH):
        kind, j = i % 3, i // 3
        mod = mods[i]
        last = i == DEPTH - 1
        n_rows = N_LAT if last else N_ALL
        g1 = norm1_g[i][None]
        g2 = norm2_g[i][None]
        if kind == 0:
            w_in, w_out = conv_bf16[j]
            if i == 0:
                b, u, xs, *ffn = _conv_in_first(x.reshape(N_LAT, D), ctx.reshape(N_CTX, D), mod, g1,
                                                w_in, None, ffn_w, i)
            else:
                b, u, *ffn = _conv_in(xs, mod, g1, w_in, None, n_rows, ffn_w, i)
            xs = _conv_tail(b, u, conv_w[j], w_out, None, xs, mod, g2, *ffn, final_g[None], n_rows, last)
        elif kind == 1:
            w_qkv = jnp.concatenate([_gqa_slab_pairs(gqa_wq[j].astype(BF16), GQA_KV_HEADS, GQA_GROUP),
                                     _gqa_slab_pairs(gqa_wk[j].astype(BF16), GQA_KV_HEADS, 1),
                                     gqa_wv[j].astype(BF16)], axis=-1)
            q_gain = _gqa_slab_pairs(jnp.tile(gqa_q_norm[j], GQA_KV_HEADS)[None], GQA_KV_HEADS, 1)
            k_gain = _gqa_slab_pairs(jnp.tile(gqa_k_norm[j], GQA_KV_HEADS)[None], GQA_KV_HEADS, 1)
            q_gain = q_gain * (GQA_HEAD_DIM ** -0.5 * LOG2E)
            bound = 1.02 * GQA_HEAD_DIM * jnp.max(jnp.abs(q_gain)) * jnp.max(jnp.abs(k_gain))
            q, k, vt = _gqa_proj(xs, mod, g1, w_qkv, q_gain, k_gain, half_ones, gqa_cos, gqa_sin)
            casts = [(w, i) for w in ffn_w] + [(w, N_CONV_LAYERS - 1) for w in conv_stacks]
            a_lat, a_ctx, *cast_out, mod_next = _attention(
                q, k, vt, jnp.full((BATCH,), bound), casts, (i + 1, 2), mod_operands,
                kv_heads=GQA_KV_HEADS, group=GQA_GROUP, dq=GQA_QK, dv=GQA_HEAD_DIM,
                heads_per_step=1, shared_k=True)
            ffn, conv_bf16[N_CONV_LAYERS - 1] = cast_out[:3], tuple(cast_out[3:])
            mods += list(mod_next.reshape(2, MOD_ROWS, N_MOD, D))
            xs = _attn_tail(a_lat, a_ctx, gqa_wo, j, xs, mod, g2, *ffn)
        else:
            w_dq, w_uq, w_dkv, w_ukv = _mla_weights(mla_w_dq[j], mla_w_uq[j], mla_w_dkv[j], mla_w_ukv[j])
            q_gain = mla_q_norm[j][None] * ((MLA_NOPE + MLA_ROPE) ** -0.5 * LOG2E)
            q, k, vt, qn, kn = _mla_proj(xs, mod, g1, w_dq, q_gain, w_uq, w_dkv,
                                         mla_kv_norm[j][None], w_ukv, pair_ones, mla_cos, mla_sin)
            a_lat, a_ctx, *ffn = _attention(
                q, k, vt, _score_bounds(qn, kn), [(w, i) for w in ffn_w], None, mod_operands,
                kv_heads=MLA_HEADS, group=1, dq=MLA_QK, dv=MLA_V, heads_per_step=4, shared_k=False)
            xs = _attn_tail(a_lat, a_ctx, mla_wo, j, xs, mod, g2, *ffn)

    assert (DEPTH - 1) % 3 == 0
    return xs.reshape(BATCH, SEQ, D)
```

```python
import functools

import jax
import jax.numpy as jnp
from jax import lax
from jax.experimental import pallas as pl
from jax.experimental.pallas import tpu as pltpu

D = 1024
BATCH = 8
SEQ = 2048
DEPTH = 4
GRID_W = 64
CTX_LEN = 256
N_MOD = 6
FFN_HIDDEN = 2816
ROPE_THETA = 10000.0
EPS = 1e-6
GQA_HEAD_DIM = 128
GQA_HEADS = 8
GQA_KV_HEADS = 2
GQA_GROUP = 4
MLA_HEADS = 8
MLA_NOPE = 128
MLA_ROPE = 64
MLA_V = 128
MLA_KV_RANK = 256
MLA_Q_RANK = 768
MLA_QK = 256
GQA_QK = 256
LANES = 128
SUBLANES = 8
VT_ROWS = 128

N_LAT = BATCH * SEQ
N_CTX = BATCH * CTX_LEN
N_ALL = N_LAT + N_CTX
MOD_ROWS = 16
N_CONV_LAYERS = (DEPTH + 2) // 3

TM = 512
TP = 1024
TQ = 1024
HALO = SUBLANES
CONV_COLS = 256
FFN_SPLITS = 2
CAST_CHUNKS = 16
VMEM_LIMIT = 56 * 1024 * 1024
LOG2E = 1.4426950408889634
MAX_BOUND_SHIFT = 50.0

F32 = jnp.float32
BF16 = jnp.bfloat16


def _params(n_axes):
    return pltpu.CompilerParams(
        dimension_semantics=("arbitrary",) * n_axes, vmem_limit_bytes=VMEM_LIMIT)


def _resident(shape, layer=None):
    nd = len(shape)
    if layer is None:
        return pl.BlockSpec(shape, lambda *_: (0,) * nd, pipeline_mode=pl.Buffered(1))
    return pl.BlockSpec((1,) + tuple(shape), lambda *_: (layer,) + (0,) * nd, pipeline_mode=pl.Buffered(1))


def _w(ref):
    return ref[0] if len(ref.shape) == 3 else ref[...]


def _cast_riders(stacks, chunk_of):
    in_specs, out_specs, out_shapes = [], [], []
    for w, layer in stacks:
        _, rows, cols = w.shape
        r = rows // CAST_CHUNKS
        in_specs.append(pl.BlockSpec((1, r, cols), lambda *g, layer=layer: (layer, chunk_of(*g), 0)))
        out_specs.append(pl.BlockSpec((r, cols), lambda *g: (chunk_of(*g), 0)))
        out_shapes.append(jax.ShapeDtypeStruct((rows, cols), BF16))
    return in_specs, out_specs, out_shapes


def _mod_rider(first_layer, n_layers, chunk_of):
    tn = N_MOD * D // CAST_CHUNKS
    assert first_layer % n_layers == 0
    blk = first_layer // n_layers
    in_specs = [pl.BlockSpec((MOD_ROWS, D), lambda *g: (0, 0)),
                pl.BlockSpec((n_layers, D, tn), lambda *g: (blk, 0, chunk_of(*g))),
                pl.BlockSpec((n_layers, 1, tn), lambda *g: (blk, 0, chunk_of(*g)))]
    out_spec = pl.BlockSpec((n_layers, MOD_ROWS, tn), lambda *g: (0, 0, chunk_of(*g)))
    return in_specs, out_spec, jax.ShapeDtypeStruct((n_layers, MOD_ROWS, N_MOD * D), F32)


def _with_riders(body, n_in, n_out, mod_rider=False):
    n_mod_in, n_mod_out = (3, 1) if mod_rider else (0, 0)

    def kern(*refs):
        n_casts = (len(refs) - n_in - n_out - n_mod_in - n_mod_out) // 2
        ins, cast_in = refs[:n_in], refs[n_in:n_in + n_casts]
        mod_in = refs[n_in + n_casts:n_in + n_casts + n_mod_in]
        o0 = n_in + n_casts + n_mod_in
        outs, cast_out = refs[o0:o0 + n_out], refs[o0 + n_out:o0 + n_out + n_casts]
        mod_out = refs[o0 + n_out + n_casts:]
        body(*ins, *outs)
        for src, dst in zip(cast_in, cast_out):
            dst[...] = src[0].astype(dst.dtype)
        if mod_rider:
            _mod_kernel(*mod_in, *mod_out)
    return kern


def _mod_spec(tm):
    return pl.BlockSpec((1, N_MOD, D), lambda i: (jnp.minimum(i * tm // SEQ, BATCH), 0, 0))


def _row_spec(tm, width):
    return pl.BlockSpec((tm, width), lambda i: (i, 0))


def _col_spec(height, tm):
    return pl.BlockSpec((height, tm), lambda i: (0, i))


def _dot(a, b):
    return jnp.dot(a, b, preferred_element_type=F32)


def _modulated_norm(x, gain, shift, scale):
    r = lax.rsqrt(jnp.mean(x * x, axis=-1, keepdims=True) + EPS)
    return (x * r) * (gain * (1.0 + scale)) + shift


def _head_norm(x, gain):
    r = lax.rsqrt(jnp.mean(x * x, axis=-1, keepdims=True) + EPS)
    return (x * r) * gain


def _store_vt(vt_ref, v, n_heads, tokens=slice(None)):
    for j in range(n_heads):
        vt_ref[j * VT_ROWS:(j + 1) * VT_ROWS, tokens] = v[:, j * VT_ROWS:(j + 1) * VT_ROWS].T.astype(BF16)


def _rope(a, b, cos, sin):
    return a * cos - b * sin, a * sin + b * cos


def _lane_mask(lo, hi):
    lane = lax.broadcasted_iota(jnp.int32, (1, LANES), 1)
    return (lane >= lo) & (lane < hi)


def _mod_kernel(cond_ref, w_ref, b_ref, o_ref):
    c = cond_ref[...]
    act = (c * jax.nn.sigmoid(c)).astype(BF16)
    for l in range(w_ref.shape[0]):
        o_ref[l] = _dot(act, w_ref[l].astype(BF16)) + b_ref[l]


def _modulation_tables(cond, ada_w, ada_b, n_layers):
    tn = 1536
    return pl.pallas_call(
        _mod_kernel,
        grid=(n_layers, N_MOD * D // tn),
        in_specs=[
            pl.BlockSpec((MOD_ROWS, D), lambda l, j: (0, 0)),
            pl.BlockSpec((1, D, tn), lambda l, j: (l, 0, j)),
            pl.BlockSpec((1, 1, tn), lambda l, j: (l, 0, j)),
        ],
        out_specs=pl.BlockSpec((1, MOD_ROWS, tn), lambda l, j: (l, 0, j)),
        out_shape=jax.ShapeDtypeStruct((n_layers, MOD_ROWS, N_MOD * D), F32),
        compiler_params=_params(2),
        name="modulation",
    )(cond, ada_w, ada_b)


def _conv_in_compute(x, mod_ref, g_ref, w_ref, b_ref, u_ref):
    m = mod_ref[0]
    h = _modulated_norm(x, g_ref[...], m[0:1], m[1:2]).astype(BF16)
    y = _dot(h, _w(w_ref))
    b_ref[...] = y[:, :D].astype(BF16)
    u_ref[...] = y[:, D:2 * D] * y[:, 2 * D:]


def _conv_in_kernel(x_ref, mod_ref, g_ref, w_ref, b_ref, u_ref):
    _conv_in_compute(x_ref[...], mod_ref, g_ref, w_ref, b_ref, u_ref)


def _conv_in_first_kernel(lat_ref, ctx_ref, mod_ref, g_ref, w_ref, b_ref, u_ref, xs_ref):
    x = jnp.where(pl.program_id(0) < N_LAT // TP, lat_ref[...], ctx_ref[...])
    xs_ref[...] = x
    _conv_in_compute(x, mod_ref, g_ref, w_ref, b_ref, u_ref)


def _conv_in(x, mod, gain, w_in, layer, n_rows, ffn_w, ffn_layer):
    r_in, r_out, r_shape = _cast_riders([(w, ffn_layer) for w in ffn_w], lambda i: jnp.minimum(i, CAST_CHUNKS - 1))
    assert n_rows // TP >= CAST_CHUNKS
    return pl.pallas_call(
        _with_riders(_conv_in_kernel, 4, 2),
        grid=(n_rows // TP,),
        in_specs=[_row_spec(TP, D), _mod_spec(TP), _resident((1, D)), _resident((D, 3 * D), layer)] + r_in,
        out_specs=[_row_spec(TP, D), _row_spec(TP, D)] + r_out,
        out_shape=[jax.ShapeDtypeStruct((n_rows, D), BF16), jax.ShapeDtypeStruct((n_rows, D), F32)] + r_shape,
        compiler_params=_params(1),
        name="conv_in",
    )(x, mod, gain, w_in, *ffn_w)


def _conv_in_first(lat, ctx, mod, gain, w_in, layer, ffn_w, ffn_layer):
    lat_tiles = N_LAT // TP
    r_in, r_out, r_shape = _cast_riders([(w, ffn_layer) for w in ffn_w], lambda i: jnp.minimum(i, CAST_CHUNKS - 1))
    assert N_ALL // TP >= CAST_CHUNKS
    return pl.pallas_call(
        _with_riders(_conv_in_first_kernel, 5, 3),
        grid=(N_ALL // TP,),
        in_specs=[
            pl.BlockSpec((TP, D), lambda i: (jnp.minimum(i, lat_tiles - 1), 0)),
            pl.BlockSpec((TP, D), lambda i: (jnp.maximum(i - lat_tiles, 0), 0)),
            _mod_spec(TP), _resident((1, D)), _resident((D, 3 * D), layer),
        ] + r_in,
        out_specs=[_row_spec(TP, D), _row_spec(TP, D), _row_spec(TP, D)] + r_out,
        out_shape=[jax.ShapeDtypeStruct((N_ALL, D), BF16), jax.ShapeDtypeStruct((N_ALL, D), F32),
                   jax.ShapeDtypeStruct((N_ALL, D), F32)] + r_shape,
        compiler_params=_params(1),
        name="conv_in_first",
    )(lat, ctx, mod, gain, w_in, *ffn_w)


def _conv_gate(b_ref, u_ref, up_ref, un_ref, cw_ref, cols):
    tm = u_ref.shape[0]
    u = u_ref[:, cols]
    local = lax.broadcasted_iota(jnp.int32, (tm, 1), 0)
    row = local + pl.program_id(0) * tm
    in_ctx = row >= N_LAT
    first = ((row & (CTX_LEN - 1)) == 0) & (in_ctx | ((row & (SEQ - 1)) == 0))
    last = ((row & (CTX_LEN - 1)) == CTX_LEN - 1) & (in_ctx | ((row & (SEQ - 1)) == SEQ - 1))
    prev = jnp.where(local == 0, up_ref[HALO - 1:HALO, cols], pltpu.roll(u, 1, axis=0))
    prev = jnp.where(first, 0.0, prev)
    nxt = jnp.where(local == tm - 1, un_ref[0:1, cols], pltpu.roll(u, tm - 1, axis=0))
    nxt = jnp.where(last, 0.0, nxt)
    cw = cw_ref[:, cols]
    z = prev * cw[0:1] + u * cw[1:2] + nxt * cw[2:3]
    return (b_ref[:, cols].astype(F32) * z).astype(BF16)


def _conv_mix(b_ref, u_ref, up_ref, un_ref, cw_ref, wo_ref):
    wo = wo_ref.at[0] if len(wo_ref.shape) == 3 else wo_ref
    acc = None
    for c0 in range(0, D, CONV_COLS):
        cols = slice(c0, c0 + CONV_COLS)
        part = _dot(_conv_gate(b_ref, u_ref, up_ref, un_ref, cw_ref, cols), wo[cols, :])
        acc = part if acc is None else acc + part
    return acc


def _gqa_proj_kernel(x_ref, mod_ref, g_ref, w_ref, qg_ref, kg_ref, ones_ref, cos_ref, sin_ref,
                     q_ref, k_ref, vt_ref):
    m = mod_ref[0]
    ones2 = ones_ref[...]
    wide = 2 * GQA_QK
    kv0 = _lane_mask(0, LANES // 2)

    def project(rows):
        h = _modulated_norm(x_ref[rows, :], g_ref[...], m[0:1], m[1:2]).astype(BF16)
        cos = cos_ref[rows, :]
        sin = sin_ref[rows, :]

        def normed_rotated(y, gain):
            a, b = y[:, :LANES], y[:, LANES:]
            ss = a * a + b * b
            hi = ss.astype(BF16)
            lo = (ss - hi.astype(F32)).astype(BF16)
            tot = _dot(jnp.concatenate([hi, lo], axis=1), ones2)
            r = lax.rsqrt(tot * (1.0 / GQA_HEAD_DIM) + EPS)
            return _rope(a * r * gain[:, :LANES], b * r * gain[:, LANES:], cos, sin)

        for gg in range(GQA_GROUP // 2):
            y = _dot(h, w_ref[:, gg * wide:(gg + 1) * wide])
            for t in range(2):
                g = 2 * gg + t
                o1, o2 = normed_rotated(y[:, t * GQA_QK:(t + 1) * GQA_QK], qg_ref[...])
                for kv, keep in ((0, kv0), (1, jnp.logical_not(kv0))):
                    c0 = (kv * GQA_GROUP + g) * GQA_QK
                    q_ref[rows, c0:c0 + LANES] = jnp.where(keep, o1, 0.0).astype(BF16)
                    q_ref[rows, c0 + LANES:c0 + GQA_QK] = jnp.where(keep, o2, 0.0).astype(BF16)
        y = _dot(h, w_ref[:, GQA_GROUP * GQA_QK:])
        o1, o2 = normed_rotated(y[:, :GQA_QK], kg_ref[...])
        k_ref[rows, :LANES] = o1.astype(BF16)
        k_ref[rows, LANES:] = o2.astype(BF16)
        _store_vt(vt_ref, y[:, GQA_QK:], GQA_KV_HEADS, rows)

    rows = x_ref.shape[0] // 2
    project(slice(0, rows))
    project(slice(rows, 2 * rows))


def _rope_spec(tm):
    per_seq = SEQ // tm
    return pl.BlockSpec(
        (tm, LANES), lambda i: (jnp.where(i < N_LAT // tm, i % per_seq, per_seq), 0))


def _gqa_proj(x, mod, gain, w_qkv, q_gain, k_gain, half_ones, cos, sin):
    kvw = GQA_KV_HEADS * GQA_HEAD_DIM
    qw = GQA_HEADS * GQA_QK
    return pl.pallas_call(
        _gqa_proj_kernel,
        grid=(N_ALL // TP,),
        in_specs=[
            _row_spec(TP, D), _mod_spec(TP), _resident((1, D)),
            _resident((D, D + 2 * kvw)),
            _resident((1, GQA_QK)), _resident((1, GQA_QK)), _resident((2 * LANES, LANES)),
            _rope_spec(TP), _rope_spec(TP),
        ],
        out_specs=[_row_spec(TP, qw), _row_spec(TP, GQA_QK), _col_spec(GQA_KV_HEADS * VT_ROWS, TP)],
        out_shape=[jax.ShapeDtypeStruct((N_ALL, qw), BF16),
                   jax.ShapeDtypeStruct((N_ALL, GQA_QK), BF16),
                   jax.ShapeDtypeStruct((GQA_KV_HEADS * VT_ROWS, N_ALL), BF16)],
        compiler_params=_params(1),
        name="gqa_proj",
    )(x, mod, gain, w_qkv, q_gain, k_gain, half_ones, cos, sin)


def _mla_proj_kernel(x_ref, mod_ref, g_ref, wdq_ref, qg_ref, wuq_ref, wdkv_ref, kvg_ref, wukv_ref,
                     ones_ref, cos_ref, sin_ref, q_ref, k_ref, vt_ref, qn_ref, kn_ref):
    m = mod_ref[0]
    half = MLA_ROPE // 2
    ones2 = ones_ref[...]

    def max_sq_norm(sq_pair):
        sums = _nt_dot(ones2, jnp.concatenate(sq_pair, axis=1).astype(BF16))
        top = jnp.max(sums, axis=1, keepdims=True)
        lane = lax.broadcasted_iota(jnp.int32, (1, 2 * LANES), 1)
        return jnp.where(lane < LANES, top[0:1], top[1:2])

    def project(rows):
        h = _modulated_norm(x_ref[rows, :], g_ref[...], m[0:1], m[1:2]).astype(BF16)
        cos = cos_ref[rows, :]
        sin = sin_ref[rows, :]
        cq = _head_norm(_dot(h, wdq_ref[...]), qg_ref[...]).astype(BF16)
        q = _dot(cq, wuq_ref[...])
        ckv_pe = _dot(h, wdkv_ref[...])
        ckv = _head_norm(ckv_pe[:, :MLA_KV_RANK], kvg_ref[...]).astype(BF16)
        ko1, ko2 = _rope(ckv_pe[:, MLA_KV_RANK:MLA_KV_RANK + LANES], ckv_pe[:, MLA_KV_RANK + LANES:], cos, sin)
        k_pe = jnp.where(_lane_mask(0, 2 * half), ko1, ko2)
        k_pe_sq = k_pe * k_pe
        k_pe = k_pe.astype(BF16)
        kv = _dot(ckv, wukv_ref[...])
        q_norms, k_norms = [], []
        for pp in range(MLA_HEADS // 2):
            p0 = pp * 2 * MLA_QK
            o1, o2 = _rope(q[:, p0 + 2 * MLA_NOPE:p0 + 2 * MLA_NOPE + LANES],
                           q[:, p0 + 2 * MLA_NOPE + LANES:p0 + 2 * MLA_QK], cos, sin)
            q_sq, k_sq = [], []
            for t in range(2):
                j = 2 * pp + t
                c0 = j * MLA_QK
                pe = jnp.where(_lane_mask(t * half, (t + 1) * half), o1,
                               jnp.where(_lane_mask(2 * half + t * half, 2 * half + (t + 1) * half), o2, 0.0))
                q_nope = q[:, p0 + t * MLA_NOPE:p0 + (t + 1) * MLA_NOPE]
                k_nope = kv[:, j * MLA_NOPE:(j + 1) * MLA_NOPE]
                q_ref[rows, c0:c0 + MLA_NOPE] = q_nope.astype(BF16)
                q_ref[rows, c0 + MLA_NOPE:c0 + MLA_QK] = pe.astype(BF16)
                k_ref[rows, c0:c0 + MLA_NOPE] = k_nope.astype(BF16)
                k_ref[rows, c0 + MLA_NOPE:c0 + MLA_QK] = k_pe
                q_sq.append(q_nope * q_nope + pe * pe)
                k_sq.append(k_nope * k_nope + k_pe_sq)
            q_norms.append(max_sq_norm(q_sq))
            k_norms.append(max_sq_norm(k_sq))
        _store_vt(vt_ref, kv[:, MLA_HEADS * MLA_NOPE:], MLA_HEADS, rows)
        return jnp.concatenate(q_norms, axis=1), jnp.concatenate(k_norms, axis=1)

    rows = x_ref.shape[0] // 2
    (qa, ka), (qb, kb) = project(slice(0, rows)), project(slice(rows, 2 * rows))
    qn_ref[0] = jnp.broadcast_to(jnp.maximum(qa, qb), qn_ref.shape[1:])
    kn_ref[0] = jnp.broadcast_to(jnp.maximum(ka, kb), kn_ref.shape[1:])


def _mla_proj(x, mod, gain, w_dq, q_gain, w_uq, w_dkv, kv_gain, w_ukv, pair_ones, cos, sin):
    qkw = MLA_HEADS * MLA_QK
    n_tiles = N_ALL // TP
    norm_spec = pl.BlockSpec((1, SUBLANES, MLA_HEADS * LANES), lambda i: (i, 0, 0))
    norm_shape = jax.ShapeDtypeStruct((n_tiles, SUBLANES, MLA_HEADS * LANES), F32)
    return pl.pallas_call(
        _mla_proj_kernel,
        grid=(N_ALL // TP,),
        in_specs=[
            _row_spec(TP, D), _mod_spec(TP), _resident((1, D)),
            _resident((D, MLA_Q_RANK)), _resident((1, MLA_Q_RANK)), _resident((MLA_Q_RANK, qkw)),
            _resident((D, MLA_KV_RANK + 2 * LANES)), _resident((1, MLA_KV_RANK)),
            _resident((MLA_KV_RANK, MLA_HEADS * (MLA_NOPE + MLA_V))),
            _resident((SUBLANES, 2 * LANES)),
            _rope_spec(TP), _rope_spec(TP),
        ],
        out_specs=[_row_spec(TP, qkw), _row_spec(TP, qkw), _col_spec(MLA_HEADS * VT_ROWS, TP),
                   norm_spec, norm_spec],
        out_shape=[jax.ShapeDtypeStruct((N_ALL, qkw), BF16),
                   jax.ShapeDtypeStruct((N_ALL, qkw), BF16),
                   jax.ShapeDtypeStruct((MLA_HEADS * VT_ROWS, N_ALL), BF16),
                   norm_shape, norm_shape],
        compiler_params=_params(1),
        name="mla_proj",
    )(x, mod, gain, w_dq, q_gain, w_uq, w_dkv, kv_gain, w_ukv, pair_ones, cos, sin)


def _score_bounds(qn, kn):
    qn = qn[:, 0, ::LANES]
    kn = kn[:, 0, ::LANES]
    per_seq = SEQ // TP
    q_lat = jnp.max(qn[:N_LAT // TP].reshape(BATCH, per_seq, -1), axis=1)
    k_lat = jnp.max(kn[:N_LAT // TP].reshape(BATCH, per_seq, -1), axis=1)
    ctx_tile = N_LAT // TP + (jnp.arange(BATCH) * CTX_LEN) // TP
    k_all = jnp.maximum(k_lat, kn[ctx_tile])
    return 1.02 * jnp.sqrt(jnp.max(q_lat * k_all, axis=1))


def _nt_dot(a, b):
    return lax.dot_general(a, b, (((1,), (1,)), ((), ())), preferred_element_type=F32)


def _attn_scores(q, k_parts):
    return [_nt_dot(k, q) for k in k_parts]


def _attn_kernel(bound_ref, q_ref, qc_ref, kl_ref, kc_ref, vtl_ref, vtc_ref, o_ref, oc_ref,
                 *, heads, group, dq, dv, shared_k):
    n = heads * group

    def k_cols(qh):
        kv = 0 if shared_k else qh // group
        return slice(kv * dq, (kv + 1) * dq)

    def v_rows(qh):
        return slice(qh // group * VT_ROWS, (qh // group + 1) * VT_ROWS)

    def run(queries, out_ref, k_refs, vt_refs, shift=None):
        def scores(qh):
            return _attn_scores(queries[:, qh * dq:(qh + 1) * dq], [k[:, k_cols(qh)] for k in k_refs])

        ahead = 2 if shift is None else 1
        pending = [scores(qh) for qh in range(min(ahead, n))]
        for qh in range(n):
            cur = pending.pop(0)
            if shift is None:
                mx = functools.reduce(jnp.maximum, [jnp.max(s, axis=0, keepdims=True) for s in cur])
            else:
                mx = shift
            probs = [jnp.exp2(s - mx) for s in cur]
            den = sum(jnp.sum(p, axis=0, keepdims=True) for p in probs)
            probs = [p.astype(BF16) for p in probs]
            if qh + ahead < n:
                pending.append(scores(qh + ahead))
            acc = sum(_dot(vt[v_rows(qh), :], p) for vt, p in zip(vt_refs, probs))
            out_ref[:, qh * dv:(qh + 1) * dv] = (acc / den).T.astype(BF16)

    lat = (q_ref, o_ref, [kl_ref, kc_ref], [vtl_ref, vtc_ref])
    bound = bound_ref[0, pl.program_id(0)]

    @pl.when(bound <= MAX_BOUND_SHIFT)
    def _():
        run(*lat, shift=bound)

    @pl.when(jnp.logical_not(bound <= MAX_BOUND_SHIFT))
    def _():
        run(*lat)

    @pl.when(pl.program_id(2) == 0)
    def _():
        run(qc_ref, oc_ref, [kc_ref], [vtc_ref])


def _attention(q, k, vt, score_bounds, casts, mod_layers, mod_operands, *, kv_heads, group, dq, dv,
               heads_per_step, shared_k):
    hp = heads_per_step
    lat_tiles = SEQ // TQ
    ctx_block0 = N_LAT // CTX_LEN
    kw = dq if shared_k else hp * dq
    k_col = (lambda h: 0) if shared_k else (lambda h: h)
    width = kv_heads * group * dv
    h_steps = kv_heads // hp
    n_steps = BATCH * h_steps * lat_tiles
    assert n_steps >= CAST_CHUNKS
    def chunk_of(b, h, t):
        return ((b * h_steps + h) * lat_tiles + t) * CAST_CHUNKS // n_steps

    r_in, r_out, r_shape = _cast_riders(casts, chunk_of)
    rider_operands = [w for w, _ in casts]
    if mod_layers is not None:
        m_in, m_out, m_shape = _mod_rider(*mod_layers, chunk_of)
        r_in, r_out, r_shape = r_in + m_in, r_out + [m_out], r_shape + [m_shape]
        rider_operands += list(mod_operands)
    body = functools.partial(_attn_kernel, heads=hp, group=group, dq=dq, dv=dv, shared_k=shared_k)
    return pl.pallas_call(
        _with_riders(body, 7, 2, mod_rider=mod_layers is not None),
        grid=(BATCH, h_steps, lat_tiles),
        in_specs=[
            pl.BlockSpec(memory_space=pltpu.SMEM),
            pl.BlockSpec((TQ, hp * group * dq), lambda b, h, t: (b * lat_tiles + t, h)),
            pl.BlockSpec((CTX_LEN, hp * group * dq), lambda b, h, t: (ctx_block0 + b, h)),
            pl.BlockSpec((SEQ, kw), lambda b, h, t: (b, k_col(h))),
            pl.BlockSpec((CTX_LEN, kw), lambda b, h, t: (ctx_block0 + b, k_col(h))),
            pl.BlockSpec((hp * VT_ROWS, SEQ), lambda b, h, t: (h, b)),
            pl.BlockSpec((hp * VT_ROWS, CTX_LEN), lambda b, h, t: (h, ctx_block0 + b)),
        ] + r_in,
        out_specs=[pl.BlockSpec((TQ, hp * group * dv), lambda b, h, t: (b * lat_tiles + t, h)),
                   pl.BlockSpec((CTX_LEN, hp * group * dv), lambda b, h, t: (b, h))] + r_out,
        out_shape=[jax.ShapeDtypeStruct((N_LAT, width), BF16), jax.ShapeDtypeStruct((N_CTX, width), BF16)] + r_shape,
        compiler_params=_params(3),
        name="attention",
    )(score_bounds.reshape(1, BATCH).astype(F32), q, q, k, k, vt, vt, *rider_operands)


def _ffn_tail(x1, m, g2_ref, w1_ref, w3_ref, w2_ref):
    rows = x1.shape[0] // FFN_SPLITS
    parts = [x1[i * rows:(i + 1) * rows] for i in range(FFN_SPLITS)]

    def up(x):
        h = _modulated_norm(x, g2_ref[...], m[3:4], m[4:5]).astype(BF16)
        return _dot(h, _w(w1_ref)), _dot(h, _w(w3_ref))

    def down(x, ab):
        a, b = ab
        gated = (a * jax.nn.sigmoid(a) * b).astype(BF16)
        return x + m[5:6] * _dot(gated, _w(w2_ref))

    outs = []
    nxt = up(parts[0])
    for i, x in enumerate(parts):
        cur = nxt
        if i + 1 < FFN_SPLITS:
            nxt = up(parts[i + 1])
        outs.append(down(x, cur))
    return jnp.concatenate(outs, axis=0)


def _attn_tail_kernel(al_ref, ac_ref, wo_ref, x_ref, mod_ref, g2_ref, w1_ref, w3_ref, w2_ref, o_ref):
    m = mod_ref[0]
    a = jnp.where(pl.program_id(0) < N_LAT // TM, al_ref[...], ac_ref[...])
    x1 = x_ref[...] + m[2:3] * _dot(a, _w(wo_ref))
    o_ref[...] = _ffn_tail(x1, m, g2_ref, w1_ref, w3_ref, w2_ref)


def _conv_tail_kernel(b_ref, u_ref, up_ref, un_ref, cw_ref, wo_ref, x_ref, mod_ref,
                      g2_ref, w1_ref, w3_ref, w2_ref, fg_ref, o_ref, *, final):
    m = mod_ref[0]
    x1 = x_ref[...] + m[2:3] * _conv_mix(b_ref, u_ref, up_ref, un_ref, cw_ref, wo_ref)
    x2 = _ffn_tail(x1, m, g2_ref, w1_ref, w3_ref, w2_ref)
    o_ref[...] = _head_norm(x2, fg_ref[...]) if final else x2


def _ffn_specs():
    return [_mod_spec(TM), _resident((1, D)), _resident((D, FFN_HIDDEN)),
            _resident((D, FFN_HIDDEN)), _resident((FFN_HIDDEN, D))]


def _attn_tail(a_lat, a_ctx, w_o, mixer_layer, x, mod, g2, w1, w3, w2):
    lat_tiles = N_LAT // TM
    return pl.pallas_call(
        _attn_tail_kernel,
        grid=(N_ALL // TM,),
        in_specs=[
            pl.BlockSpec((TM, D), lambda i: (jnp.minimum(i, lat_tiles - 1), 0)),
            pl.BlockSpec((TM, D), lambda i: (jnp.maximum(i - lat_tiles, 0), 0)),
            _resident((D, D), mixer_layer), _row_spec(TM, D),
        ] + _ffn_specs(),
        out_specs=_row_spec(TM, D),
        out_shape=jax.ShapeDtypeStruct((N_ALL, D), F32),
        compiler_params=_params(1),
        name="attn_tail",
    )(a_lat, a_ctx, w_o, x, mod, g2, w1, w3, w2)


def _conv_tail(b, u, conv_w, w_out, mixer_layer, x, mod, g2, w1, w3, w2, final_gain, n_rows, final):
    per = TM // HALO
    n_halo = n_rows // HALO
    return pl.pallas_call(
        functools.partial(_conv_tail_kernel, final=final),
        grid=(n_rows // TM,),
        in_specs=[
            _row_spec(TM, D), _row_spec(TM, D),
            pl.BlockSpec((HALO, D), lambda i: (jnp.maximum(i * per - 1, 0), 0)),
            pl.BlockSpec((HALO, D), lambda i: (jnp.minimum((i + 1) * per, n_halo - 1), 0)),
            _resident((3, D)), _resident((D, D), mixer_layer), _row_spec(TM, D),
        ] + _ffn_specs() + [_resident((1, D))],
        out_specs=_row_spec(TM, D),
        out_shape=jax.ShapeDtypeStruct((n_rows, D), F32),
        compiler_params=_params(1),
        name="conv_tail",
    )(b, u, u, u, conv_w, w_out, x, mod, g2, w1, w3, w2, final_gain)


def _axial_angles(rot_dim):
    n = rot_dim // 4
    rows = jnp.repeat(jnp.arange(SEQ // GRID_W, dtype=F32), GRID_W)
    cols = jnp.tile(jnp.arange(GRID_W, dtype=F32), SEQ // GRID_W)
    freqs = ROPE_THETA ** (-jnp.arange(n, dtype=F32) / n)
    return jnp.concatenate([rows[:, None] * freqs, cols[:, None] * freqs], axis=-1)


def _rope_tables(rot_dim):
    ang = _axial_angles(rot_dim)
    copies = LANES // (rot_dim // 2)
    cos_t = jnp.concatenate([jnp.tile(jnp.cos(ang), (1, copies)), jnp.ones((TP, LANES), F32)], axis=0)
    sin_t = jnp.concatenate([jnp.tile(jnp.sin(ang), (1, copies)), jnp.zeros((TP, LANES), F32)], axis=0)
    return cos_t, sin_t


def _gqa_slab_pairs(w, n_kv, n_group):
    half = GQA_HEAD_DIM // 2
    w = w.reshape(w.shape[0], n_kv, n_group, 2, half)
    return jnp.transpose(w, (0, 2, 3, 1, 4)).reshape(w.shape[0], n_group * GQA_QK)


def _mla_weights(w_dq, w_uq, w_dkv, w_ukv):
    half = MLA_ROPE // 2
    qk = MLA_NOPE + MLA_ROPE
    w_dq, w_uq, w_dkv, w_ukv = (w.astype(BF16) for w in (w_dq, w_uq, w_dkv, w_ukv))
    uq = w_uq.reshape(MLA_Q_RANK, MLA_HEADS // 2, 2, qk)
    nope = uq[..., :MLA_NOPE].reshape(MLA_Q_RANK, MLA_HEADS // 2, 2 * MLA_NOPE)
    x1 = uq[..., MLA_NOPE:MLA_NOPE + half].reshape(MLA_Q_RANK, MLA_HEADS // 2, 2 * half)
    x2 = uq[..., MLA_NOPE + half:].reshape(MLA_Q_RANK, MLA_HEADS // 2, 2 * half)
    uq = jnp.concatenate([nope, x1, x1, x2, x2], axis=-1).reshape(MLA_Q_RANK, MLA_HEADS * MLA_QK)
    k1 = w_dkv[:, MLA_KV_RANK:MLA_KV_RANK + half]
    k2 = w_dkv[:, MLA_KV_RANK + half:]
    dkv = jnp.concatenate([w_dkv[:, :MLA_KV_RANK]] + [k1] * 4 + [k2] * 4, axis=-1)
    ukv = w_ukv.reshape(MLA_KV_RANK, MLA_HEADS, MLA_NOPE + MLA_V)
    ukv = jnp.concatenate([ukv[:, :, :MLA_NOPE].reshape(MLA_KV_RANK, -1),
                           ukv[:, :, MLA_NOPE:].reshape(MLA_KV_RANK, -1)], axis=-1)
    return w_dq, uq, dkv, ukv


def kernel(x, c, ctx, c_ctx, ada_w, ada_b, norm1_g, norm2_g, ffn_w1, ffn_w3, ffn_w2, conv_w_in, conv_w, conv_w_out, gqa_wq, gqa_wk, gqa_wv, gqa_q_norm, gqa_k_norm, gqa_wo, mla_w_dq, mla_q_norm, mla_w_uq, mla_w_dkv, mla_kv_norm, mla_w_ukv, mla_wo, final_g):
    assert x.shape == (BATCH, SEQ, D) and ctx.shape == (BATCH, CTX_LEN, D)
    cond = jnp.concatenate(
        [c, c_ctx[None], jnp.zeros((MOD_ROWS - BATCH - 1, D), F32)], axis=0)
    mod_operands = (cond, ada_w, ada_b.reshape(DEPTH, 1, N_MOD * D))
    mods = list(_modulation_tables(*mod_operands, 2).reshape(2, MOD_ROWS, N_MOD, D))

    gqa_cos, gqa_sin = _rope_tables(GQA_HEAD_DIM)
    lane_half = jnp.arange(LANES) // (LANES // 2)
    half_ones = jnp.tile((lane_half[:, None] == lane_half[None, :]).astype(BF16), (2, 1))
    lane_head = jnp.arange(2 * LANES) // LANES
    pair_ones = (jnp.arange(SUBLANES)[:, None] == lane_head[None, :]).astype(BF16)
    mla_cos, mla_sin = _rope_tables(MLA_ROPE)

    ffn_w = (ffn_w1, ffn_w3, ffn_w2)
    conv_stacks = (conv_w_in, conv_w_out)
    conv_bf16 = {0: (conv_w_in[0].astype(BF16), conv_w_out[0].astype(BF16))}
    gqa_wo = gqa_wo.astype(BF16)
    mla_wo = mla_wo.astype(BF16)

    xs = None
    for i in range(DEPTH):
        kind, j = i % 3, i // 3
        mod = mods[i]
        last = i == DEPTH - 1
        n_rows = N_LAT if last else N_ALL
        g1 = norm1_g[i][None]
        g2 = norm2_g[i][None]
        if kind == 0:
            w_in, w_out = conv_bf16[j]
            if i == 0:
                b, u, xs, *ffn = _conv_in_first(x.reshape(N_LAT, D), ctx.reshape(N_CTX, D), mod, g1,
                                                w_in, None, ffn_w, i)
            else:
                b, u, *ffn = _conv_in(xs, mod, g1, w_in, None, n_rows, ffn_w, i)
            xs = _conv_tail(b, u, conv_w[j], w_out, None, xs, mod, g2, *ffn, final_g[None], n_rows, last)
        elif kind == 1:
            w_qkv = jnp.concatenate([_gqa_slab_pairs(gqa_wq[j].astype(BF16), GQA_KV_HEADS, GQA_GROUP),
                                     _gqa_slab_pairs(gqa_wk[j].astype(BF16), GQA_KV_HEADS, 1),
                                     gqa_wv[j].astype(BF16)], axis=-1)
            q_gain = _gqa_slab_pairs(jnp.tile(gqa_q_norm[j], GQA_KV_HEADS)[None], GQA_KV_HEADS, 1)
            k_gain = _gqa_slab_pairs(jnp.tile(gqa_k_norm[j], GQA_KV_HEADS)[None], GQA_KV_HEADS, 1)
            q_gain = q_gain * (GQA_HEAD_DIM ** -0.5 * LOG2E)
            bound = 1.02 * GQA_HEAD_DIM * jnp.max(jnp.abs(q_gain)) * jnp.max(jnp.abs(k_gain))
            q, k, vt = _gqa_proj(xs, mod, g1, w_qkv, q_gain, k_gain, half_ones, gqa_cos, gqa_sin)
            casts = [(w, i) for w in ffn_w] + [(w, N_CONV_LAYERS - 1) for w in conv_stacks]
            a_lat, a_ctx, *cast_out, mod_next = _attention(
                q, k, vt, jnp.full((BATCH,), bound), casts, (i + 1, 2), mod_operands,
                kv_heads=GQA_KV_HEADS, group=GQA_GROUP, dq=GQA_QK, dv=GQA_HEAD_DIM,
                heads_per_step=1, shared_k=True)
            ffn, conv_bf16[N_CONV_LAYERS - 1] = cast_out[:3], tuple(cast_out[3:])
            mods += list(mod_next.reshape(2, MOD_ROWS, N_MOD, D))
            xs = _attn_tail(a_lat, a_ctx, gqa_wo, j, xs, mod, g2, *ffn)
        else:
            w_dq, w_uq, w_dkv, w_ukv = _mla_weights(mla_w_dq[j], mla_w_uq[j], mla_w_dkv[j], mla_w_ukv[j])
            q_gain = mla_q_norm[j][None] * ((MLA_NOPE + MLA_ROPE) ** -0.5 * LOG2E)
            q, k, vt, qn, kn = _mla_proj(xs, mod, g1, w_dq, q_gain, w_uq, w_dkv,
                                         mla_kv_norm[j][None], w_ukv, pair_ones, mla_cos, mla_sin)
            a_lat, a_ctx, *ffn = _attention(
                q, k, vt, _score_bounds(qn, kn), [(w, i) for w in ffn_w], None, mod_operands,
                kv_heads=MLA_HEADS, group=1, dq=MLA_QK, dv=MLA_V, heads_per_step=4, shared_k=False)
            xs = _attn_tail(a_lat, a_ctx, mla_wo, j, xs, mod, g2, *ffn)

    assert (DEPTH - 1) % 3 == 0
    return xs.reshape(BATCH, SEQ, D)
```

```python
import functools

import jax
import jax.numpy as jnp
from jax import lax
from jax.experimental import pallas as pl
from jax.experimental.pallas import tpu as pltpu

D = 1024
BATCH = 8
SEQ = 2048
DEPTH = 4
GRID_W = 64
CTX_LEN = 256
N_MOD = 6
FFN_HIDDEN = 2816
ROPE_THETA = 10000.0
EPS = 1e-6
GQA_HEAD_DIM = 128
GQA_HEADS = 8
GQA_KV_HEADS = 2
GQA_GROUP = 4
MLA_HEADS = 8
MLA_NOPE = 128
MLA_ROPE = 64
MLA_V = 128
MLA_KV_RANK = 256
MLA_Q_RANK = 768
MLA_QK = 256
GQA_QK = 256
LANES = 128
SUBLANES = 8
VT_ROWS = 128

N_LAT = BATCH * SEQ
N_CTX = BATCH * CTX_LEN
N_ALL = N_LAT + N_CTX
MOD_ROWS = 16
N_CONV_LAYERS = (DEPTH + 2) // 3

TM = 512
TP = 1024
TQ = 1024
HALO = SUBLANES
CONV_COLS = 256
FFN_SPLITS = 2
CAST_CHUNKS = 16
VMEM_LIMIT = 56 * 1024 * 1024
LOG2E = 1.4426950408889634
MAX_BOUND_SHIFT = 50.0

F32 = jnp.float32
BF16 = jnp.bfloat16


def _params(n_axes):
    return pltpu.CompilerParams(
        dimension_semantics=("arbitrary",) * n_axes, vmem_limit_bytes=VMEM_LIMIT)


def _resident(shape, layer=None):
    nd = len(shape)
    if layer is None:
        return pl.BlockSpec(shape, lambda *_: (0,) * nd, pipeline_mode=pl.Buffered(1))
    return pl.BlockSpec((1,) + tuple(shape), lambda *_: (layer,) + (0,) * nd, pipeline_mode=pl.Buffered(1))


def _w(ref):
    return ref[0] if len(ref.shape) == 3 else ref[...]


def _cast_riders(stacks, chunk_of):
    in_specs, out_specs, out_shapes = [], [], []
    for w, layer in stacks:
        _, rows, cols = w.shape
        r = rows // CAST_CHUNKS
        in_specs.append(pl.BlockSpec((1, r, cols), lambda *g, layer=layer: (layer, chunk_of(*g), 0)))
        out_specs.append(pl.BlockSpec((r, cols), lambda *g: (chunk_of(*g), 0)))
        out_shapes.append(jax.ShapeDtypeStruct((rows, cols), BF16))
    return in_specs, out_specs, out_shapes


def _mod_rider(first_layer, n_layers, chunk_of):
    tn = N_MOD * D // CAST_CHUNKS
    assert first_layer % n_layers == 0
    blk = first_layer // n_layers
    in_specs = [pl.BlockSpec((MOD_ROWS, D), lambda *g: (0, 0)),
                pl.BlockSpec((n_layers, D, tn), lambda *g: (blk, 0, chunk_of(*g))),
                pl.BlockSpec((n_layers, 1, tn), lambda *g: (blk, 0, chunk_of(*g)))]
    out_spec = pl.BlockSpec((n_layers, MOD_ROWS, tn), lambda *g: (0, 0, chunk_of(*g)))
    return in_specs, out_spec, jax.ShapeDtypeStruct((n_layers, MOD_ROWS, N_MOD * D), F32)


def _with_riders(body, n_in, n_out, mod_rider=False):
    n_mod_in, n_mod_out = (3, 1) if mod_rider else (0, 0)

    def kern(*refs):
        n_casts = (len(refs) - n_in - n_out - n_mod_in - n_mod_out) // 2
        ins, cast_in = refs[:n_in], refs[n_in:n_in + n_casts]
        mod_in = refs[n_in + n_casts:n_in + n_casts + n_mod_in]
        o0 = n_in + n_casts + n_mod_in
        outs, cast_out = refs[o0:o0 + n_out], refs[o0 + n_out:o0 + n_out + n_casts]
        mod_out = refs[o0 + n_out + n_casts:]
        body(*ins, *outs)
        for src, dst in zip(cast_in, cast_out):
            dst[...] = src[0].astype(dst.dtype)
        if mod_rider:
            _mod_kernel(*mod_in, *mod_out)
    return kern


def _mod_spec(tm):
    return pl.BlockSpec((1, N_MOD, D), lambda i: (jnp.minimum(i * tm // SEQ, BATCH), 0, 0))


def _row_spec(tm, width):
    return pl.BlockSpec((tm, width), lambda i: (i, 0))


def _col_spec(height, tm):
    return pl.BlockSpec((height, tm), lambda i: (0, i))


def _dot(a, b):
    return jnp.dot(a, b, preferred_element_type=F32)


def _modulated_norm(x, gain, shift, scale):
    r = lax.rsqrt(jnp.mean(x * x, axis=-1, keepdims=True) + EPS)
    return (x * r) * (gain * (1.0 + scale)) + shift


def _head_norm(x, gain):
    r = lax.rsqrt(jnp.mean(x * x, axis=-1, keepdims=True) + EPS)
    return (x * r) * gain


def _store_vt(vt_ref, v, n_heads, tokens=slice(None)):
    for j in range(n_heads):
        vt_ref[j * VT_ROWS:(j + 1) * VT_ROWS, tokens] = v[:, j * VT_ROWS:(j + 1) * VT_ROWS].T.astype(BF16)


def _rope(a, b, cos, sin):
    return a * cos - b * sin, a * sin + b * cos


def _lane_mask(lo, hi):
    lane = lax.broadcasted_iota(jnp.int32, (1, LANES), 1)
    return (lane >= lo) & (lane < hi)


def _mod_kernel(cond_ref, w_ref, b_ref, o_ref):
    c = cond_ref[...]
    act = (c * jax.nn.sigmoid(c)).astype(BF16)
    for l in range(w_ref.shape[0]):
        o_ref[l] = _dot(act, w_ref[l].astype(BF16)) + b_ref[l]


def _modulation_tables(cond, ada_w, ada_b, n_layers):
    tn = 1536
    return pl.pallas_call(
        _mod_kernel,
        grid=(n_layers, N_MOD * D // tn),
        in_specs=[
            pl.BlockSpec((MOD_ROWS, D), lambda l, j: (0, 0)),
            pl.BlockSpec((1, D, tn), lambda l, j: (l, 0, j)),
            pl.BlockSpec((1, 1, tn), lambda l, j: (l, 0, j)),
        ],
        out_specs=pl.BlockSpec((1, MOD_ROWS, tn), lambda l, j: (l, 0, j)),
        out_shape=jax.ShapeDtypeStruct((n_layers, MOD_ROWS, N_MOD * D), F32),
        compiler_params=_params(2),
        name="modulation",
    )(cond, ada_w, ada_b)


def _conv_in_compute(x, mod_ref, g_ref, w_ref, b_ref, u_ref):
    m = mod_ref[0]
    h = _modulated_norm(x, g_ref[...], m[0:1], m[1:2]).astype(BF16)
    y = _dot(h, _w(w_ref))
    b_ref[...] = y[:, :D].astype(BF16)
    u_ref[...] = y[:, D:2 * D] * y[:, 2 * D:]


def _conv_in_kernel(x_ref, mod_ref, g_ref, w_ref, b_ref, u_ref):
    _conv_in_compute(x_ref[...], mod_ref, g_ref, w_ref, b_ref, u_ref)


def _conv_in_first_kernel(lat_ref, ctx_ref, mod_ref, g_ref, w_ref, b_ref, u_ref, xs_ref):
    x = jnp.where(pl.program_id(0) < N_LAT // TP, lat_ref[...], ctx_ref[...])
    xs_ref[...] = x
    _conv_in_compute(x, mod_ref, g_ref, w_ref, b_ref, u_ref)


def _conv_in(x, mod, gain, w_in, layer, n_rows, ffn_w, ffn_layer):
    r_in, r_out, r_shape = _cast_riders([(w, ffn_layer) for w in ffn_w], lambda i: jnp.minimum(i, CAST_CHUNKS - 1))
    assert n_rows // TP >= CAST_CHUNKS
    return pl.pallas_call(
        _with_riders(_conv_in_kernel, 4, 2),
        grid=(n_rows // TP,),
        in_specs=[_row_spec(TP, D), _mod_spec(TP), _resident((1, D)), _resident((D, 3 * D), layer)] + r_in,
        out_specs=[_row_spec(TP, D), _row_spec(TP, D)] + r_out,
        out_shape=[jax.ShapeDtypeStruct((n_rows, D), BF16), jax.ShapeDtypeStruct((n_rows, D), F32)] + r_shape,
        compiler_params=_params(1),
        name="conv_in",
    )(x, mod, gain, w_in, *ffn_w)


def _conv_in_first(lat, ctx, mod, gain, w_in, layer, ffn_w, ffn_layer):
    lat_tiles = N_LAT // TP
    r_in, r_out, r_shape = _cast_riders([(w, ffn_layer) for w in ffn_w], lambda i: jnp.minimum(i, CAST_CHUNKS - 1))
    assert N_ALL // TP >= CAST_CHUNKS
    return pl.pallas_call(
        _with_riders(_conv_in_first_kernel, 5, 3),
        grid=(N_ALL // TP,),
        in_specs=[
            pl.BlockSpec((TP, D), lambda i: (jnp.minimum(i, lat_tiles - 1), 0)),
            pl.BlockSpec((TP, D), lambda i: (jnp.maximum(i - lat_tiles, 0), 0)),
            _mod_spec(TP), _resident((1, D)), _resident((D, 3 * D), layer),
        ] + r_in,
        out_specs=[_row_spec(TP, D), _row_spec(TP, D), _row_spec(TP, D)] + r_out,
        out_shape=[jax.ShapeDtypeStruct((N_ALL, D), BF16), jax.ShapeDtypeStruct((N_ALL, D), F32),
                   jax.ShapeDtypeStruct((N_ALL, D), F32)] + r_shape,
        compiler_params=_params(1),
        name="conv_in_first",
    )(lat, ctx, mod, gain, w_in, *ffn_w)


def _conv_gate_exact(b_ref, u_ref, up_ref, un_ref, cw_ref, r0, r1, cols):
    tm = u_ref.shape[0]
    n = r1 - r0
    u = u_ref[r0:r1, cols]
    before = up_ref[HALO - 1:HALO, cols] if r0 == 0 else u_ref[r0 - 1:r0, cols]
    after = un_ref[0:1, cols] if r1 == tm else u_ref[r1:r1 + 1, cols]
    local = lax.broadcasted_iota(jnp.int32, (n, 1), 0)
    row = local + (pl.program_id(0) * tm + r0)
    in_ctx = row >= N_LAT
    first = ((row & (CTX_LEN - 1)) == 0) & (in_ctx | ((row & (SEQ - 1)) == 0))
    last = ((row & (CTX_LEN - 1)) == CTX_LEN - 1) & (in_ctx | ((row & (SEQ - 1)) == SEQ - 1))
    prev = jnp.where(local == 0, before, pltpu.roll(u, 1, axis=0))
    prev = jnp.where(first, 0.0, prev)
    nxt = jnp.where(local == n - 1, after, pltpu.roll(u, n - 1, axis=0))
    nxt = jnp.where(last, 0.0, nxt)
    cw = cw_ref[:, cols]
    z = prev * cw[0:1] + u * cw[1:2] + nxt * cw[2:3]
    return (b_ref[r0:r1, cols].astype(F32) * z).astype(BF16)


def _conv_mix(b_ref, u_ref, up_ref, un_ref, cw_ref, wo_ref, bz_ref):
    tm = u_ref.shape[0]
    group = 2 * HALO
    edges = [(0, group), (tm - group, tm)]
    edges += [(r - group, r + group) for r in range(CTX_LEN, tm, CTX_LEN)]
    wo = wo_ref.at[0] if len(wo_ref.shape) == 3 else wo_ref
    acc = None
    for c0 in range(0, D, CONV_COLS):
        cols = slice(c0, c0 + CONV_COLS)
        u = u_ref[:, cols]
        cw = cw_ref[:, cols]
        z = pltpu.roll(u, 1, axis=0) * cw[0:1] + u * cw[1:2] + pltpu.roll(u, tm - 1, axis=0) * cw[2:3]
        bz_ref[:, cols] = (b_ref[:, cols].astype(F32) * z).astype(BF16)
        for r0, r1 in edges:
            bz_ref[r0:r1, cols] = _conv_gate_exact(b_ref, u_ref, up_ref, un_ref, cw_ref, r0, r1, cols)
        part = _dot(bz_ref[:, cols], wo[cols, :])
        acc = part if acc is None else acc + part
    return acc


def _gqa_proj_kernel(x_ref, mod_ref, g_ref, w_ref, qg_ref, kg_ref, ones_ref, cos_ref, sin_ref,
                     q_ref, k_ref, vt_ref):
    m = mod_ref[0]
    ones2 = ones_ref[...]
    wide = 2 * GQA_QK
    kv0 = _lane_mask(0, LANES // 2)

    def project(rows):
        h = _modulated_norm(x_ref[rows, :], g_ref[...], m[0:1], m[1:2]).astype(BF16)
        cos = cos_ref[rows, :]
        sin = sin_ref[rows, :]

        def normed_rotated(y, gain):
            a, b = y[:, :LANES], y[:, LANES:]
            ss = a * a + b * b
            hi = ss.astype(BF16)
            lo = (ss - hi.astype(F32)).astype(BF16)
            tot = _dot(jnp.concatenate([hi, lo], axis=1), ones2)
            r = lax.rsqrt(tot * (1.0 / GQA_HEAD_DIM) + EPS)
            return _rope(a * r * gain[:, :LANES], b * r * gain[:, LANES:], cos, sin)

        for gg in range(GQA_GROUP // 2):
            y = _dot(h, w_ref[:, gg * wide:(gg + 1) * wide])
            for t in range(2):
                g = 2 * gg + t
                o1, o2 = normed_rotated(y[:, t * GQA_QK:(t + 1) * GQA_QK], qg_ref[...])
                for kv, keep in ((0, kv0), (1, jnp.logical_not(kv0))):
                    c0 = (kv * GQA_GROUP + g) * GQA_QK
                    q_ref[rows, c0:c0 + LANES] = jnp.where(keep, o1, 0.0).astype(BF16)
                    q_ref[rows, c0 + LANES:c0 + GQA_QK] = jnp.where(keep, o2, 0.0).astype(BF16)
        y = _dot(h, w_ref[:, GQA_GROUP * GQA_QK:])
        o1, o2 = normed_rotated(y[:, :GQA_QK], kg_ref[...])
        k_ref[rows, :LANES] = o1.astype(BF16)
        k_ref[rows, LANES:] = o2.astype(BF16)
        _store_vt(vt_ref, y[:, GQA_QK:], GQA_KV_HEADS, rows)

    rows = x_ref.shape[0] // 2
    project(slice(0, rows))
    project(slice(rows, 2 * rows))


def _rope_spec(tm):
    per_seq = SEQ // tm
    return pl.BlockSpec(
        (tm, LANES), lambda i: (jnp.where(i < N_LAT // tm, i % per_seq, per_seq), 0))


def _gqa_proj(x, mod, gain, w_qkv, q_gain, k_gain, half_ones, cos, sin):
    kvw = GQA_KV_HEADS * GQA_HEAD_DIM
    qw = GQA_HEADS * GQA_QK
    return pl.pallas_call(
        _gqa_proj_kernel,
        grid=(N_ALL // TP,),
        in_specs=[
            _row_spec(TP, D), _mod_spec(TP), _resident((1, D)),
            _resident((D, D + 2 * kvw)),
            _resident((1, GQA_QK)), _resident((1, GQA_QK)), _resident((2 * LANES, LANES)),
            _rope_spec(TP), _rope_spec(TP),
        ],
        out_specs=[_row_spec(TP, qw), _row_spec(TP, GQA_QK), _col_spec(GQA_KV_HEADS * VT_ROWS, TP)],
        out_shape=[jax.ShapeDtypeStruct((N_ALL, qw), BF16),
                   jax.ShapeDtypeStruct((N_ALL, GQA_QK), BF16),
                   jax.ShapeDtypeStruct((GQA_KV_HEADS * VT_ROWS, N_ALL), BF16)],
        compiler_params=_params(1),
        name="gqa_proj",
    )(x, mod, gain, w_qkv, q_gain, k_gain, half_ones, cos, sin)


def _mla_proj_kernel(x_ref, mod_ref, g_ref, wdq_ref, qg_ref, wuq_ref, wdkv_ref, kvg_ref, wukv_ref,
                     ones_ref, cos_ref, sin_ref, q_ref, k_ref, vt_ref, qn_ref, kn_ref):
    m = mod_ref[0]
    half = MLA_ROPE // 2
    ones2 = ones_ref[...]

    def max_sq_norm(sq_pair):
        sums = _nt_dot(ones2, jnp.concatenate(sq_pair, axis=1).astype(BF16))
        top = jnp.max(sums, axis=1, keepdims=True)
        lane = lax.broadcasted_iota(jnp.int32, (1, 2 * LANES), 1)
        return jnp.where(lane < LANES, top[0:1], top[1:2])

    def project(rows):
        h = _modulated_norm(x_ref[rows, :], g_ref[...], m[0:1], m[1:2]).astype(BF16)
        cos = cos_ref[rows, :]
        sin = sin_ref[rows, :]
        cq = _head_norm(_dot(h, wdq_ref[...]), qg_ref[...]).astype(BF16)
        q = _dot(cq, wuq_ref[...])
        ckv_pe = _dot(h, wdkv_ref[...])
        ckv = _head_norm(ckv_pe[:, :MLA_KV_RANK], kvg_ref[...]).astype(BF16)
        ko1, ko2 = _rope(ckv_pe[:, MLA_KV_RANK:MLA_KV_RANK + LANES], ckv_pe[:, MLA_KV_RANK + LANES:], cos, sin)
        k_pe = jnp.where(_lane_mask(0, 2 * half), ko1, ko2)
        k_pe_sq = k_pe * k_pe
        k_pe = k_pe.astype(BF16)
        kv = _dot(ckv, wukv_ref[...])
        q_norms, k_norms = [], []
        for pp in range(MLA_HEADS // 2):
            p0 = pp * 2 * MLA_QK
            o1, o2 = _rope(q[:, p0 + 2 * MLA_NOPE:p0 + 2 * MLA_NOPE + LANES],
                           q[:, p0 + 2 * MLA_NOPE + LANES:p0 + 2 * MLA_QK], cos, sin)
            q_sq, k_sq = [], []
            for t in range(2):
                j = 2 * pp + t
                c0 = j * MLA_QK
                pe = jnp.where(_lane_mask(t * half, (t + 1) * half), o1,
                               jnp.where(_lane_mask(2 * half + t * half, 2 * half + (t + 1) * half), o2, 0.0))
                q_nope = q[:, p0 + t * MLA_NOPE:p0 + (t + 1) * MLA_NOPE]
                k_nope = kv[:, j * MLA_NOPE:(j + 1) * MLA_NOPE]
                q_ref[rows, c0:c0 + MLA_NOPE] = q_nope.astype(BF16)
                q_ref[rows, c0 + MLA_NOPE:c0 + MLA_QK] = pe.astype(BF16)
                k_ref[rows, c0:c0 + MLA_NOPE] = k_nope.astype(BF16)
                k_ref[rows, c0 + MLA_NOPE:c0 + MLA_QK] = k_pe
                q_sq.append(q_nope * q_nope + pe * pe)
                k_sq.append(k_nope * k_nope + k_pe_sq)
            q_norms.append(max_sq_norm(q_sq))
            k_norms.append(max_sq_norm(k_sq))
        _store_vt(vt_ref, kv[:, MLA_HEADS * MLA_NOPE:], MLA_HEADS, rows)
        return jnp.concatenate(q_norms, axis=1), jnp.concatenate(k_norms, axis=1)

    rows = x_ref.shape[0] // 2
    (qa, ka), (qb, kb) = project(slice(0, rows)), project(slice(rows, 2 * rows))
    qn_ref[0] = jnp.broadcast_to(jnp.maximum(qa, qb), qn_ref.shape[1:])
    kn_ref[0] = jnp.broadcast_to(jnp.maximum(ka, kb), kn_ref.shape[1:])


def _mla_proj(x, mod, gain, w_dq, q_gain, w_uq, w_dkv, kv_gain, w_ukv, pair_ones, cos, sin):
    qkw = MLA_HEADS * MLA_QK
    n_tiles = N_ALL // TP
    norm_spec = pl.BlockSpec((1, SUBLANES, MLA_HEADS * LANES), lambda i: (i, 0, 0))
    norm_shape = jax.ShapeDtypeStruct((n_tiles, SUBLANES, MLA_HEADS * LANES), F32)
    return pl.pallas_call(
        _mla_proj_kernel,
        grid=(N_ALL // TP,),
        in_specs=[
            _row_spec(TP, D), _mod_spec(TP), _resident((1, D)),
            _resident((D, MLA_Q_RANK)), _resident((1, MLA_Q_RANK)), _resident((MLA_Q_RANK, qkw)),
            _resident((D, MLA_KV_RANK + 2 * LANES)), _resident((1, MLA_KV_RANK)),
            _resident((MLA_KV_RANK, MLA_HEADS * (MLA_NOPE + MLA_V))),
            _resident((SUBLANES, 2 * LANES)),
            _rope_spec(TP), _rope_spec(TP),
        ],
        out_specs=[_row_spec(TP, qkw), _row_spec(TP, qkw), _col_spec(MLA_HEADS * VT_ROWS, TP),
                   norm_spec, norm_spec],
        out_shape=[jax.ShapeDtypeStruct((N_ALL, qkw), BF16),
                   jax.ShapeDtypeStruct((N_ALL, qkw), BF16),
                   jax.ShapeDtypeStruct((MLA_HEADS * VT_ROWS, N_ALL), BF16),
                   norm_shape, norm_shape],
        compiler_params=_params(1),
        name="mla_proj",
    )(x, mod, gain, w_dq, q_gain, w_uq, w_dkv, kv_gain, w_ukv, pair_ones, cos, sin)


def _score_bounds(qn, kn):
    qn = qn[:, 0, ::LANES]
    kn = kn[:, 0, ::LANES]
    per_seq = SEQ // TP
    q_lat = jnp.max(qn[:N_LAT // TP].reshape(BATCH, per_seq, -1), axis=1)
    k_lat = jnp.max(kn[:N_LAT // TP].reshape(BATCH, per_seq, -1), axis=1)
    ctx_tile = N_LAT // TP + (jnp.arange(BATCH) * CTX_LEN) // TP
    k_all = jnp.maximum(k_lat, kn[ctx_tile])
    return 1.02 * jnp.sqrt(jnp.max(q_lat * k_all, axis=1))


def _nt_dot(a, b):
    return lax.dot_general(a, b, (((1,), (1,)), ((), ())), preferred_element_type=F32)


def _attn_scores(q, k_parts):
    return [_nt_dot(k, q) for k in k_parts]


def _attn_kernel(bound_ref, q_ref, qc_ref, kl_ref, kc_ref, vtl_ref, vtc_ref, o_ref, oc_ref,
                 *, heads, group, dq, dv, shared_k):
    n = heads * group

    def k_cols(qh):
        kv = 0 if shared_k else qh // group
        return slice(kv * dq, (kv + 1) * dq)

    def v_rows(qh):
        return slice(qh // group * VT_ROWS, (qh // group + 1) * VT_ROWS)

    def run(queries, out_ref, k_refs, vt_refs, shift=None):
        def scores(qh):
            return _attn_scores(queries[:, qh * dq:(qh + 1) * dq], [k[:, k_cols(qh)] for k in k_refs])

        ahead = 2 if shift is None else 1
        pending = [scores(qh) for qh in range(min(ahead, n))]
        for qh in range(n):
            cur = pending.pop(0)
            if shift is None:
                mx = functools.reduce(jnp.maximum, [jnp.max(s, axis=0, keepdims=True) for s in cur])
            else:
                mx = shift
            probs = [jnp.exp2(s - mx) for s in cur]
            den = sum(jnp.sum(p, axis=0, keepdims=True) for p in probs)
            probs = [p.astype(BF16) for p in probs]
            if qh + ahead < n:
                pending.append(scores(qh + ahead))
            acc = sum(_dot(vt[v_rows(qh), :], p) for vt, p in zip(vt_refs, probs))
            out_ref[:, qh * dv:(qh + 1) * dv] = (acc / den).T.astype(BF16)

    lat = (q_ref, o_ref, [kl_ref, kc_ref], [vtl_ref, vtc_ref])
    bound = bound_ref[0, pl.program_id(0)]

    @pl.when(bound <= MAX_BOUND_SHIFT)
    def _():
        run(*lat, shift=bound)

    @pl.when(jnp.logical_not(bound <= MAX_BOUND_SHIFT))
    def _():
        run(*lat)

    @pl.when(pl.program_id(2) == 0)
    def _():
        run(qc_ref, oc_ref, [kc_ref], [vtc_ref])


def _attention(q, k, vt, score_bounds, casts, mod_layers, mod_operands, *, kv_heads, group, dq, dv,
               heads_per_step, shared_k):
    hp = heads_per_step
    lat_tiles = SEQ // TQ
    ctx_block0 = N_LAT // CTX_LEN
    kw = dq if shared_k else hp * dq
    k_col = (lambda h: 0) if shared_k else (lambda h: h)
    width = kv_heads * group * dv
    h_steps = kv_heads // hp
    n_steps = BATCH * h_steps * lat_tiles
    assert n_steps >= CAST_CHUNKS
    def chunk_of(b, h, t):
        return ((b * h_steps + h) * lat_tiles + t) * CAST_CHUNKS // n_steps

    r_in, r_out, r_shape = _cast_riders(casts, chunk_of)
    rider_operands = [w for w, _ in casts]
    if mod_layers is not None:
        m_in, m_out, m_shape = _mod_rider(*mod_layers, chunk_of)
        r_in, r_out, r_shape = r_in + m_in, r_out + [m_out], r_shape + [m_shape]
        rider_operands += list(mod_operands)
    body = functools.partial(_attn_kernel, heads=hp, group=group, dq=dq, dv=dv, shared_k=shared_k)
    return pl.pallas_call(
        _with_riders(body, 7, 2, mod_rider=mod_layers is not None),
        grid=(BATCH, h_steps, lat_tiles),
        in_specs=[
            pl.BlockSpec(memory_space=pltpu.SMEM),
            pl.BlockSpec((TQ, hp * group * dq), lambda b, h, t: (b * lat_tiles + t, h)),
            pl.BlockSpec((CTX_LEN, hp * group * dq), lambda b, h, t: (ctx_block0 + b, h)),
            pl.BlockSpec((SEQ, kw), lambda b, h, t: (b, k_col(h))),
            pl.BlockSpec((CTX_LEN, kw), lambda b, h, t: (ctx_block0 + b, k_col(h))),
            pl.BlockSpec((hp * VT_ROWS, SEQ), lambda b, h, t: (h, b)),
            pl.BlockSpec((hp * VT_ROWS, CTX_LEN), lambda b, h, t: (h, ctx_block0 + b)),
        ] + r_in,
        out_specs=[pl.BlockSpec((TQ, hp * group * dv), lambda b, h, t: (b * lat_tiles + t, h)),
                   pl.BlockSpec((CTX_LEN, hp * group * dv), lambda b, h, t: (b, h))] + r_out,
        out_shape=[jax.ShapeDtypeStruct((N_LAT, width), BF16), jax.ShapeDtypeStruct((N_CTX, width), BF16)] + r_shape,
        compiler_params=_params(3),
        name="attention",
    )(score_bounds.reshape(1, BATCH).astype(F32), q, q, k, k, vt, vt, *rider_operands)


def _ffn_tail(x1, m, g2_ref, w1_ref, w3_ref, w2_ref):
    rows = x1.shape[0] // FFN_SPLITS
    parts = [x1[i * rows:(i + 1) * rows] for i in range(FFN_SPLITS)]

    def up(x):
        h = _modulated_norm(x, g2_ref[...], m[3:4], m[4:5]).astype(BF16)
        return _dot(h, _w(w1_ref)), _dot(h, _w(w3_ref))

    def down(x, ab):
        a, b = ab
        gated = (a * jax.nn.sigmoid(a) * b).astype(BF16)
        return x + m[5:6] * _dot(gated, _w(w2_ref))

    outs = []
    nxt = up(parts[0])
    for i, x in enumerate(parts):
        cur = nxt
        if i + 1 < FFN_SPLITS:
            nxt = up(parts[i + 1])
        outs.append(down(x, cur))
    return jnp.concatenate(outs, axis=0)


def _attn_tail_kernel(al_ref, ac_ref, wo_ref, x_ref, mod_ref, g2_ref, w1_ref, w3_ref, w2_ref, o_ref):
    m = mod_ref[0]
    a = jnp.where(pl.program_id(0) < N_LAT // TM, al_ref[...], ac_ref[...])
    x1 = x_ref[...] + m[2:3] * _dot(a, _w(wo_ref))
    o_ref[...] = _ffn_tail(x1, m, g2_ref, w1_ref, w3_ref, w2_ref)


def _conv_tail_kernel(b_ref, u_ref, up_ref, un_ref, cw_ref, wo_ref, x_ref, mod_ref,
                      g2_ref, w1_ref, w3_ref, w2_ref, fg_ref, o_ref, bz_ref, *, final):
    m = mod_ref[0]
    x1 = x_ref[...] + m[2:3] * _conv_mix(b_ref, u_ref, up_ref, un_ref, cw_ref, wo_ref, bz_ref)
    x2 = _ffn_tail(x1, m, g2_ref, w1_ref, w3_ref, w2_ref)
    o_ref[...] = _head_norm(x2, fg_ref[...]) if final else x2


def _ffn_specs():
    return [_mod_spec(TM), _resident((1, D)), _resident((D, FFN_HIDDEN)),
            _resident((D, FFN_HIDDEN)), _resident((FFN_HIDDEN, D))]


def _attn_tail(a_lat, a_ctx, w_o, mixer_layer, x, mod, g2, w1, w3, w2):
    lat_tiles = N_LAT // TM
    return pl.pallas_call(
        _attn_tail_kernel,
        grid=(N_ALL // TM,),
        in_specs=[
            pl.BlockSpec((TM, D), lambda i: (jnp.minimum(i, lat_tiles - 1), 0)),
            pl.BlockSpec((TM, D), lambda i: (jnp.maximum(i - lat_tiles, 0), 0)),
            _resident((D, D), mixer_layer), _row_spec(TM, D),
        ] + _ffn_specs(),
        out_specs=_row_spec(TM, D),
        out_shape=jax.ShapeDtypeStruct((N_ALL, D), F32),
        compiler_params=_params(1),
        name="attn_tail",
    )(a_lat, a_ctx, w_o, x, mod, g2, w1, w3, w2)


def _conv_tail(b, u, conv_w, w_out, mixer_layer, x, mod, g2, w1, w3, w2, final_gain, n_rows, final):
    per = TM // HALO
    n_halo = n_rows // HALO
    return pl.pallas_call(
        functools.partial(_conv_tail_kernel, final=final),
        grid=(n_rows // TM,),
        in_specs=[
            _row_spec(TM, D), _row_spec(TM, D),
            pl.BlockSpec((HALO, D), lambda i: (jnp.maximum(i * per - 1, 0), 0)),
            pl.BlockSpec((HALO, D), lambda i: (jnp.minimum((i + 1) * per, n_halo - 1), 0)),
            _resident((3, D)), _resident((D, D), mixer_layer), _row_spec(TM, D),
        ] + _ffn_specs() + [_resident((1, D))],
        out_specs=_row_spec(TM, D),
        out_shape=jax.ShapeDtypeStruct((n_rows, D), F32),
        scratch_shapes=[pltpu.VMEM((TM, D), BF16)],
        compiler_params=_params(1),
        name="conv_tail",
    )(b, u, u, u, conv_w, w_out, x, mod, g2, w1, w3, w2, final_gain)


def _axial_angles(rot_dim):
    n = rot_dim // 4
    rows = jnp.repeat(jnp.arange(SEQ // GRID_W, dtype=F32), GRID_W)
    cols = jnp.tile(jnp.arange(GRID_W, dtype=F32), SEQ // GRID_W)
    freqs = ROPE_THETA ** (-jnp.arange(n, dtype=F32) / n)
    return jnp.concatenate([rows[:, None] * freqs, cols[:, None] * freqs], axis=-1)


def _rope_tables(rot_dim):
    ang = _axial_angles(rot_dim)
    copies = LANES // (rot_dim // 2)
    cos_t = jnp.concatenate([jnp.tile(jnp.cos(ang), (1, copies)), jnp.ones((TP, LANES), F32)], axis=0)
    sin_t = jnp.concatenate([jnp.tile(jnp.sin(ang), (1, copies)), jnp.zeros((TP, LANES), F32)], axis=0)
    return cos_t, sin_t


def _gqa_slab_pairs(w, n_kv, n_group):
    half = GQA_HEAD_DIM // 2
    w = w.reshape(w.shape[0], n_kv, n_group, 2, half)
    return jnp.transpose(w, (0, 2, 3, 1, 4)).reshape(w.shape[0], n_group * GQA_QK)


def _mla_weights(w_dq, w_uq, w_dkv, w_ukv):
    half = MLA_ROPE // 2
    qk = MLA_NOPE + MLA_ROPE
    w_dq, w_uq, w_dkv, w_ukv = (w.astype(BF16) for w in (w_dq, w_uq, w_dkv, w_ukv))
    uq = w_uq.reshape(MLA_Q_RANK, MLA_HEADS // 2, 2, qk)
    nope = uq[..., :MLA_NOPE].reshape(MLA_Q_RANK, MLA_HEADS // 2, 2 * MLA_NOPE)
    x1 = uq[..., MLA_NOPE:MLA_NOPE + half].reshape(MLA_Q_RANK, MLA_HEADS // 2, 2 * half)
    x2 = uq[..., MLA_NOPE + half:].reshape(MLA_Q_RANK, MLA_HEADS // 2, 2 * half)
    uq = jnp.concatenate([nope, x1, x1, x2, x2], axis=-1).reshape(MLA_Q_RANK, MLA_HEADS * MLA_QK)
    k1 = w_dkv[:, MLA_KV_RANK:MLA_KV_RANK + half]
    k2 = w_dkv[:, MLA_KV_RANK + half:]
    dkv = jnp.concatenate([w_dkv[:, :MLA_KV_RANK]] + [k1] * 4 + [k2] * 4, axis=-1)
    ukv = w_ukv.reshape(MLA_KV_RANK, MLA_HEADS, MLA_NOPE + MLA_V)
    ukv = jnp.concatenate([ukv[:, :, :MLA_NOPE].reshape(MLA_KV_RANK, -1),
                           ukv[:, :, MLA_NOPE:].reshape(MLA_KV_RANK, -1)], axis=-1)
    return w_dq, uq, dkv, ukv


def kernel(x, c, ctx, c_ctx, ada_w, ada_b, norm1_g, norm2_g, ffn_w1, ffn_w3, ffn_w2, conv_w_in, conv_w, conv_w_out, gqa_wq, gqa_wk, gqa_wv, gqa_q_norm, gqa_k_norm, gqa_wo, mla_w_dq, mla_q_norm, mla_w_uq, mla_w_dkv, mla_kv_norm, mla_w_ukv, mla_wo, final_g):
    assert x.shape == (BATCH, SEQ, D) and ctx.shape == (BATCH, CTX_LEN, D)
    cond = jnp.concatenate(
        [c, c_ctx[None], jnp.zeros((MOD_ROWS - BATCH - 1, D), F32)], axis=0)
    mod_operands = (cond, ada_w, ada_b.reshape(DEPTH, 1, N_MOD * D))
    mods = list(_modulation_tables(*mod_operands, 2).reshape(2, MOD_ROWS, N_MOD, D))

    gqa_cos, gqa_sin = _rope_tables(GQA_HEAD_DIM)
    lane_half = jnp.arange(LANES) // (LANES // 2)
    half_ones = jnp.tile((lane_half[:, None] == lane_half[None, :]).astype(BF16), (2, 1))
    lane_head = jnp.arange(2 * LANES) // LANES
    pair_ones = (jnp.arange(SUBLANES)[:, None] == lane_head[None, :]).astype(BF16)
    mla_cos, mla_sin = _rope_tables(MLA_ROPE)

    ffn_w = (ffn_w1, ffn_w3, ffn_w2)
    conv_stacks = (conv_w_in, conv_w_out)
    conv_bf16 = {0: (conv_w_in[0].astype(BF16), conv_w_out[0].astype(BF16))}
    gqa_wo = gqa_wo.astype(BF16)
    mla_wo = mla_wo.astype(BF16)

    xs = None
    for i in range(DEPTH):
        kind, j = i % 3, i // 3
        mod = mods[i]
        last = i == DEPTH - 1
        n_rows = N_LAT if last else N_ALL
        g1 = norm1_g[i][None]
        g2 = norm2_g[i][None]
        if kind == 0:
            w_in, w_out = conv_bf16[j]
            if i == 0:
                b, u, xs, *ffn = _conv_in_first(x.reshape(N_LAT, D), ctx.reshape(N_CTX, D), mod, g1,
                                                w_in, None, ffn_w, i)
            else:
                b, u, *ffn = _conv_in(xs, mod, g1, w_in, None, n_rows, ffn_w, i)
            xs = _conv_tail(b, u, conv_w[j], w_out, None, xs, mod, g2, *ffn, final_g[None], n_rows, last)
        elif kind == 1:
            w_qkv = jnp.concatenate([_gqa_slab_pairs(gqa_wq[j].astype(BF16), GQA_KV_HEADS, GQA_GROUP),
                                     _gqa_slab_pairs(gqa_wk[j].astype(BF16), GQA_KV_HEADS, 1),
                                     gqa_wv[j].astype(BF16)], axis=-1)
            q_gain = _gqa_slab_pairs(jnp.tile(gqa_q_norm[j], GQA_KV_HEADS)[None], GQA_KV_HEADS, 1)
            k_gain = _gqa_slab_pairs(jnp.tile(gqa_k_norm[j], GQA_KV_HEADS)[None], GQA_KV_HEADS, 1)
            q_gain = q_gain * (GQA_HEAD_DIM ** -0.5 * LOG2E)
            bound = 1.02 * GQA_HEAD_DIM * jnp.max(jnp.abs(q_gain)) * jnp.max(jnp.abs(k_gain))
            q, k, vt = _gqa_proj(xs, mod, g1, w_qkv, q_gain, k_gain, half_ones, gqa_cos, gqa_sin)
            casts = [(w, i) for w in ffn_w] + [(w, N_CONV_LAYERS - 1) for w in conv_stacks]
            a_lat, a_ctx, *cast_out, mod_next = _attention(
                q, k, vt, jnp.full((BATCH,), bound), casts, (i + 1, 2), mod_operands,
                kv_heads=GQA_KV_HEADS, group=GQA_GROUP, dq=GQA_QK, dv=GQA_HEAD_DIM,
                heads_per_step=1, shared_k=True)
            ffn, conv_bf16[N_CONV_LAYERS - 1] = cast_out[:3], tuple(cast_out[3:])
            mods += list(mod_next.reshape(2, MOD_ROWS, N_MOD, D))
            xs = _attn_tail(a_lat, a_ctx, gqa_wo, j, xs, mod, g2, *ffn)
        else:
            w_dq, w_uq, w_dkv, w_ukv = _mla_weights(mla_w_dq[j], mla_w_uq[j], mla_w_dkv[j], mla_w_ukv[j])
            q_gain = mla_q_norm[j][None] * ((MLA_NOPE + MLA_ROPE) ** -0.5 * LOG2E)
            q, k, vt, qn, kn = _mla_proj(xs, mod, g1, w_dq, q_gain, w_uq, w_dkv,
                                         mla_kv_norm[j][None], w_ukv, pair_ones, mla_cos, mla_sin)
            a_lat, a_ctx, *ffn = _attention(
                q, k, vt, _score_bounds(qn, kn), [(w, i) for w in ffn_w], None, mod_operands,
                kv_heads=MLA_HEADS, group=1, dq=MLA_QK, dv=MLA_V, heads_per_step=4, shared_k=False)
            xs = _attn_tail(a_lat, a_ctx, mla_wo, j, xs, mod, g2, *ffn)

    assert (DEPTH - 1) % 3 == 0
    return xs.reshape(BATCH, SEQ, D)
```

```python
import functools

import jax
import jax.numpy as jnp
from jax import lax
from jax.experimental import pallas as pl
from jax.experimental.pallas import tpu as pltpu

D = 1024
BATCH = 8
SEQ = 2048
DEPTH = 4
GRID_W = 64
CTX_LEN = 256
N_MOD = 6
FFN_HIDDEN = 2816
ROPE_THETA = 10000.0
EPS = 1e-6
GQA_HEAD_DIM = 128
GQA_HEADS = 8
GQA_KV_HEADS = 2
GQA_GROUP = 4
MLA_HEADS = 8
MLA_NOPE = 128
MLA_ROPE = 64
MLA_V = 128
MLA_KV_RANK = 256
MLA_Q_RANK = 768
MLA_QK = 256
GQA_QK = 256
LANES = 128
SUBLANES = 8
VT_ROWS = 128

N_LAT = BATCH * SEQ
N_CTX = BATCH * CTX_LEN
N_ALL = N_LAT + N_CTX
MOD_ROWS = 16
N_CONV_LAYERS = (DEPTH + 2) // 3

TM = 512
TP = 1024
TQ = 1024
HALO = SUBLANES
CONV_COLS = 256
FFN_SPLITS = 2
CAST_CHUNKS = 16
MOD_COLS = 3072
VMEM_LIMIT = 56 * 1024 * 1024
LOG2E = 1.4426950408889634
MAX_BOUND_SHIFT = 50.0

F32 = jnp.float32
BF16 = jnp.bfloat16


def _params(n_axes):
    return pltpu.CompilerParams(
        dimension_semantics=("arbitrary",) * n_axes, vmem_limit_bytes=VMEM_LIMIT)


def _resident(shape, layer=None):
    nd = len(shape)
    if layer is None:
        return pl.BlockSpec(shape, lambda *_: (0,) * nd, pipeline_mode=pl.Buffered(1))
    return pl.BlockSpec((1,) + tuple(shape), lambda *_: (layer,) + (0,) * nd, pipeline_mode=pl.Buffered(1))


def _w(ref):
    return ref[0] if len(ref.shape) == 3 else ref[...]


def _cast_riders(stacks, chunk_of):
    in_specs, out_specs, out_shapes = [], [], []
    for w, layer in stacks:
        _, rows, cols = w.shape
        r = rows // CAST_CHUNKS
        in_specs.append(pl.BlockSpec((1, r, cols), lambda *g, layer=layer: (layer, chunk_of(*g), 0)))
        out_specs.append(pl.BlockSpec((r, cols), lambda *g: (chunk_of(*g), 0)))
        out_shapes.append(jax.ShapeDtypeStruct((rows, cols), BF16))
    return in_specs, out_specs, out_shapes


def _mod_rider(first_layer, n_layers, chunk_of):
    tn = N_MOD * D // CAST_CHUNKS
    assert first_layer % n_layers == 0
    blk = first_layer // n_layers
    in_specs = [pl.BlockSpec((MOD_ROWS, D), lambda *g: (0, 0)),
                pl.BlockSpec((n_layers, D, tn), lambda *g: (blk, 0, chunk_of(*g))),
                pl.BlockSpec((n_layers, 1, tn), lambda *g: (blk, 0, chunk_of(*g)))]
    out_spec = pl.BlockSpec((n_layers, MOD_ROWS, tn), lambda *g: (0, 0, chunk_of(*g)))
    return in_specs, out_spec, jax.ShapeDtypeStruct((n_layers, MOD_ROWS, N_MOD * D), F32)


def _with_riders(body, n_in, n_out, mod_rider=False):
    n_mod_in, n_mod_out = (3, 1) if mod_rider else (0, 0)

    def kern(*refs):
        n_casts = (len(refs) - n_in - n_out - n_mod_in - n_mod_out) // 2
        ins, cast_in = refs[:n_in], refs[n_in:n_in + n_casts]
        mod_in = refs[n_in + n_casts:n_in + n_casts + n_mod_in]
        o0 = n_in + n_casts + n_mod_in
        outs, cast_out = refs[o0:o0 + n_out], refs[o0 + n_out:o0 + n_out + n_casts]
        mod_out = refs[o0 + n_out + n_casts:]
        body(*ins, *outs)
        for src, dst in zip(cast_in, cast_out):
            dst[...] = src[0].astype(dst.dtype)
        if mod_rider:
            _mod_kernel(*mod_in, *mod_out)
    return kern


def _mod_spec(tm):
    return pl.BlockSpec((1, N_MOD, D), lambda i: (jnp.minimum(i * tm // SEQ, BATCH), 0, 0))


def _row_spec(tm, width):
    return pl.BlockSpec((tm, width), lambda i: (i, 0))


def _col_spec(height, tm):
    return pl.BlockSpec((height, tm), lambda i: (0, i))


def _dot(a, b):
    return jnp.dot(a, b, preferred_element_type=F32)


def _modulated_norm(x, gain, shift, scale):
    r = lax.rsqrt(jnp.mean(x * x, axis=-1, keepdims=True) + EPS)
    return (x * r) * (gain * (1.0 + scale)) + shift


def _head_norm(x, gain):
    r = lax.rsqrt(jnp.mean(x * x, axis=-1, keepdims=True) + EPS)
    return (x * r) * gain


def _store_vt(vt_ref, v, n_heads, tokens=slice(None)):
    for j in range(n_heads):
        vt_ref[j * VT_ROWS:(j + 1) * VT_ROWS, tokens] = v[:, j * VT_ROWS:(j + 1) * VT_ROWS].T.astype(BF16)


def _rope(a, b, cos, sin):
    return a * cos - b * sin, a * sin + b * cos


def _lane_mask(lo, hi):
    lane = lax.broadcasted_iota(jnp.int32, (1, LANES), 1)
    return (lane >= lo) & (lane < hi)


def _mod_kernel(cond_ref, w_ref, b_ref, o_ref):
    c = cond_ref[...]
    act = (c * jax.nn.sigmoid(c)).astype(BF16)
    for l in range(w_ref.shape[0]):
        o_ref[l] = _dot(act, w_ref[l].astype(BF16)) + b_ref[l]


def _modulation_tables(cond, ada_w, ada_b, n_layers):
    tn = MOD_COLS
    return pl.pallas_call(
        _mod_kernel,
        grid=(n_layers, N_MOD * D // tn),
        in_specs=[
            pl.BlockSpec((MOD_ROWS, D), lambda l, j: (0, 0)),
            pl.BlockSpec((1, D, tn), lambda l, j: (l, 0, j)),
            pl.BlockSpec((1, 1, tn), lambda l, j: (l, 0, j)),
        ],
        out_specs=pl.BlockSpec((1, MOD_ROWS, tn), lambda l, j: (l, 0, j)),
        out_shape=jax.ShapeDtypeStruct((n_layers, MOD_ROWS, N_MOD * D), F32),
        compiler_params=_params(2),
        name="modulation",
    )(cond, ada_w, ada_b)


def _conv_in_compute(x, mod_ref, g_ref, w_ref, b_ref, u_ref):
    m = mod_ref[0]
    h = _modulated_norm(x, g_ref[...], m[0:1], m[1:2]).astype(BF16)
    y = _dot(h, _w(w_ref))
    b_ref[...] = y[:, :D].astype(BF16)
    u_ref[...] = y[:, D:2 * D] * y[:, 2 * D:]


def _conv_in_kernel(x_ref, mod_ref, g_ref, w_ref, b_ref, u_ref):
    _conv_in_compute(x_ref[...], mod_ref, g_ref, w_ref, b_ref, u_ref)


def _conv_in_first_kernel(lat_ref, ctx_ref, mod_ref, g_ref, w_ref, b_ref, u_ref, xs_ref):
    x = jnp.where(pl.program_id(0) < N_LAT // TP, lat_ref[...], ctx_ref[...])
    xs_ref[...] = x
    _conv_in_compute(x, mod_ref, g_ref, w_ref, b_ref, u_ref)


def _conv_in(x, mod, gain, w_in, layer, n_rows, ffn_w, ffn_layer):
    r_in, r_out, r_shape = _cast_riders([(w, ffn_layer) for w in ffn_w], lambda i: jnp.minimum(i, CAST_CHUNKS - 1))
    assert n_rows // TP >= CAST_CHUNKS
    return pl.pallas_call(
        _with_riders(_conv_in_kernel, 4, 2),
        grid=(n_rows // TP,),
        in_specs=[_row_spec(TP, D), _mod_spec(TP), _resident((1, D)), _resident((D, 3 * D), layer)] + r_in,
        out_specs=[_row_spec(TP, D), _row_spec(TP, D)] + r_out,
        out_shape=[jax.ShapeDtypeStruct((n_rows, D), BF16), jax.ShapeDtypeStruct((n_rows, D), F32)] + r_shape,
        compiler_params=_params(1),
        name="conv_in",
    )(x, mod, gain, w_in, *ffn_w)


def _conv_in_first(lat, ctx, mod, gain, w_in, layer, ffn_w, ffn_layer):
    lat_tiles = N_LAT // TP
    r_in, r_out, r_shape = _cast_riders([(w, ffn_layer) for w in ffn_w], lambda i: jnp.minimum(i, CAST_CHUNKS - 1))
    assert N_ALL // TP >= CAST_CHUNKS
    return pl.pallas_call(
        _with_riders(_conv_in_first_kernel, 5, 3),
        grid=(N_ALL // TP,),
        in_specs=[
            pl.BlockSpec((TP, D), lambda i: (jnp.minimum(i, lat_tiles - 1), 0)),
            pl.BlockSpec((TP, D), lambda i: (jnp.maximum(i - lat_tiles, 0), 0)),
            _mod_spec(TP), _resident((1, D)), _resident((D, 3 * D), layer),
        ] + r_in,
        out_specs=[_row_spec(TP, D), _row_spec(TP, D), _row_spec(TP, D)] + r_out,
        out_shape=[jax.ShapeDtypeStruct((N_ALL, D), BF16), jax.ShapeDtypeStruct((N_ALL, D), F32),
                   jax.ShapeDtypeStruct((N_ALL, D), F32)] + r_shape,
        compiler_params=_params(1),
        name="conv_in_first",
    )(lat, ctx, mod, gain, w_in, *ffn_w)


def _conv_gate_exact(b_ref, u_ref, up_ref, un_ref, cw_ref, r0, r1, cols):
    tm = u_ref.shape[0]
    n = r1 - r0
    u = u_ref[r0:r1, cols]
    before = up_ref[HALO - 1:HALO, cols] if r0 == 0 else u_ref[r0 - 1:r0, cols]
    after = un_ref[0:1, cols] if r1 == tm else u_ref[r1:r1 + 1, cols]
    local = lax.broadcasted_iota(jnp.int32, (n, 1), 0)
    row = local + (pl.program_id(0) * tm + r0)
    in_ctx = row >= N_LAT
    first = ((row & (CTX_LEN - 1)) == 0) & (in_ctx | ((row & (SEQ - 1)) == 0))
    last = ((row & (CTX_LEN - 1)) == CTX_LEN - 1) & (in_ctx | ((row & (SEQ - 1)) == SEQ - 1))
    prev = jnp.where(local == 0, before, pltpu.roll(u, 1, axis=0))
    prev = jnp.where(first, 0.0, prev)
    nxt = jnp.where(local == n - 1, after, pltpu.roll(u, n - 1, axis=0))
    nxt = jnp.where(last, 0.0, nxt)
    cw = cw_ref[:, cols]
    z = prev * cw[0:1] + u * cw[1:2] + nxt * cw[2:3]
    return (b_ref[r0:r1, cols].astype(F32) * z).astype(BF16)


def _conv_mix(b_ref, u_ref, up_ref, un_ref, cw_ref, wo_ref, bz_ref):
    tm = u_ref.shape[0]
    group = 2 * HALO
    edges = [(0, group), (tm - group, tm)]
    edges += [(r - group, r + group) for r in range(CTX_LEN, tm, CTX_LEN)]
    wo = wo_ref.at[0] if len(wo_ref.shape) == 3 else wo_ref
    acc = None
    for c0 in range(0, D, CONV_COLS):
        cols = slice(c0, c0 + CONV_COLS)
        u = u_ref[:, cols]
        cw = cw_ref[:, cols]
        z = pltpu.roll(u, 1, axis=0) * cw[0:1] + u * cw[1:2] + pltpu.roll(u, tm - 1, axis=0) * cw[2:3]
        bz_ref[:, cols] = (b_ref[:, cols].astype(F32) * z).astype(BF16)
        for r0, r1 in edges:
            bz_ref[r0:r1, cols] = _conv_gate_exact(b_ref, u_ref, up_ref, un_ref, cw_ref, r0, r1, cols)
        part = _dot(bz_ref[:, cols], wo[cols, :])
        acc = part if acc is None else acc + part
    return acc


def _gqa_proj_kernel(x_ref, mod_ref, g_ref, w_ref, qg_ref, kg_ref, ones_ref, cos_ref, sin_ref,
                     q_ref, k_ref, vt_ref):
    m = mod_ref[0]
    ones2 = ones_ref[...]
    wide = 2 * GQA_QK
    kv0 = _lane_mask(0, LANES // 2)

    def project(rows):
        h = _modulated_norm(x_ref[rows, :], g_ref[...], m[0:1], m[1:2]).astype(BF16)
        cos = cos_ref[rows, :]
        sin = sin_ref[rows, :]

        def normed_rotated(y, gain):
            a, b = y[:, :LANES], y[:, LANES:]
            ss = a * a + b * b
            hi = ss.astype(BF16)
            lo = (ss - hi.astype(F32)).astype(BF16)
            tot = _dot(jnp.concatenate([hi, lo], axis=1), ones2)
            r = lax.rsqrt(tot * (1.0 / GQA_HEAD_DIM) + EPS)
            return _rope(a * r * gain[:, :LANES], b * r * gain[:, LANES:], cos, sin)

        for gg in range(GQA_GROUP // 2):
            y = _dot(h, w_ref[:, gg * wide:(gg + 1) * wide])
            for t in range(2):
                g = 2 * gg + t
                o1, o2 = normed_rotated(y[:, t * GQA_QK:(t + 1) * GQA_QK], qg_ref[...])
                for kv, keep in ((0, kv0), (1, jnp.logical_not(kv0))):
                    c0 = (kv * GQA_GROUP + g) * GQA_QK
                    q_ref[rows, c0:c0 + LANES] = jnp.where(keep, o1, 0.0).astype(BF16)
                    q_ref[rows, c0 + LANES:c0 + GQA_QK] = jnp.where(keep, o2, 0.0).astype(BF16)
        y = _dot(h, w_ref[:, GQA_GROUP * GQA_QK:])
        o1, o2 = normed_rotated(y[:, :GQA_QK], kg_ref[...])
        k_ref[rows, :LANES] = o1.astype(BF16)
        k_ref[rows, LANES:] = o2.astype(BF16)
        _store_vt(vt_ref, y[:, GQA_QK:], GQA_KV_HEADS, rows)

    rows = x_ref.shape[0] // 2
    project(slice(0, rows))
    project(slice(rows, 2 * rows))


def _rope_spec(tm):
    per_seq = SEQ // tm
    return pl.BlockSpec(
        (tm, LANES), lambda i: (jnp.where(i < N_LAT // tm, i % per_seq, per_seq), 0))


def _gqa_proj(x, mod, gain, w_qkv, q_gain, k_gain, half_ones, cos, sin):
    kvw = GQA_KV_HEADS * GQA_HEAD_DIM
    qw = GQA_HEADS * GQA_QK
    return pl.pallas_call(
        _gqa_proj_kernel,
        grid=(N_ALL // TP,),
        in_specs=[
            _row_spec(TP, D), _mod_spec(TP), _resident((1, D)),
            _resident((D, D + 2 * kvw)),
            _resident((1, GQA_QK)), _resident((1, GQA_QK)), _resident((2 * LANES, LANES)),
            _rope_spec(TP), _rope_spec(TP),
        ],
        out_specs=[_row_spec(TP, qw), _row_spec(TP, GQA_QK), _col_spec(GQA_KV_HEADS * VT_ROWS, TP)],
        out_shape=[jax.ShapeDtypeStruct((N_ALL, qw), BF16),
                   jax.ShapeDtypeStruct((N_ALL, GQA_QK), BF16),
                   jax.ShapeDtypeStruct((GQA_KV_HEADS * VT_ROWS, N_ALL), BF16)],
        compiler_params=_params(1),
        name="gqa_proj",
    )(x, mod, gain, w_qkv, q_gain, k_gain, half_ones, cos, sin)


def _mla_proj_kernel(x_ref, mod_ref, g_ref, wdq_ref, qg_ref, wuq_ref, wdkv_ref, kvg_ref, wukv_ref,
                     ones_ref, cos_ref, sin_ref, q_ref, k_ref, vt_ref, qn_ref, kn_ref):
    m = mod_ref[0]
    half = MLA_ROPE // 2
    ones2 = ones_ref[...]

    def max_sq_norm(sq_pair):
        sums = _nt_dot(ones2, jnp.concatenate(sq_pair, axis=1).astype(BF16))
        top = jnp.max(sums, axis=1, keepdims=True)
        lane = lax.broadcasted_iota(jnp.int32, (1, 2 * LANES), 1)
        return jnp.where(lane < LANES, top[0:1], top[1:2])

    def project(rows):
        h = _modulated_norm(x_ref[rows, :], g_ref[...], m[0:1], m[1:2]).astype(BF16)
        cos = cos_ref[rows, :]
        sin = sin_ref[rows, :]
        cq = _head_norm(_dot(h, wdq_ref[...]), qg_ref[...]).astype(BF16)
        q = _dot(cq, wuq_ref[...])
        ckv_pe = _dot(h, wdkv_ref[...])
        ckv = _head_norm(ckv_pe[:, :MLA_KV_RANK], kvg_ref[...]).astype(BF16)
        ko1, ko2 = _rope(ckv_pe[:, MLA_KV_RANK:MLA_KV_RANK + LANES], ckv_pe[:, MLA_KV_RANK + LANES:], cos, sin)
        k_pe = jnp.where(_lane_mask(0, 2 * half), ko1, ko2)
        k_pe_sq = k_pe * k_pe
        k_pe = k_pe.astype(BF16)
        kv = _dot(ckv, wukv_ref[...])
        q_norms, k_norms = [], []
        for pp in range(MLA_HEADS // 2):
            p0 = pp * 2 * MLA_QK
            o1, o2 = _rope(q[:, p0 + 2 * MLA_NOPE:p0 + 2 * MLA_NOPE + LANES],
                           q[:, p0 + 2 * MLA_NOPE + LANES:p0 + 2 * MLA_QK], cos, sin)
            q_sq, k_sq = [], []
            for t in range(2):
                j = 2 * pp + t
                c0 = j * MLA_QK
                pe = jnp.where(_lane_mask(t * half, (t + 1) * half), o1,
                               jnp.where(_lane_mask(2 * half + t * half, 2 * half + (t + 1) * half), o2, 0.0))
                q_nope = q[:, p0 + t * MLA_NOPE:p0 + (t + 1) * MLA_NOPE]
                k_nope = kv[:, j * MLA_NOPE:(j + 1) * MLA_NOPE]
                q_ref[rows, c0:c0 + MLA_NOPE] = q_nope.astype(BF16)
                q_ref[rows, c0 + MLA_NOPE:c0 + MLA_QK] = pe.astype(BF16)
                k_ref[rows, c0:c0 + MLA_NOPE] = k_nope.astype(BF16)
                k_ref[rows, c0 + MLA_NOPE:c0 + MLA_QK] = k_pe
                q_sq.append(q_nope * q_nope + pe * pe)
                k_sq.append(k_nope * k_nope + k_pe_sq)
            q_norms.append(max_sq_norm(q_sq))
            k_norms.append(max_sq_norm(k_sq))
        _store_vt(vt_ref, kv[:, MLA_HEADS * MLA_NOPE:], MLA_HEADS, rows)
        return jnp.concatenate(q_norms, axis=1), jnp.concatenate(k_norms, axis=1)

    rows = x_ref.shape[0] // 2
    (qa, ka), (qb, kb) = project(slice(0, rows)), project(slice(rows, 2 * rows))
    qn_ref[0] = jnp.broadcast_to(jnp.maximum(qa, qb), qn_ref.shape[1:])
    kn_ref[0] = jnp.broadcast_to(jnp.maximum(ka, kb), kn_ref.shape[1:])


def _mla_proj(x, mod, gain, w_dq, q_gain, w_uq, w_dkv, kv_gain, w_ukv, pair_ones, cos, sin):
    qkw = MLA_HEADS * MLA_QK
    n_tiles = N_ALL // TP
    norm_spec = pl.BlockSpec((1, SUBLANES, MLA_HEADS * LANES), lambda i: (i, 0, 0))
    norm_shape = jax.ShapeDtypeStruct((n_tiles, SUBLANES, MLA_HEADS * LANES), F32)
    return pl.pallas_call(
        _mla_proj_kernel,
        grid=(N_ALL // TP,),
        in_specs=[
            _row_spec(TP, D), _mod_spec(TP), _resident((1, D)),
            _resident((D, MLA_Q_RANK)), _resident((1, MLA_Q_RANK)), _resident((MLA_Q_RANK, qkw)),
            _resident((D, MLA_KV_RANK + 2 * LANES)), _resident((1, MLA_KV_RANK)),
            _resident((MLA_KV_RANK, MLA_HEADS * (MLA_NOPE + MLA_V))),
            _resident((SUBLANES, 2 * LANES)),
            _rope_spec(TP), _rope_spec(TP),
        ],
        out_specs=[_row_spec(TP, qkw), _row_spec(TP, qkw), _col_spec(MLA_HEADS * VT_ROWS, TP),
                   norm_spec, norm_spec],
        out_shape=[jax.ShapeDtypeStruct((N_ALL, qkw), BF16),
                   jax.ShapeDtypeStruct((N_ALL, qkw), BF16),
                   jax.ShapeDtypeStruct((MLA_HEADS * VT_ROWS, N_ALL), BF16),
                   norm_shape, norm_shape],
        compiler_params=_params(1),
        name="mla_proj",
    )(x, mod, gain, w_dq, q_gain, w_uq, w_dkv, kv_gain, w_ukv, pair_ones, cos, sin)


def _score_bounds(qn, kn):
    qn = qn[:, 0, ::LANES]
    kn = kn[:, 0, ::LANES]
    per_seq = SEQ // TP
    q_lat = jnp.max(qn[:N_LAT // TP].reshape(BATCH, per_seq, -1), axis=1)
    k_lat = jnp.max(kn[:N_LAT // TP].reshape(BATCH, per_seq, -1), axis=1)
    ctx_tile = N_LAT // TP + (jnp.arange(BATCH) * CTX_LEN) // TP
    k_all = jnp.maximum(k_lat, kn[ctx_tile])
    return 1.02 * jnp.sqrt(jnp.max(q_lat * k_all, axis=1))


def _nt_dot(a, b):
    return lax.dot_general(a, b, (((1,), (1,)), ((), ())), preferred_element_type=F32)


def _attn_scores(q, k_parts):
    return [_nt_dot(k, q) for k in k_parts]


def _attn_kernel(bound_ref, q_ref, qc_ref, kl_ref, kc_ref, vtl_ref, vtc_ref, o_ref, oc_ref,
                 *, heads, group, dq, dv, shared_k):
    n = heads * group

    def k_cols(qh):
        kv = 0 if shared_k else qh // group
        return slice(kv * dq, (kv + 1) * dq)

    def v_rows(qh):
        return slice(qh // group * VT_ROWS, (qh // group + 1) * VT_ROWS)

    def run(queries, out_ref, k_refs, vt_refs, shift=None):
        def scores(qh):
            return _attn_scores(queries[:, qh * dq:(qh + 1) * dq], [k[:, k_cols(qh)] for k in k_refs])

        ahead = 2 if shift is None else 1
        pending = [scores(qh) for qh in range(min(ahead, n))]
        for qh in range(n):
            cur = pending.pop(0)
            if shift is None:
                mx = functools.reduce(jnp.maximum, [jnp.max(s, axis=0, keepdims=True) for s in cur])
            else:
                mx = shift
            probs = [jnp.exp2(s - mx) for s in cur]
            den = sum(jnp.sum(p, axis=0, keepdims=True) for p in probs)
            probs = [p.astype(BF16) for p in probs]
            if qh + ahead < n:
                pending.append(scores(qh + ahead))
            acc = sum(_dot(vt[v_rows(qh), :], p) for vt, p in zip(vt_refs, probs))
            out_ref[:, qh * dv:(qh + 1) * dv] = (acc / den).T.astype(BF16)

    lat = (q_ref, o_ref, [kl_ref, kc_ref], [vtl_ref, vtc_ref])
    bound = bound_ref[0, pl.program_id(0)]

    @pl.when(bound <= MAX_BOUND_SHIFT)
    def _():
        run(*lat, shift=bound)

    @pl.when(jnp.logical_not(bound <= MAX_BOUND_SHIFT))
    def _():
        run(*lat)

    @pl.when(pl.program_id(2) == 0)
    def _():
        run(qc_ref, oc_ref, [kc_ref], [vtc_ref])


def _attention(q, k, vt, score_bounds, casts, mod_layers, mod_operands, *, kv_heads, group, dq, dv,
               heads_per_step, shared_k):
    hp = heads_per_step
    lat_tiles = SEQ // TQ
    ctx_block0 = N_LAT // CTX_LEN
    kw = dq if shared_k else hp * dq
    k_col = (lambda h: 0) if shared_k else (lambda h: h)
    width = kv_heads * group * dv
    h_steps = kv_heads // hp
    n_steps = BATCH * h_steps * lat_tiles
    assert n_steps >= CAST_CHUNKS
    def chunk_of(b, h, t):
        return ((b * h_steps + h) * lat_tiles + t) * CAST_CHUNKS // n_steps

    r_in, r_out, r_shape = _cast_riders(casts, chunk_of)
    rider_operands = [w for w, _ in casts]
    if mod_layers is not None:
        m_in, m_out, m_shape = _mod_rider(*mod_layers, chunk_of)
        r_in, r_out, r_shape = r_in + m_in, r_out + [m_out], r_shape + [m_shape]
        rider_operands += list(mod_operands)
    body = functools.partial(_attn_kernel, heads=hp, group=group, dq=dq, dv=dv, shared_k=shared_k)
    return pl.pallas_call(
        _with_riders(body, 7, 2, mod_rider=mod_layers is not None),
        grid=(BATCH, h_steps, lat_tiles),
        in_specs=[
            pl.BlockSpec(memory_space=pltpu.SMEM),
            pl.BlockSpec((TQ, hp * group * dq), lambda b, h, t: (b * lat_tiles + t, h)),
            pl.BlockSpec((CTX_LEN, hp * group * dq), lambda b, h, t: (ctx_block0 + b, h)),
            pl.BlockSpec((SEQ, kw), lambda b, h, t: (b, k_col(h))),
            pl.BlockSpec((CTX_LEN, kw), lambda b, h, t: (ctx_block0 + b, k_col(h))),
            pl.BlockSpec((hp * VT_ROWS, SEQ), lambda b, h, t: (h, b)),
            pl.BlockSpec((hp * VT_ROWS, CTX_LEN), lambda b, h, t: (h, ctx_block0 + b)),
        ] + r_in,
        out_specs=[pl.BlockSpec((TQ, hp * group * dv), lambda b, h, t: (b * lat_tiles + t, h)),
                   pl.BlockSpec((CTX_LEN, hp * group * dv), lambda b, h, t: (b, h))] + r_out,
        out_shape=[jax.ShapeDtypeStruct((N_LAT, width), BF16), jax.ShapeDtypeStruct((N_CTX, width), BF16)] + r_shape,
        compiler_params=_params(3),
        name="attention",
    )(score_bounds.reshape(1, BATCH).astype(F32), q, q, k, k, vt, vt, *rider_operands)


def _ffn_tail(x1, m, g2_ref, w1_ref, w3_ref, w2_ref):
    rows = x1.shape[0] // FFN_SPLITS
    parts = [x1[i * rows:(i + 1) * rows] for i in range(FFN_SPLITS)]

    def up(x):
        h = _modulated_norm(x, g2_ref[...], m[3:4], m[4:5]).astype(BF16)
        return _dot(h, _w(w1_ref)), _dot(h, _w(w3_ref))

    def down(x, ab):
        a, b = ab
        gated = (a * jax.nn.sigmoid(a) * b).astype(BF16)
        return x + m[5:6] * _dot(gated, _w(w2_ref))

    outs = []
    nxt = up(parts[0])
    for i, x in enumerate(parts):
        cur = nxt
        if i + 1 < FFN_SPLITS:
            nxt = up(parts[i + 1])
        outs.append(down(x, cur))
    return jnp.concatenate(outs, axis=0)


def _attn_tail_kernel(al_ref, ac_ref, wo_ref, x_ref, mod_ref, g2_ref, w1_ref, w3_ref, w2_ref, o_ref):
    m = mod_ref[0]
    a = jnp.where(pl.program_id(0) < N_LAT // TM, al_ref[...], ac_ref[...])
    x1 = x_ref[...] + m[2:3] * _dot(a, _w(wo_ref))
    o_ref[...] = _ffn_tail(x1, m, g2_ref, w1_ref, w3_ref, w2_ref)


def _conv_tail_kernel(b_ref, u_ref, up_ref, un_ref, cw_ref, wo_ref, x_ref, mod_ref,
                      g2_ref, w1_ref, w3_ref, w2_ref, fg_ref, o_ref, bz_ref, *, final):
    m = mod_ref[0]
    x1 = x_ref[...] + m[2:3] * _conv_mix(b_ref, u_ref, up_ref, un_ref, cw_ref, wo_ref, bz_ref)
    x2 = _ffn_tail(x1, m, g2_ref, w1_ref, w3_ref, w2_ref)
    o_ref[...] = _head_norm(x2, fg_ref[...]) if final else x2


def _ffn_specs():
    return [_mod_spec(TM), _resident((1, D)), _resident((D, FFN_HIDDEN)),
            _resident((D, FFN_HIDDEN)), _resident((FFN_HIDDEN, D))]


def _attn_tail(a_lat, a_ctx, w_o, mixer_layer, x, mod, g2, w1, w3, w2):
    lat_tiles = N_LAT // TM
    return pl.pallas_call(
        _attn_tail_kernel,
        grid=(N_ALL // TM,),
        in_specs=[
            pl.BlockSpec((TM, D), lambda i: (jnp.minimum(i, lat_tiles - 1), 0)),
            pl.BlockSpec((TM, D), lambda i: (jnp.maximum(i - lat_tiles, 0), 0)),
            _resident((D, D), mixer_layer), _row_spec(TM, D),
        ] + _ffn_specs(),
        out_specs=_row_spec(TM, D),
        out_shape=jax.ShapeDtypeStruct((N_ALL, D), F32),
        compiler_params=_params(1),
        name="attn_tail",
    )(a_lat, a_ctx, w_o, x, mod, g2, w1, w3, w2)


def _conv_tail(b, u, conv_w, w_out, mixer_layer, x, mod, g2, w1, w3, w2, final_gain, n_rows, final):
    per = TM // HALO
    n_halo = n_rows // HALO
    return pl.pallas_call(
        functools.partial(_conv_tail_kernel, final=final),
        grid=(n_rows // TM,),
        in_specs=[
            _row_spec(TM, D), _row_spec(TM, D),
            pl.BlockSpec((HALO, D), lambda i: (jnp.maximum(i * per - 1, 0), 0)),
            pl.BlockSpec((HALO, D), lambda i: (jnp.minimum((i + 1) * per, n_halo - 1), 0)),
            _resident((3, D)), _resident((D, D), mixer_layer), _row_spec(TM, D),
        ] + _ffn_specs() + [_resident((1, D))],
        out_specs=_row_spec(TM, D),
        out_shape=jax.ShapeDtypeStruct((n_rows, D), F32),
        scratch_shapes=[pltpu.VMEM((TM, D), BF16)],
        compiler_params=_params(1),
        name="conv_tail",
    )(b, u, u, u, conv_w, w_out, x, mod, g2, w1, w3, w2, final_gain)


def _axial_angles(rot_dim):
    n = rot_dim // 4
    rows = jnp.repeat(jnp.arange(SEQ // GRID_W, dtype=F32), GRID_W)
    cols = jnp.tile(jnp.arange(GRID_W, dtype=F32), SEQ // GRID_W)
    freqs = ROPE_THETA ** (-jnp.arange(n, dtype=F32) / n)
    return jnp.concatenate([rows[:, None] * freqs, cols[:, None] * freqs], axis=-1)


def _rope_tables(rot_dim):
    ang = _axial_angles(rot_dim)
    copies = LANES // (rot_dim // 2)
    cos_t = jnp.concatenate([jnp.tile(jnp.cos(ang), (1, copies)), jnp.ones((TP, LANES), F32)], axis=0)
    sin_t = jnp.concatenate([jnp.tile(jnp.sin(ang), (1, copies)), jnp.zeros((TP, LANES), F32)], axis=0)
    return cos_t, sin_t


def _gqa_slab_pairs(w, n_kv, n_group):
    half = GQA_HEAD_DIM // 2
    w = w.reshape(w.shape[0], n_kv, n_group, 2, half)
    return jnp.transpose(w, (0, 2, 3, 1, 4)).reshape(w.shape[0], n_group * GQA_QK)


def _mla_weights(w_dq, w_uq, w_dkv, w_ukv):
    half = MLA_ROPE // 2
    qk = MLA_NOPE + MLA_ROPE
    w_dq, w_uq, w_dkv, w_ukv = (w.astype(BF16) for w in (w_dq, w_uq, w_dkv, w_ukv))
    uq = w_uq.reshape(MLA_Q_RANK, MLA_HEADS // 2, 2, qk)
    nope = uq[..., :MLA_NOPE].reshape(MLA_Q_RANK, MLA_HEADS // 2, 2 * MLA_NOPE)
    x1 = uq[..., MLA_NOPE:MLA_NOPE + half].reshape(MLA_Q_RANK, MLA_HEADS // 2, 2 * half)
    x2 = uq[..., MLA_NOPE + half:].reshape(MLA_Q_RANK, MLA_HEADS // 2, 2 * half)
    uq = jnp.concatenate([nope, x1, x1, x2, x2], axis=-1).reshape(MLA_Q_RANK, MLA_HEADS * MLA_QK)
    k1 = w_dkv[:, MLA_KV_RANK:MLA_KV_RANK + half]
    k2 = w_dkv[:, MLA_KV_RANK + half:]
    dkv = jnp.concatenate([w_dkv[:, :MLA_KV_RANK]] + [k1] * 4 + [k2] * 4, axis=-1)
    ukv = w_ukv.reshape(MLA_KV_RANK, MLA_HEADS, MLA_NOPE + MLA_V)
    ukv = jnp.concatenate([ukv[:, :, :MLA_NOPE].reshape(MLA_KV_RANK, -1),
                           ukv[:, :, MLA_NOPE:].reshape(MLA_KV_RANK, -1)], axis=-1)
    return w_dq, uq, dkv, ukv


def kernel(x, c, ctx, c_ctx, ada_w, ada_b, norm1_g, norm2_g, ffn_w1, ffn_w3, ffn_w2, conv_w_in, conv_w, conv_w_out, gqa_wq, gqa_wk, gqa_wv, gqa_q_norm, gqa_k_norm, gqa_wo, mla_w_dq, mla_q_norm, mla_w_uq, mla_w_dkv, mla_kv_norm, mla_w_ukv, mla_wo, final_g):
    assert x.shape == (BATCH, SEQ, D) and ctx.shape == (BATCH, CTX_LEN, D)
    cond = jnp.concatenate(
        [c, c_ctx[None], jnp.zeros((MOD_ROWS - BATCH - 1, D), F32)], axis=0)
    mod_operands = (cond, ada_w, ada_b.reshape(DEPTH, 1, N_MOD * D))
    mods = list(_modulation_tables(*mod_operands, 2).reshape(2, MOD_ROWS, N_MOD, D))

    gqa_cos, gqa_sin = _rope_tables(GQA_HEAD_DIM)
    lane_half = jnp.arange(LANES) // (LANES // 2)
    half_ones = jnp.tile((lane_half[:, None] == lane_half[None, :]).astype(BF16), (2, 1))
    lane_head = jnp.arange(2 * LANES) // LANES
    pair_ones = (jnp.arange(SUBLANES)[:, None] == lane_head[None, :]).astype(BF16)
    mla_cos, mla_sin = _rope_tables(MLA_ROPE)

    ffn_w = (ffn_w1, ffn_w3, ffn_w2)
    conv_stacks = (conv_w_in, conv_w_out)
    conv_bf16 = {0: (conv_w_in[0].astype(BF16), conv_w_out[0].astype(BF16))}
    gqa_wo = gqa_wo.astype(BF16)
    mla_wo = mla_wo.astype(BF16)

    xs = None
    for i in range(DEPTH):
        kind, j = i % 3, i // 3
        mod = mods[i]
        last = i == DEPTH - 1
        n_rows = N_LAT if last else N_ALL
        g1 = norm1_g[i][None]
        g2 = norm2_g[i][None]
        if kind == 0:
            w_in, w_out = conv_bf16[j]
            if i == 0:
                b, u, xs, *ffn = _conv_in_first(x.reshape(N_LAT, D), ctx.reshape(N_CTX, D), mod, g1,
                                                w_in, None, ffn_w, i)
            else:
                b, u, *ffn = _conv_in(xs, mod, g1, w_in, None, n_rows, ffn_w, i)
            xs = _conv_tail(b, u, conv_w[j], w_out, None, xs, mod, g2, *ffn, final_g[None], n_rows, last)
        elif kind == 1:
            w_qkv = jnp.concatenate([_gqa_slab_pairs(gqa_wq[j].astype(BF16), GQA_KV_HEADS, GQA_GROUP),
                                     _gqa_slab_pairs(gqa_wk[j].astype(BF16), GQA_KV_HEADS, 1),
                                     gqa_wv[j].astype(BF16)], axis=-1)
            q_gain = _gqa_slab_pairs(jnp.tile(gqa_q_norm[j], GQA_KV_HEADS)[None], GQA_KV_HEADS, 1)
            k_gain = _gqa_slab_pairs(jnp.tile(gqa_k_norm[j], GQA_KV_HEADS)[None], GQA_KV_HEADS, 1)
            q_gain = q_gain * (GQA_HEAD_DIM ** -0.5 * LOG2E)
            bound = 1.02 * GQA_HEAD_DIM * jnp.max(jnp.abs(q_gain)) * jnp.max(jnp.abs(k_gain))
            q, k, vt = _gqa_proj(xs, mod, g1, w_qkv, q_gain, k_gain, half_ones, gqa_cos, gqa_sin)
            casts = [(w, i) for w in ffn_w] + [(w, N_CONV_LAYERS - 1) for w in conv_stacks]
            a_lat, a_ctx, *cast_out, mod_next = _attention(
                q, k, vt, jnp.full((BATCH,), bound), casts, (i + 1, 2), mod_operands,
                kv_heads=GQA_KV_HEADS, group=GQA_GROUP, dq=GQA_QK, dv=GQA_HEAD_DIM,
                heads_per_step=1, shared_k=True)
            ffn, conv_bf16[N_CONV_LAYERS - 1] = cast_out[:3], tuple(cast_out[3:])
            mods += list(mod_next.reshape(2, MOD_ROWS, N_MOD, D))
            xs = _attn_tail(a_lat, a_ctx, gqa_wo, j, xs, mod, g2, *ffn)
        else:
            w_dq, w_uq, w_dkv, w_ukv = _mla_weights(mla_w_dq[j], mla_w_uq[j], mla_w_dkv[j], mla_w_ukv[j])
            q_gain = mla_q_norm[j][None] * ((MLA_NOPE + MLA_ROPE) ** -0.5 * LOG2E)
            q, k, vt, qn, kn = _mla_proj(xs, mod, g1, w_dq, q_gain, w_uq, w_dkv,
                                         mla_kv_norm[j][None], w_ukv, pair_ones, mla_cos, mla_sin)
            a_lat, a_ctx, *ffn = _attention(
                q, k, vt, _score_bounds(qn, kn), [(w, i) for w in ffn_w], None, mod_operands,
                kv_heads=MLA_HEADS, group=1, dq=MLA_QK, dv=MLA_V, heads_per_step=4, shared_k=False)
            xs = _attn_tail(a_lat, a_ctx, mla_wo, j, xs, mod, g2, *ffn)

    assert (DEPTH - 1) % 3 == 0
    return xs.reshape(BATCH, SEQ, D)
```

```python
import functools

import jax
import jax.numpy as jnp
from jax import lax
from jax.experimental import pallas as pl
from jax.experimental.pallas import tpu as pltpu

D = 1024
BATCH = 8
SEQ = 2048
DEPTH = 4
GRID_W = 64
CTX_LEN = 256
N_MOD = 6
FFN_HIDDEN = 2816
ROPE_THETA = 10000.0
EPS = 1e-6
GQA_HEAD_DIM = 128
GQA_HEADS = 8
GQA_KV_HEADS = 2
GQA_GROUP = 4
MLA_HEADS = 8
MLA_NOPE = 128
MLA_ROPE = 64
MLA_V = 128
MLA_KV_RANK = 256
MLA_Q_RANK = 768
MLA_QK = 256
GQA_QK = 256
LANES = 128
SUBLANES = 8
VT_ROWS = 128

N_LAT = BATCH * SEQ
N_CTX = BATCH * CTX_LEN
N_ALL = N_LAT + N_CTX
MOD_ROWS = 16
N_CONV_LAYERS = (DEPTH + 2) // 3

TM = 512
TP = 1024
TQ = 1024
HALO = SUBLANES
CONV_COLS = 256
FFN_SPLITS = 2
CAST_CHUNKS = 16
MOD_COLS = 3072
VMEM_LIMIT = 56 * 1024 * 1024
LOG2E = 1.4426950408889634
MAX_BOUND_SHIFT = 50.0

F32 = jnp.float32
BF16 = jnp.bfloat16


def _params(n_axes):
    return pltpu.CompilerParams(
        dimension_semantics=("arbitrary",) * n_axes, vmem_limit_bytes=VMEM_LIMIT)


def _resident(shape, layer=None):
    nd = len(shape)
    if layer is None:
        return pl.BlockSpec(shape, lambda *_: (0,) * nd, pipeline_mode=pl.Buffered(1))
    return pl.BlockSpec((1,) + tuple(shape), lambda *_: (layer,) + (0,) * nd, pipeline_mode=pl.Buffered(1))


def _w(ref):
    return ref[0] if len(ref.shape) == 3 else ref[...]


def _cast_riders(stacks, chunk_of):
    in_specs, out_specs, out_shapes = [], [], []
    for w, layer in stacks:
        _, rows, cols = w.shape
        r = rows // CAST_CHUNKS
        in_specs.append(pl.BlockSpec((1, r, cols), lambda *g, layer=layer: (layer, chunk_of(*g), 0)))
        out_specs.append(pl.BlockSpec((r, cols), lambda *g: (chunk_of(*g), 0)))
        out_shapes.append(jax.ShapeDtypeStruct((rows, cols), BF16))
    return in_specs, out_specs, out_shapes


def _mod_rider(first_layer, n_layers, chunk_of):
    tn = N_MOD * D // CAST_CHUNKS
    assert first_layer % n_layers == 0
    blk = first_layer // n_layers
    in_specs = [pl.BlockSpec((MOD_ROWS, D), lambda *g: (0, 0)),
                pl.BlockSpec((n_layers, D, tn), lambda *g: (blk, 0, chunk_of(*g))),
                pl.BlockSpec((n_layers, 1, tn), lambda *g: (blk, 0, chunk_of(*g)))]
    out_spec = pl.BlockSpec((n_layers, MOD_ROWS, tn), lambda *g: (0, 0, chunk_of(*g)))
    return in_specs, out_spec, jax.ShapeDtypeStruct((n_layers, MOD_ROWS, N_MOD * D), F32)


def _with_riders(body, n_in, n_out, mod_rider=False):
    n_mod_in, n_mod_out = (3, 1) if mod_rider else (0, 0)

    def kern(*refs):
        n_casts = (len(refs) - n_in - n_out - n_mod_in - n_mod_out) // 2
        ins, cast_in = refs[:n_in], refs[n_in:n_in + n_casts]
        mod_in = refs[n_in + n_casts:n_in + n_casts + n_mod_in]
        o0 = n_in + n_casts + n_mod_in
        outs, cast_out = refs[o0:o0 + n_out], refs[o0 + n_out:o0 + n_out + n_casts]
        mod_out = refs[o0 + n_out + n_casts:]
        body(*ins, *outs)
        for src, dst in zip(cast_in, cast_out):
            dst[...] = src[0].astype(dst.dtype)
        if mod_rider:
            _mod_kernel(*mod_in, *mod_out)
    return kern


def _mod_spec(tm):
    return pl.BlockSpec((1, N_MOD, D), lambda i: (jnp.minimum(i * tm // SEQ, BATCH), 0, 0))


def _row_spec(tm, width):
    return pl.BlockSpec((tm, width), lambda i: (i, 0))


def _col_spec(height, tm):
    return pl.BlockSpec((height, tm), lambda i: (0, i))


def _dot(a, b):
    return jnp.dot(a, b, preferred_element_type=F32)


def _modulated_norm(x, gain, shift, scale):
    r = lax.rsqrt(jnp.mean(x * x, axis=-1, keepdims=True) + EPS)
    return (x * r) * (gain * (1.0 + scale)) + shift


def _head_norm(x, gain):
    r = lax.rsqrt(jnp.mean(x * x, axis=-1, keepdims=True) + EPS)
    return (x * r) * gain


def _store_vt(vt_ref, v, n_heads, tokens=slice(None)):
    for j in range(n_heads):
        vt_ref[j * VT_ROWS:(j + 1) * VT_ROWS, tokens] = v[:, j * VT_ROWS:(j + 1) * VT_ROWS].T.astype(BF16)


def _rope(a, b, cos, sin):
    return a * cos - b * sin, a * sin + b * cos


def _lane_mask(lo, hi):
    lane = lax.broadcasted_iota(jnp.int32, (1, LANES), 1)
    return (lane >= lo) & (lane < hi)


def _mod_kernel(cond_ref, w_ref, b_ref, o_ref):
    c = cond_ref[...]
    act = (c * jax.nn.sigmoid(c)).astype(BF16)
    for l in range(w_ref.shape[0]):
        o_ref[l] = _dot(act, w_ref[l].astype(BF16)) + b_ref[l]


def _modulation_tables(cond, ada_w, ada_b, n_layers):
    tn = MOD_COLS
    return pl.pallas_call(
        _mod_kernel,
        grid=(n_layers, N_MOD * D // tn),
        in_specs=[
            pl.BlockSpec((MOD_ROWS, D), lambda l, j: (0, 0)),
            pl.BlockSpec((1, D, tn), lambda l, j: (l, 0, j)),
            pl.BlockSpec((1, 1, tn), lambda l, j: (l, 0, j)),
        ],
        out_specs=pl.BlockSpec((1, MOD_ROWS, tn), lambda l, j: (l, 0, j)),
        out_shape=jax.ShapeDtypeStruct((n_layers, MOD_ROWS, N_MOD * D), F32),
        compiler_params=_params(2),
        name="modulation",
    )(cond, ada_w, ada_b)


def _conv_in_compute(x, mod_ref, g_ref, w_ref, b_ref, u_ref):
    m = mod_ref[0]
    h = _modulated_norm(x, g_ref[...], m[0:1], m[1:2]).astype(BF16)
    y = _dot(h, _w(w_ref))
    b_ref[...] = y[:, :D].astype(BF16)
    u_ref[...] = y[:, D:2 * D] * y[:, 2 * D:]


def _conv_in_kernel(x_ref, mod_ref, g_ref, w_ref, b_ref, u_ref):
    _conv_in_compute(x_ref[...], mod_ref, g_ref, w_ref, b_ref, u_ref)


def _conv_in_first_kernel(lat_ref, ctx_ref, mod_ref, g_ref, w_ref, b_ref, u_ref, xs_ref):
    x = jnp.where(pl.program_id(0) < N_LAT // TP, lat_ref[...], ctx_ref[...])
    xs_ref[...] = x
    _conv_in_compute(x, mod_ref, g_ref, w_ref, b_ref, u_ref)


def _conv_in(x, mod, gain, w_in, layer, n_rows, ffn_w, ffn_layer):
    r_in, r_out, r_shape = _cast_riders([(w, ffn_layer) for w in ffn_w], lambda i: jnp.minimum(i, CAST_CHUNKS - 1))
    assert n_rows // TP >= CAST_CHUNKS
    return pl.pallas_call(
        _with_riders(_conv_in_kernel, 4, 2),
        grid=(n_rows // TP,),
        in_specs=[_row_spec(TP, D), _mod_spec(TP), _resident((1, D)), _resident((D, 3 * D), layer)] + r_in,
        out_specs=[_row_spec(TP, D), _row_spec(TP, D)] + r_out,
        out_shape=[jax.ShapeDtypeStruct((n_rows, D), BF16), jax.ShapeDtypeStruct((n_rows, D), F32)] + r_shape,
        compiler_params=_params(1),
        name="conv_in",
    )(x, mod, gain, w_in, *ffn_w)


def _conv_in_first(lat, ctx, mod, gain, w_in, layer, ffn_w, ffn_layer):
    lat_tiles = N_LAT // TP
    r_in, r_out, r_shape = _cast_riders([(w, ffn_layer) for w in ffn_w], lambda i: jnp.minimum(i, CAST_CHUNKS - 1))
    assert N_ALL // TP >= CAST_CHUNKS
    return pl.pallas_call(
        _with_riders(_conv_in_first_kernel, 5, 3),
        grid=(N_ALL // TP,),
        in_specs=[
            pl.BlockSpec((TP, D), lambda i: (jnp.minimum(i, lat_tiles - 1), 0)),
            pl.BlockSpec((TP, D), lambda i: (jnp.maximum(i - lat_tiles, 0), 0)),
            _mod_spec(TP), _resident((1, D)), _resident((D, 3 * D), layer),
        ] + r_in,
        out_specs=[_row_spec(TP, D), _row_spec(TP, D), _row_spec(TP, D)] + r_out,
        out_shape=[jax.ShapeDtypeStruct((N_ALL, D), BF16), jax.ShapeDtypeStruct((N_ALL, D), F32),
                   jax.ShapeDtypeStruct((N_ALL, D), F32)] + r_shape,
        compiler_params=_params(1),
        name="conv_in_first",
    )(lat, ctx, mod, gain, w_in, *ffn_w)


def _conv_gate_exact(b_ref, u_ref, up_ref, un_ref, cw_ref, r0, r1, cols):
    tm = u_ref.shape[0]
    n = r1 - r0
    u = u_ref[r0:r1, cols]
    before = up_ref[HALO - 1:HALO, cols] if r0 == 0 else u_ref[r0 - 1:r0, cols]
    after = un_ref[0:1, cols] if r1 == tm else u_ref[r1:r1 + 1, cols]
    local = lax.broadcasted_iota(jnp.int32, (n, 1), 0)
    row = local + (pl.program_id(0) * tm + r0)
    in_ctx = row >= N_LAT
    first = ((row & (CTX_LEN - 1)) == 0) & (in_ctx | ((row & (SEQ - 1)) == 0))
    last = ((row & (CTX_LEN - 1)) == CTX_LEN - 1) & (in_ctx | ((row & (SEQ - 1)) == SEQ - 1))
    prev = jnp.where(local == 0, before, pltpu.roll(u, 1, axis=0))
    prev = jnp.where(first, 0.0, prev)
    nxt = jnp.where(local == n - 1, after, pltpu.roll(u, n - 1, axis=0))
    nxt = jnp.where(last, 0.0, nxt)
    cw = cw_ref[:, cols]
    z = prev * cw[0:1] + u * cw[1:2] + nxt * cw[2:3]
    return (b_ref[r0:r1, cols].astype(F32) * z).astype(BF16)


def _conv_mix(b_ref, u_ref, up_ref, un_ref, cw_ref, wo_ref, bz_ref):
    tm = u_ref.shape[0]
    group = 2 * HALO
    edges = [(0, group), (tm - group, tm)]
    edges += [(r - group, r + group) for r in range(CTX_LEN, tm, CTX_LEN)]
    wo = wo_ref.at[0] if len(wo_ref.shape) == 3 else wo_ref
    acc = None
    for c0 in range(0, D, CONV_COLS):
        cols = slice(c0, c0 + CONV_COLS)
        u = u_ref[:, cols]
        cw = cw_ref[:, cols]
        z = pltpu.roll(u, 1, axis=0) * cw[0:1] + u * cw[1:2] + pltpu.roll(u, tm - 1, axis=0) * cw[2:3]
        bz_ref[:, cols] = (b_ref[:, cols].astype(F32) * z).astype(BF16)
        for r0, r1 in edges:
            bz_ref[r0:r1, cols] = _conv_gate_exact(b_ref, u_ref, up_ref, un_ref, cw_ref, r0, r1, cols)
        part = _dot(bz_ref[:, cols], wo[cols, :])
        acc = part if acc is None else acc + part
    return acc


def _gqa_proj_kernel(x_ref, mod_ref, g_ref, w_ref, qg_ref, kg_ref, ones_ref, cos_ref, sin_ref,
                     q_ref, k_ref, vt_ref):
    m = mod_ref[0]
    ones2 = ones_ref[...]
    wide = 2 * GQA_QK

    def project(rows):
        h = _modulated_norm(x_ref[rows, :], g_ref[...], m[0:1], m[1:2]).astype(BF16)
        cos = cos_ref[rows, :]
        sin = sin_ref[rows, :]

        def normed_rotated(y, gain):
            a, b = y[:, :LANES], y[:, LANES:]
            ss = a * a + b * b
            hi = ss.astype(BF16)
            lo = (ss - hi.astype(F32)).astype(BF16)
            tot = _dot(jnp.concatenate([hi, lo], axis=1), ones2)
            r = lax.rsqrt(tot * (1.0 / GQA_HEAD_DIM) + EPS)
            return _rope(a * r * gain[:, :LANES], b * r * gain[:, LANES:], cos, sin)

        for gg in range(GQA_GROUP // 2):
            y = _dot(h, w_ref[:, gg * wide:(gg + 1) * wide])
            for t in range(2):
                g = 2 * gg + t
                o1, o2 = normed_rotated(y[:, t * GQA_QK:(t + 1) * GQA_QK], qg_ref[...])
                q_ref[rows, g * GQA_QK:g * GQA_QK + LANES] = o1.astype(BF16)
                q_ref[rows, g * GQA_QK + LANES:(g + 1) * GQA_QK] = o2.astype(BF16)
        y = _dot(h, w_ref[:, GQA_GROUP * GQA_QK:])
        o1, o2 = normed_rotated(y[:, :GQA_QK], kg_ref[...])
        k_ref[rows, :LANES] = o1.astype(BF16)
        k_ref[rows, LANES:] = o2.astype(BF16)
        _store_vt(vt_ref, y[:, GQA_QK:], GQA_KV_HEADS, rows)

    rows = x_ref.shape[0] // 2
    project(slice(0, rows))
    project(slice(rows, 2 * rows))


def _rope_spec(tm):
    per_seq = SEQ // tm
    return pl.BlockSpec(
        (tm, LANES), lambda i: (jnp.where(i < N_LAT // tm, i % per_seq, per_seq), 0))


def _gqa_proj(x, mod, gain, w_qkv, q_gain, k_gain, half_ones, cos, sin):
    kvw = GQA_KV_HEADS * GQA_HEAD_DIM
    qw = GQA_GROUP * GQA_QK
    return pl.pallas_call(
        _gqa_proj_kernel,
        grid=(N_ALL // TP,),
        in_specs=[
            _row_spec(TP, D), _mod_spec(TP), _resident((1, D)),
            _resident((D, D + 2 * kvw)),
            _resident((1, GQA_QK)), _resident((1, GQA_QK)), _resident((2 * LANES, LANES)),
            _rope_spec(TP), _rope_spec(TP),
        ],
        out_specs=[_row_spec(TP, qw), _row_spec(TP, GQA_QK), _col_spec(GQA_KV_HEADS * VT_ROWS, TP)],
        out_shape=[jax.ShapeDtypeStruct((N_ALL, qw), BF16),
                   jax.ShapeDtypeStruct((N_ALL, GQA_QK), BF16),
                   jax.ShapeDtypeStruct((GQA_KV_HEADS * VT_ROWS, N_ALL), BF16)],
        compiler_params=_params(1),
        name="gqa_proj",
    )(x, mod, gain, w_qkv, q_gain, k_gain, half_ones, cos, sin)


def _mla_proj_kernel(x_ref, mod_ref, g_ref, wdq_ref, qg_ref, wuq_ref, wdkv_ref, kvg_ref, wukv_ref,
                     ones_ref, cos_ref, sin_ref, q_ref, k_ref, vt_ref, qn_ref, kn_ref):
    m = mod_ref[0]
    half = MLA_ROPE // 2
    ones2 = ones_ref[...]

    def max_sq_norm(sq_pair):
        sums = _nt_dot(ones2, jnp.concatenate(sq_pair, axis=1).astype(BF16))
        top = jnp.max(sums, axis=1, keepdims=True)
        lane = lax.broadcasted_iota(jnp.int32, (1, 2 * LANES), 1)
        return jnp.where(lane < LANES, top[0:1], top[1:2])

    def project(rows):
        h = _modulated_norm(x_ref[rows, :], g_ref[...], m[0:1], m[1:2]).astype(BF16)
        cos = cos_ref[rows, :]
        sin = sin_ref[rows, :]
        cq = _head_norm(_dot(h, wdq_ref[...]), qg_ref[...]).astype(BF16)
        q = _dot(cq, wuq_ref[...])
        ckv_pe = _dot(h, wdkv_ref[...])
        ckv = _head_norm(ckv_pe[:, :MLA_KV_RANK], kvg_ref[...]).astype(BF16)
        ko1, ko2 = _rope(ckv_pe[:, MLA_KV_RANK:MLA_KV_RANK + LANES], ckv_pe[:, MLA_KV_RANK + LANES:], cos, sin)
        k_pe = jnp.where(_lane_mask(0, 2 * half), ko1, ko2)
        k_pe_sq = k_pe * k_pe
        k_pe = k_pe.astype(BF16)
        kv = _dot(ckv, wukv_ref[...])
        q_norms, k_norms = [], []
        for pp in range(MLA_HEADS // 2):
            p0 = pp * 2 * MLA_QK
            o1, o2 = _rope(q[:, p0 + 2 * MLA_NOPE:p0 + 2 * MLA_NOPE + LANES],
                           q[:, p0 + 2 * MLA_NOPE + LANES:p0 + 2 * MLA_QK], cos, sin)
            q_sq, k_sq = [], []
            for t in range(2):
                j = 2 * pp + t
                c0 = j * MLA_QK
                pe = jnp.where(_lane_mask(t * half, (t + 1) * half), o1,
                               jnp.where(_lane_mask(2 * half + t * half, 2 * half + (t + 1) * half), o2, 0.0))
                q_nope = q[:, p0 + t * MLA_NOPE:p0 + (t + 1) * MLA_NOPE]
                k_nope = kv[:, j * MLA_NOPE:(j + 1) * MLA_NOPE]
                q_ref[rows, c0:c0 + MLA_NOPE] = q_nope.astype(BF16)
                q_ref[rows, c0 + MLA_NOPE:c0 + MLA_QK] = pe.astype(BF16)
                k_ref[rows, c0:c0 + MLA_NOPE] = k_nope.astype(BF16)
                k_ref[rows, c0 + MLA_NOPE:c0 + MLA_QK] = k_pe
                q_sq.append(q_nope * q_nope + pe * pe)
                k_sq.append(k_nope * k_nope + k_pe_sq)
            q_norms.append(max_sq_norm(q_sq))
            k_norms.append(max_sq_norm(k_sq))
        _store_vt(vt_ref, kv[:, MLA_HEADS * MLA_NOPE:], MLA_HEADS, rows)
        return jnp.concatenate(q_norms, axis=1), jnp.concatenate(k_norms, axis=1)

    rows = x_ref.shape[0] // 2
    (qa, ka), (qb, kb) = project(slice(0, rows)), project(slice(rows, 2 * rows))
    qn_ref[0] = jnp.broadcast_to(jnp.maximum(qa, qb), qn_ref.shape[1:])
    kn_ref[0] = jnp.broadcast_to(jnp.maximum(ka, kb), kn_ref.shape[1:])


def _mla_proj(x, mod, gain, w_dq, q_gain, w_uq, w_dkv, kv_gain, w_ukv, pair_ones, cos, sin):
    qkw = MLA_HEADS * MLA_QK
    n_tiles = N_ALL // TP
    norm_spec = pl.BlockSpec((1, SUBLANES, MLA_HEADS * LANES), lambda i: (i, 0, 0))
    norm_shape = jax.ShapeDtypeStruct((n_tiles, SUBLANES, MLA_HEADS * LANES), F32)
    return pl.pallas_call(
        _mla_proj_kernel,
        grid=(N_ALL // TP,),
        in_specs=[
            _row_spec(TP, D), _mod_spec(TP), _resident((1, D)),
            _resident((D, MLA_Q_RANK)), _resident((1, MLA_Q_RANK)), _resident((MLA_Q_RANK, qkw)),
            _resident((D, MLA_KV_RANK + 2 * LANES)), _resident((1, MLA_KV_RANK)),
            _resident((MLA_KV_RANK, MLA_HEADS * (MLA_NOPE + MLA_V))),
            _resident((SUBLANES, 2 * LANES)),
            _rope_spec(TP), _rope_spec(TP),
        ],
        out_specs=[_row_spec(TP, qkw), _row_spec(TP, qkw), _col_spec(MLA_HEADS * VT_ROWS, TP),
                   norm_spec, norm_spec],
        out_shape=[jax.ShapeDtypeStruct((N_ALL, qkw), BF16),
                   jax.ShapeDtypeStruct((N_ALL, qkw), BF16),
                   jax.ShapeDtypeStruct((MLA_HEADS * VT_ROWS, N_ALL), BF16),
                   norm_shape, norm_shape],
        compiler_params=_params(1),
        name="mla_proj",
    )(x, mod, gain, w_dq, q_gain, w_uq, w_dkv, kv_gain, w_ukv, pair_ones, cos, sin)


def _score_bounds(qn, kn):
    qn = qn[:, 0, ::LANES]
    kn = kn[:, 0, ::LANES]
    per_seq = SEQ // TP
    q_lat = jnp.max(qn[:N_LAT // TP].reshape(BATCH, per_seq, -1), axis=1)
    k_lat = jnp.max(kn[:N_LAT // TP].reshape(BATCH, per_seq, -1), axis=1)
    ctx_tile = N_LAT // TP + (jnp.arange(BATCH) * CTX_LEN) // TP
    k_all = jnp.maximum(k_lat, kn[ctx_tile])
    return 1.02 * jnp.sqrt(jnp.max(q_lat * k_all, axis=1))


def _nt_dot(a, b):
    return lax.dot_general(a, b, (((1,), (1,)), ((), ())), preferred_element_type=F32)


def _attn_scores(q, k_parts):
    return [_nt_dot(k, q) for k in k_parts]


def _attn_kernel(bound_ref, q_ref, qc_ref, kl_ref, kc_ref, vtl_ref, vtc_ref, o_ref, oc_ref,
                 *, heads, group, dq, dv, shared_k):
    n = heads * group

    def k_cols(qh):
        kv = 0 if shared_k else qh // group
        return slice(kv * dq, (kv + 1) * dq)

    def v_rows(qh):
        return slice(qh // group * VT_ROWS, (qh // group + 1) * VT_ROWS)

    def run(queries, out_ref, k_refs, vt_refs, shift=None):
        def scores(qh):
            if shared_k:
                g = qh % group
                kv = pl.program_id(1) * heads + qh // group
                q = queries[:, g * dq:(g + 1) * dq]
                lane = lax.broadcasted_iota(jnp.int32, (1, dq), 1)
                q = jnp.where(((lane // (LANES // 2)) & 1) == kv, q, jnp.zeros_like(q))
            else:
                q = queries[:, qh * dq:(qh + 1) * dq]
            return _attn_scores(q, [k[:, k_cols(qh)] for k in k_refs])

        ahead = 2 if shift is None else 1
        pending = [scores(qh) for qh in range(min(ahead, n))]
        for qh in range(n):
            cur = pending.pop(0)
            if shift is None:
                mx = functools.reduce(jnp.maximum, [jnp.max(s, axis=0, keepdims=True) for s in cur])
            else:
                mx = shift
            probs = [jnp.exp2(s - mx) for s in cur]
            den = sum(jnp.sum(p, axis=0, keepdims=True) for p in probs)
            probs = [p.astype(BF16) for p in probs]
            if qh + ahead < n:
                pending.append(scores(qh + ahead))
            acc = sum(_dot(vt[v_rows(qh), :], p) for vt, p in zip(vt_refs, probs))
            out_ref[:, qh * dv:(qh + 1) * dv] = (acc / den).T.astype(BF16)

    lat = (q_ref, o_ref, [kl_ref, kc_ref], [vtl_ref, vtc_ref])
    bound = bound_ref[0, pl.program_id(0)]

    @pl.when(bound <= MAX_BOUND_SHIFT)
    def _():
        run(*lat, shift=bound)

    @pl.when(jnp.logical_not(bound <= MAX_BOUND_SHIFT))
    def _():
        run(*lat)

    @pl.when(pl.program_id(2) == 0)
    def _():
        run(qc_ref, oc_ref, [kc_ref], [vtc_ref])


def _attention(q, k, vt, score_bounds, casts, mod_layers, mod_operands, *, kv_heads, group, dq, dv,
               heads_per_step, shared_k):
    hp = heads_per_step
    assert hp == 1 or not shared_k
    lat_tiles = SEQ // TQ
    ctx_block0 = N_LAT // CTX_LEN
    kw = dq if shared_k else hp * dq
    k_col = (lambda h: 0) if shared_k else (lambda h: h)
    width = kv_heads * group * dv
    h_steps = kv_heads // hp
    n_steps = BATCH * h_steps * lat_tiles
    assert n_steps >= CAST_CHUNKS
    def chunk_of(b, h, t):
        return ((b * h_steps + h) * lat_tiles + t) * CAST_CHUNKS // n_steps

    r_in, r_out, r_shape = _cast_riders(casts, chunk_of)
    rider_operands = [w for w, _ in casts]
    if mod_layers is not None:
        m_in, m_out, m_shape = _mod_rider(*mod_layers, chunk_of)
        r_in, r_out, r_shape = r_in + m_in, r_out + [m_out], r_shape + [m_shape]
        rider_operands += list(mod_operands)
    body = functools.partial(_attn_kernel, heads=hp, group=group, dq=dq, dv=dv, shared_k=shared_k)
    return pl.pallas_call(
        _with_riders(body, 7, 2, mod_rider=mod_layers is not None),
        grid=(BATCH, h_steps, lat_tiles),
        in_specs=[
            pl.BlockSpec(memory_space=pltpu.SMEM),
            pl.BlockSpec((TQ, hp * group * dq), lambda b, h, t: (b * lat_tiles + t, k_col(h))),
            pl.BlockSpec((CTX_LEN, hp * group * dq), lambda b, h, t: (ctx_block0 + b, k_col(h))),
            pl.BlockSpec((SEQ, kw), lambda b, h, t: (b, k_col(h))),
            pl.BlockSpec((CTX_LEN, kw), lambda b, h, t: (ctx_block0 + b, k_col(h))),
            pl.BlockSpec((hp * VT_ROWS, SEQ), lambda b, h, t: (h, b)),
            pl.BlockSpec((hp * VT_ROWS, CTX_LEN), lambda b, h, t: (h, ctx_block0 + b)),
        ] + r_in,
        out_specs=[pl.BlockSpec((TQ, hp * group * dv), lambda b, h, t: (b * lat_tiles + t, h)),
                   pl.BlockSpec((CTX_LEN, hp * group * dv), lambda b, h, t: (b, h))] + r_out,
        out_shape=[jax.ShapeDtypeStruct((N_LAT, width), BF16), jax.ShapeDtypeStruct((N_CTX, width), BF16)] + r_shape,
        compiler_params=_params(3),
        name="attention",
    )(score_bounds.reshape(1, BATCH).astype(F32), q, q, k, k, vt, vt, *rider_operands)


def _ffn_tail(x1, m, g2_ref, w1_ref, w3_ref, w2_ref):
    rows = x1.shape[0] // FFN_SPLITS
    parts = [x1[i * rows:(i + 1) * rows] for i in range(FFN_SPLITS)]

    def up(x):
        h = _modulated_norm(x, g2_ref[...], m[3:4], m[4:5]).astype(BF16)
        return _dot(h, _w(w1_ref)), _dot(h, _w(w3_ref))

    def down(x, ab):
        a, b = ab
        gated = (a * jax.nn.sigmoid(a) * b).astype(BF16)
        return x + m[5:6] * _dot(gated, _w(w2_ref))

    outs = []
    nxt = up(parts[0])
    for i, x in enumerate(parts):
        cur = nxt
        if i + 1 < FFN_SPLITS:
            nxt = up(parts[i + 1])
        outs.append(down(x, cur))
    return jnp.concatenate(outs, axis=0)


def _attn_tail_kernel(al_ref, ac_ref, wo_ref, x_ref, mod_ref, g2_ref, w1_ref, w3_ref, w2_ref, o_ref):
    m = mod_ref[0]
    a = jnp.where(pl.program_id(0) < N_LAT // TM, al_ref[...], ac_ref[...])
    x1 = x_ref[...] + m[2:3] * _dot(a, _w(wo_ref))
    o_ref[...] = _ffn_tail(x1, m, g2_ref, w1_ref, w3_ref, w2_ref)


def _conv_tail_kernel(b_ref, u_ref, up_ref, un_ref, cw_ref, wo_ref, x_ref, mod_ref,
                      g2_ref, w1_ref, w3_ref, w2_ref, fg_ref, o_ref, bz_ref, *, final):
    m = mod_ref[0]
    x1 = x_ref[...] + m[2:3] * _conv_mix(b_ref, u_ref, up_ref, un_ref, cw_ref, wo_ref, bz_ref)
    x2 = _ffn_tail(x1, m, g2_ref, w1_ref, w3_ref, w2_ref)
    o_ref[...] = _head_norm(x2, fg_ref[...]) if final else x2


def _ffn_specs():
    return [_mod_spec(TM), _resident((1, D)), _resident((D, FFN_HIDDEN)),
            _resident((D, FFN_HIDDEN)), _resident((FFN_HIDDEN, D))]


def _attn_tail(a_lat, a_ctx, w_o, mixer_layer, x, mod, g2, w1, w3, w2):
    lat_tiles = N_LAT // TM
    return pl.pallas_call(
        _attn_tail_kernel,
        grid=(N_ALL // TM,),
        in_specs=[
            pl.BlockSpec((TM, D), lambda i: (jnp.minimum(i, lat_tiles - 1), 0)),
            pl.BlockSpec((TM, D), lambda i: (jnp.maximum(i - lat_tiles, 0), 0)),
            _resident((D, D), mixer_layer), _row_spec(TM, D),
        ] + _ffn_specs(),
        out_specs=_row_spec(TM, D),
        out_shape=jax.ShapeDtypeStruct((N_ALL, D), F32),
        compiler_params=_params(1),
        name="attn_tail",
    )(a_lat, a_ctx, w_o, x, mod, g2, w1, w3, w2)


def _conv_tail(b, u, conv_w, w_out, mixer_layer, x, mod, g2, w1, w3, w2, final_gain, n_rows, final):
    per = TM // HALO
    n_halo = n_rows // HALO
    return pl.pallas_call(
        functools.partial(_conv_tail_kernel, final=final),
        grid=(n_rows // TM,),
        in_specs=[
            _row_spec(TM, D), _row_spec(TM, D),
            pl.BlockSpec((HALO, D), lambda i: (jnp.maximum(i * per - 1, 0), 0)),
            pl.BlockSpec((HALO, D), lambda i: (jnp.minimum((i + 1) * per, n_halo - 1), 0)),
            _resident((3, D)), _resident((D, D), mixer_layer), _row_spec(TM, D),
        ] + _ffn_specs() + [_resident((1, D))],
        out_specs=_row_spec(TM, D),
        out_shape=jax.ShapeDtypeStruct((n_rows, D), F32),
        scratch_shapes=[pltpu.VMEM((TM, D), BF16)],
        compiler_params=_params(1),
        name="conv_tail",
    )(b, u, u, u, conv_w, w_out, x, mod, g2, w1, w3, w2, final_gain)


def _axial_angles(rot_dim):
    n = rot_dim // 4
    rows = jnp.repeat(jnp.arange(SEQ // GRID_W, dtype=F32), GRID_W)
    cols = jnp.tile(jnp.arange(GRID_W, dtype=F32), SEQ // GRID_W)
    freqs = ROPE_THETA ** (-jnp.arange(n, dtype=F32) / n)
    return jnp.concatenate([rows[:, None] * freqs, cols[:, None] * freqs], axis=-1)


def _rope_tables(rot_dim):
    ang = _axial_angles(rot_dim)
    copies = LANES // (rot_dim // 2)
    cos_t = jnp.concatenate([jnp.tile(jnp.cos(ang), (1, copies)), jnp.ones((TP, LANES), F32)], axis=0)
    sin_t = jnp.concatenate([jnp.tile(jnp.sin(ang), (1, copies)), jnp.zeros((TP, LANES), F32)], axis=0)
    return cos_t, sin_t


def _gqa_slab_pairs(w, n_kv, n_group):
    half = GQA_HEAD_DIM // 2
    w = w.reshape(w.shape[0], n_kv, n_group, 2, half)
    return jnp.transpose(w, (0, 2, 3, 1, 4)).reshape(w.shape[0], n_group * GQA_QK)


def _mla_weights(w_dq, w_uq, w_dkv, w_ukv):
    half = MLA_ROPE // 2
    qk = MLA_NOPE + MLA_ROPE
    w_dq, w_uq, w_dkv, w_ukv = (w.astype(BF16) for w in (w_dq, w_uq, w_dkv, w_ukv))
    uq = w_uq.reshape(MLA_Q_RANK, MLA_HEADS // 2, 2, qk)
    nope = uq[..., :MLA_NOPE].reshape(MLA_Q_RANK, MLA_HEADS // 2, 2 * MLA_NOPE)
    x1 = uq[..., MLA_NOPE:MLA_NOPE + half].reshape(MLA_Q_RANK, MLA_HEADS // 2, 2 * half)
    x2 = uq[..., MLA_NOPE + half:].reshape(MLA_Q_RANK, MLA_HEADS // 2, 2 * half)
    uq = jnp.concatenate([nope, x1, x1, x2, x2], axis=-1).reshape(MLA_Q_RANK, MLA_HEADS * MLA_QK)
    k1 = w_dkv[:, MLA_KV_RANK:MLA_KV_RANK + half]
    k2 = w_dkv[:, MLA_KV_RANK + half:]
    dkv = jnp.concatenate([w_dkv[:, :MLA_KV_RANK]] + [k1] * 4 + [k2] * 4, axis=-1)
    ukv = w_ukv.reshape(MLA_KV_RANK, MLA_HEADS, MLA_NOPE + MLA_V)
    ukv = jnp.concatenate([ukv[:, :, :MLA_NOPE].reshape(MLA_KV_RANK, -1),
                           ukv[:, :, MLA_NOPE:].reshape(MLA_KV_RANK, -1)], axis=-1)
    return w_dq, uq, dkv, ukv


def kernel(x, c, ctx, c_ctx, ada_w, ada_b, norm1_g, norm2_g, ffn_w1, ffn_w3, ffn_w2, conv_w_in, conv_w, conv_w_out, gqa_wq, gqa_wk, gqa_wv, gqa_q_norm, gqa_k_norm, gqa_wo, mla_w_dq, mla_q_norm, mla_w_uq, mla_w_dkv, mla_kv_norm, mla_w_ukv, mla_wo, final_g):
    assert x.shape == (BATCH, SEQ, D) and ctx.shape == (BATCH, CTX_LEN, D)
    cond = jnp.concatenate(
        [c, c_ctx[None], jnp.zeros((MOD_ROWS - BATCH - 1, D), F32)], axis=0)
    mod_operands = (cond, ada_w, ada_b.reshape(DEPTH, 1, N_MOD * D))
    mods = list(_modulation_tables(*mod_operands, 2).reshape(2, MOD_ROWS, N_MOD, D))

    gqa_cos, gqa_sin = _rope_tables(GQA_HEAD_DIM)
    lane_half = jnp.arange(LANES) // (LANES // 2)
    half_ones = jnp.tile((lane_half[:, None] == lane_half[None, :]).astype(BF16), (2, 1))
    lane_head = jnp.arange(2 * LANES) // LANES
    pair_ones = (jnp.arange(SUBLANES)[:, None] == lane_head[None, :]).astype(BF16)
    mla_cos, mla_sin = _rope_tables(MLA_ROPE)

    ffn_w = (ffn_w1, ffn_w3, ffn_w2)
    conv_stacks = (conv_w_in, conv_w_out)
    conv_bf16 = {0: (conv_w_in[0].astype(BF16), conv_w_out[0].astype(BF16))}
    gqa_wo = gqa_wo.astype(BF16)
    mla_wo = mla_wo.astype(BF16)

    xs = None
    for i in range(DEPTH):
        kind, j = i % 3, i // 3
        mod = mods[i]
        last = i == DEPTH - 1
        n_rows = N_LAT if last else N_ALL
        g1 = norm1_g[i][None]
        g2 = norm2_g[i][None]
        if kind == 0:
            w_in, w_out = conv_bf16[j]
            if i == 0:
                b, u, xs, *ffn = _conv_in_first(x.reshape(N_LAT, D), ctx.reshape(N_CTX, D), mod, g1,
                                                w_in, None, ffn_w, i)
            else:
                b, u, *ffn = _conv_in(xs, mod, g1, w_in, None, n_rows, ffn_w, i)
            xs = _conv_tail(b, u, conv_w[j], w_out, None, xs, mod, g2, *ffn, final_g[None], n_rows, last)
        elif kind == 1:
            w_qkv = jnp.concatenate([_gqa_slab_pairs(gqa_wq[j].astype(BF16), GQA_KV_HEADS, GQA_GROUP),
                                     _gqa_slab_pairs(gqa_wk[j].astype(BF16), GQA_KV_HEADS, 1),
                                     gqa_wv[j].astype(BF16)], axis=-1)
            q_gain = _gqa_slab_pairs(jnp.tile(gqa_q_norm[j], GQA_KV_HEADS)[None], GQA_KV_HEADS, 1)
            k_gain = _gqa_slab_pairs(jnp.tile(gqa_k_norm[j], GQA_KV_HEADS)[None], GQA_KV_HEADS, 1)
            q_gain = q_gain * (GQA_HEAD_DIM ** -0.5 * LOG2E)
            bound = 1.02 * GQA_HEAD_DIM * jnp.max(jnp.abs(q_gain)) * jnp.max(jnp.abs(k_gain))
            q, k, vt = _gqa_proj(xs, mod, g1, w_qkv, q_gain, k_gain, half_ones, gqa_cos, gqa_sin)
            casts = [(w, i) for w in ffn_w] + [(w, N_CONV_LAYERS - 1) for w in conv_stacks]
            a_lat, a_ctx, *cast_out, mod_next = _attention(
                q, k, vt, jnp.full((BATCH,), bound), casts, (i + 1, 2), mod_operands,
                kv_heads=GQA_KV_HEADS, group=GQA_GROUP, dq=GQA_QK, dv=GQA_HEAD_DIM,
                heads_per_step=1, shared_k=True)
            ffn, conv_bf16[N_CONV_LAYERS - 1] = cast_out[:3], tuple(cast_out[3:])
            mods += list(mod_next.reshape(2, MOD_ROWS, N_MOD, D))
            xs = _attn_tail(a_lat, a_ctx, gqa_wo, j, xs, mod, g2, *ffn)
        else:
            w_dq, w_uq, w_dkv, w_ukv = _mla_weights(mla_w_dq[j], mla_w_uq[j], mla_w_dkv[j], mla_w_ukv[j])
            q_gain = mla_q_norm[j][None] * ((MLA_NOPE + MLA_ROPE) ** -0.5 * LOG2E)
            q, k, vt, qn, kn = _mla_proj(xs, mod, g1, w_dq, q_gain, w_uq, w_dkv,
                                         mla_kv_norm[j][None], w_ukv, pair_ones, mla_cos, mla_sin)
            a_lat, a_ctx, *ffn = _attention(
                q, k, vt, _score_bounds(qn, kn), [(w, i) for w in ffn_w], None, mod_operands,
                kv_heads=MLA_HEADS, group=1, dq=MLA_QK, dv=MLA_V, heads_per_step=4, shared_k=False)
            xs = _attn_tail(a_lat, a_ctx, mla_wo, j, xs, mod, g2, *ffn)

    assert (DEPTH - 1) % 3 == 0
    return xs.reshape(BATCH, SEQ, D)
```

```python
import functools

import jax
import jax.numpy as jnp
from jax import lax
from jax.experimental import pallas as pl
from jax.experimental.pallas import tpu as pltpu

D = 1024
BATCH = 8
SEQ = 2048
DEPTH = 4
GRID_W = 64
CTX_LEN = 256
N_MOD = 6
FFN_HIDDEN = 2816
ROPE_THETA = 10000.0
EPS = 1e-6
GQA_HEAD_DIM = 128
GQA_HEADS = 8
GQA_KV_HEADS = 2
GQA_GROUP = 4
MLA_HEADS = 8
MLA_NOPE = 128
MLA_ROPE = 64
MLA_V = 128
MLA_KV_RANK = 256
MLA_Q_RANK = 768
MLA_QK = 256
GQA_QK = 256
LANES = 128
SUBLANES = 8
VT_ROWS = 128

N_LAT = BATCH * SEQ
N_CTX = BATCH * CTX_LEN
N_ALL = N_LAT + N_CTX
MOD_ROWS = 16
N_CONV_LAYERS = (DEPTH + 2) // 3

TM = 512
TP = 1024
TPG = 2048
TQ = 1024
HALO = SUBLANES
CONV_COLS = 256
FFN_SPLITS = 2
CAST_CHUNKS = 16
MOD_COLS = 3072
VMEM_LIMIT = 56 * 1024 * 1024
LOG2E = 1.4426950408889634
MAX_BOUND_SHIFT = 50.0

F32 = jnp.float32
BF16 = jnp.bfloat16


def _params(n_axes):
    return pltpu.CompilerParams(
        dimension_semantics=("arbitrary",) * n_axes, vmem_limit_bytes=VMEM_LIMIT)


def _resident(shape, layer=None):
    nd = len(shape)
    if layer is None:
        return pl.BlockSpec(shape, lambda *_: (0,) * nd, pipeline_mode=pl.Buffered(1))
    return pl.BlockSpec((1,) + tuple(shape), lambda *_: (layer,) + (0,) * nd, pipeline_mode=pl.Buffered(1))


def _w(ref):
    return ref[0] if len(ref.shape) == 3 else ref[...]


def _cast_riders(stacks, chunk_of):
    in_specs, out_specs, out_shapes = [], [], []
    for w, layer in stacks:
        _, rows, cols = w.shape
        r = rows // CAST_CHUNKS
        in_specs.append(pl.BlockSpec((1, r, cols), lambda *g, layer=layer: (layer, chunk_of(*g), 0)))
        out_specs.append(pl.BlockSpec((r, cols), lambda *g: (chunk_of(*g), 0)))
        out_shapes.append(jax.ShapeDtypeStruct((rows, cols), BF16))
    return in_specs, out_specs, out_shapes


def _mod_rider(first_layer, n_layers, chunk_of):
    tn = N_MOD * D // CAST_CHUNKS
    assert first_layer % n_layers == 0
    blk = first_layer // n_layers
    in_specs = [pl.BlockSpec((MOD_ROWS, D), lambda *g: (0, 0)),
                pl.BlockSpec((n_layers, D, tn), lambda *g: (blk, 0, chunk_of(*g))),
                pl.BlockSpec((n_layers, 1, tn), lambda *g: (blk, 0, chunk_of(*g)))]
    out_spec = pl.BlockSpec((n_layers, MOD_ROWS, tn), lambda *g: (0, 0, chunk_of(*g)))
    return in_specs, out_spec, jax.ShapeDtypeStruct((n_layers, MOD_ROWS, N_MOD * D), F32)


def _with_riders(body, n_in, n_out, mod_rider=False):
    n_mod_in, n_mod_out = (3, 1) if mod_rider else (0, 0)

    def kern(*refs):
        n_casts = (len(refs) - n_in - n_out - n_mod_in - n_mod_out) // 2
        ins, cast_in = refs[:n_in], refs[n_in:n_in + n_casts]
        mod_in = refs[n_in + n_casts:n_in + n_casts + n_mod_in]
        o0 = n_in + n_casts + n_mod_in
        outs, cast_out = refs[o0:o0 + n_out], refs[o0 + n_out:o0 + n_out + n_casts]
        mod_out = refs[o0 + n_out + n_casts:]
        body(*ins, *outs)
        for src, dst in zip(cast_in, cast_out):
            dst[...] = src[0].astype(dst.dtype)
        if mod_rider:
            _mod_kernel(*mod_in, *mod_out)
    return kern


def _mod_spec(tm):
    return pl.BlockSpec((1, N_MOD, D), lambda i: (jnp.minimum(i * tm // SEQ, BATCH), 0, 0))


def _row_spec(tm, width):
    return pl.BlockSpec((tm, width), lambda i: (i, 0))


def _col_spec(height, tm):
    return pl.BlockSpec((height, tm), lambda i: (0, i))


def _dot(a, b):
    return jnp.dot(a, b, preferred_element_type=F32)


def _modulated_norm(x, gain, shift, scale):
    r = lax.rsqrt(jnp.mean(x * x, axis=-1, keepdims=True) + EPS)
    return (x * r) * (gain * (1.0 + scale)) + shift


def _head_norm(x, gain):
    r = lax.rsqrt(jnp.mean(x * x, axis=-1, keepdims=True) + EPS)
    return (x * r) * gain


def _store_vt(vt_ref, v, n_heads, tokens=slice(None)):
    for j in range(n_heads):
        vt_ref[j * VT_ROWS:(j + 1) * VT_ROWS, tokens] = v[:, j * VT_ROWS:(j + 1) * VT_ROWS].T.astype(BF16)


def _rope(a, b, cos, sin):
    return a * cos - b * sin, a * sin + b * cos


def _lane_mask(lo, hi):
    lane = lax.broadcasted_iota(jnp.int32, (1, LANES), 1)
    return (lane >= lo) & (lane < hi)


def _mod_kernel(cond_ref, w_ref, b_ref, o_ref):
    c = cond_ref[...]
    act = (c * jax.nn.sigmoid(c)).astype(BF16)
    for l in range(w_ref.shape[0]):
        o_ref[l] = _dot(act, w_ref[l].astype(BF16)) + b_ref[l]


def _modulation_tables(cond, ada_w, ada_b, n_layers):
    tn = MOD_COLS
    return pl.pallas_call(
        _mod_kernel,
        grid=(n_layers, N_MOD * D // tn),
        in_specs=[
            pl.BlockSpec((MOD_ROWS, D), lambda l, j: (0, 0)),
            pl.BlockSpec((1, D, tn), lambda l, j: (l, 0, j)),
            pl.BlockSpec((1, 1, tn), lambda l, j: (l, 0, j)),
        ],
        out_specs=pl.BlockSpec((1, MOD_ROWS, tn), lambda l, j: (l, 0, j)),
        out_shape=jax.ShapeDtypeStruct((n_layers, MOD_ROWS, N_MOD * D), F32),
        compiler_params=_params(2),
        name="modulation",
    )(cond, ada_w, ada_b)


def _conv_in_compute(x, mod_ref, g_ref, w_ref, b_ref, u_ref):
    m = mod_ref[0]
    h = _modulated_norm(x, g_ref[...], m[0:1], m[1:2]).astype(BF16)
    y = _dot(h, _w(w_ref))
    b_ref[...] = y[:, :D].astype(BF16)
    u_ref[...] = y[:, D:2 * D] * y[:, 2 * D:]


def _conv_in_kernel(x_ref, mod_ref, g_ref, w_ref, b_ref, u_ref):
    _conv_in_compute(x_ref[...], mod_ref, g_ref, w_ref, b_ref, u_ref)


def _conv_in_first_kernel(lat_ref, ctx_ref, mod_ref, g_ref, w_ref, b_ref, u_ref, xs_ref):
    x = jnp.where(pl.program_id(0) < N_LAT // TP, lat_ref[...], ctx_ref[...])
    xs_ref[...] = x
    _conv_in_compute(x, mod_ref, g_ref, w_ref, b_ref, u_ref)


def _conv_in(x, mod, gain, w_in, layer, n_rows, ffn_w, ffn_layer):
    r_in, r_out, r_shape = _cast_riders([(w, ffn_layer) for w in ffn_w], lambda i: jnp.minimum(i, CAST_CHUNKS - 1))
    assert n_rows // TP >= CAST_CHUNKS
    return pl.pallas_call(
        _with_riders(_conv_in_kernel, 4, 2),
        grid=(n_rows // TP,),
        in_specs=[_row_spec(TP, D), _mod_spec(TP), _resident((1, D)), _resident((D, 3 * D), layer)] + r_in,
        out_specs=[_row_spec(TP, D), _row_spec(TP, D)] + r_out,
        out_shape=[jax.ShapeDtypeStruct((n_rows, D), BF16), jax.ShapeDtypeStruct((n_rows, D), F32)] + r_shape,
        compiler_params=_params(1),
        name="conv_in",
    )(x, mod, gain, w_in, *ffn_w)


def _conv_in_first(lat, ctx, mod, gain, w_in, layer, ffn_w, ffn_layer):
    lat_tiles = N_LAT // TP
    r_in, r_out, r_shape = _cast_riders([(w, ffn_layer) for w in ffn_w], lambda i: jnp.minimum(i, CAST_CHUNKS - 1))
    assert N_ALL // TP >= CAST_CHUNKS
    return pl.pallas_call(
        _with_riders(_conv_in_first_kernel, 5, 3),
        grid=(N_ALL // TP,),
        in_specs=[
            pl.BlockSpec((TP, D), lambda i: (jnp.minimum(i, lat_tiles - 1), 0)),
            pl.BlockSpec((TP, D), lambda i: (jnp.maximum(i - lat_tiles, 0), 0)),
            _mod_spec(TP), _resident((1, D)), _resident((D, 3 * D), layer),
        ] + r_in,
        out_specs=[_row_spec(TP, D), _row_spec(TP, D), _row_spec(TP, D)] + r_out,
        out_shape=[jax.ShapeDtypeStruct((N_ALL, D), BF16), jax.ShapeDtypeStruct((N_ALL, D), F32),
                   jax.ShapeDtypeStruct((N_ALL, D), F32)] + r_shape,
        compiler_params=_params(1),
        name="conv_in_first",
    )(lat, ctx, mod, gain, w_in, *ffn_w)


def _conv_gate_exact(b_ref, u_ref, up_ref, un_ref, cw_ref, r0, r1, cols):
    tm = u_ref.shape[0]
    n = r1 - r0
    u = u_ref[r0:r1, cols]
    before = up_ref[HALO - 1:HALO, cols] if r0 == 0 else u_ref[r0 - 1:r0, cols]
    after = un_ref[0:1, cols] if r1 == tm else u_ref[r1:r1 + 1, cols]
    local = lax.broadcasted_iota(jnp.int32, (n, 1), 0)
    row = local + (pl.program_id(0) * tm + r0)
    in_ctx = row >= N_LAT
    first = ((row & (CTX_LEN - 1)) == 0) & (in_ctx | ((row & (SEQ - 1)) == 0))
    last = ((row & (CTX_LEN - 1)) == CTX_LEN - 1) & (in_ctx | ((row & (SEQ - 1)) == SEQ - 1))
    prev = jnp.where(local == 0, before, pltpu.roll(u, 1, axis=0))
    prev = jnp.where(first, 0.0, prev)
    nxt = jnp.where(local == n - 1, after, pltpu.roll(u, n - 1, axis=0))
    nxt = jnp.where(last, 0.0, nxt)
    cw = cw_ref[:, cols]
    z = prev * cw[0:1] + u * cw[1:2] + nxt * cw[2:3]
    return (b_ref[r0:r1, cols].astype(F32) * z).astype(BF16)


def _conv_mix(b_ref, u_ref, up_ref, un_ref, cw_ref, wo_ref, bz_ref):
    tm = u_ref.shape[0]
    group = 2 * HALO
    edges = [(0, group), (tm - group, tm)]
    edges += [(r - group, r + group) for r in range(CTX_LEN, tm, CTX_LEN)]
    wo = wo_ref.at[0] if len(wo_ref.shape) == 3 else wo_ref
    acc = None
    for c0 in range(0, D, CONV_COLS):
        cols = slice(c0, c0 + CONV_COLS)
        u = u_ref[:, cols]
        cw = cw_ref[:, cols]
        z = pltpu.roll(u, 1, axis=0) * cw[0:1] + u * cw[1:2] + pltpu.roll(u, tm - 1, axis=0) * cw[2:3]
        bz_ref[:, cols] = (b_ref[:, cols].astype(F32) * z).astype(BF16)
        for r0, r1 in edges:
            bz_ref[r0:r1, cols] = _conv_gate_exact(b_ref, u_ref, up_ref, un_ref, cw_ref, r0, r1, cols)
        part = _dot(bz_ref[:, cols], wo[cols, :])
        acc = part if acc is None else acc + part
    return acc


def _gqa_proj_kernel(x_ref, mod_ref, g_ref, w_ref, qg_ref, kg_ref, ones_ref, cos_ref, sin_ref,
                     q_ref, k_ref, vt_ref):
    m = mod_ref[0]
    ones2 = ones_ref[...]
    wide = 2 * GQA_QK

    def project(rows):
        h = _modulated_norm(x_ref[rows, :], g_ref[...], m[0:1], m[1:2]).astype(BF16)
        cos = cos_ref[rows, :]
        sin = sin_ref[rows, :]

        def normed_rotated(y, gain):
            a, b = y[:, :LANES], y[:, LANES:]
            ss = a * a + b * b
            hi = ss.astype(BF16)
            lo = (ss - hi.astype(F32)).astype(BF16)
            tot = _dot(jnp.concatenate([hi, lo], axis=1), ones2)
            r = lax.rsqrt(tot * (1.0 / GQA_HEAD_DIM) + EPS)
            return _rope(a * r * gain[:, :LANES], b * r * gain[:, LANES:], cos, sin)

        for gg in range(GQA_GROUP // 2):
            y = _dot(h, w_ref[:, gg * wide:(gg + 1) * wide])
            for t in range(2):
                g = 2 * gg + t
                o1, o2 = normed_rotated(y[:, t * GQA_QK:(t + 1) * GQA_QK], qg_ref[...])
                q_ref[rows, g * GQA_QK:g * GQA_QK + LANES] = o1.astype(BF16)
                q_ref[rows, g * GQA_QK + LANES:(g + 1) * GQA_QK] = o2.astype(BF16)
        y = _dot(h, w_ref[:, GQA_GROUP * GQA_QK:])
        o1, o2 = normed_rotated(y[:, :GQA_QK], kg_ref[...])
        k_ref[rows, :LANES] = o1.astype(BF16)
        k_ref[rows, LANES:] = o2.astype(BF16)
        _store_vt(vt_ref, y[:, GQA_QK:], GQA_KV_HEADS, rows)

    rows = x_ref.shape[0] // 2
    project(slice(0, rows))
    project(slice(rows, 2 * rows))


def _rope_spec(tm):
    per_seq = SEQ // tm
    return pl.BlockSpec(
        (tm, LANES), lambda i: (jnp.where(i < N_LAT // tm, i % per_seq, per_seq), 0))


def _gqa_proj(x, mod, gain, w_qkv, q_gain, k_gain, half_ones, cos, sin):
    kvw = GQA_KV_HEADS * GQA_HEAD_DIM
    qw = GQA_GROUP * GQA_QK
    return pl.pallas_call(
        _gqa_proj_kernel,
        grid=(N_ALL // TPG,),
        in_specs=[
            _row_spec(TPG, D), _mod_spec(TPG), _resident((1, D)),
            _resident((D, D + 2 * kvw)),
            _resident((1, GQA_QK)), _resident((1, GQA_QK)), _resident((2 * LANES, LANES)),
            _rope_spec(TPG), _rope_spec(TPG),
        ],
        out_specs=[_row_spec(TPG, qw), _row_spec(TPG, GQA_QK), _col_spec(GQA_KV_HEADS * VT_ROWS, TPG)],
        out_shape=[jax.ShapeDtypeStruct((N_ALL, qw), BF16),
                   jax.ShapeDtypeStruct((N_ALL, GQA_QK), BF16),
                   jax.ShapeDtypeStruct((GQA_KV_HEADS * VT_ROWS, N_ALL), BF16)],
        compiler_params=_params(1),
        name="gqa_proj",
    )(x, mod, gain, w_qkv, q_gain, k_gain, half_ones, cos, sin)


def _mla_proj_kernel(x_ref, mod_ref, g_ref, wdq_ref, qg_ref, wuq_ref, wdkv_ref, kvg_ref, wukv_ref,
                     ones_ref, cos_ref, sin_ref, q_ref, k_ref, vt_ref, qn_ref, kn_ref):
    m = mod_ref[0]
    half = MLA_ROPE // 2
    ones2 = ones_ref[...]

    def max_sq_norm(sq_pair):
        sums = _nt_dot(ones2, jnp.concatenate(sq_pair, axis=1).astype(BF16))
        top = jnp.max(sums, axis=1, keepdims=True)
        lane = lax.broadcasted_iota(jnp.int32, (1, 2 * LANES), 1)
        return jnp.where(lane < LANES, top[0:1], top[1:2])

    def project(rows):
        h = _modulated_norm(x_ref[rows, :], g_ref[...], m[0:1], m[1:2]).astype(BF16)
        cos = cos_ref[rows, :]
        sin = sin_ref[rows, :]
        cq = _head_norm(_dot(h, wdq_ref[...]), qg_ref[...]).astype(BF16)
        q = _dot(cq, wuq_ref[...])
        ckv_pe = _dot(h, wdkv_ref[...])
        ckv = _head_norm(ckv_pe[:, :MLA_KV_RANK], kvg_ref[...]).astype(BF16)
        ko1, ko2 = _rope(ckv_pe[:, MLA_KV_RANK:MLA_KV_RANK + LANES], ckv_pe[:, MLA_KV_RANK + LANES:], cos, sin)
        k_pe = jnp.where(_lane_mask(0, 2 * half), ko1, ko2)
        k_pe_sq = k_pe * k_pe
        k_pe = k_pe.astype(BF16)
        kv = _dot(ckv, wukv_ref[...])
        q_norms, k_norms = [], []
        for pp in range(MLA_HEADS // 2):
            p0 = pp * 2 * MLA_QK
            o1, o2 = _rope(q[:, p0 + 2 * MLA_NOPE:p0 + 2 * MLA_NOPE + LANES],
                           q[:, p0 + 2 * MLA_NOPE + LANES:p0 + 2 * MLA_QK], cos, sin)
            q_sq, k_sq = [], []
            for t in range(2):
                j = 2 * pp + t
                c0 = j * MLA_QK
                pe = jnp.where(_lane_mask(t * half, (t + 1) * half), o1,
                               jnp.where(_lane_mask(2 * half + t * half, 2 * half + (t + 1) * half), o2, 0.0))
                q_nope = q[:, p0 + t * MLA_NOPE:p0 + (t + 1) * MLA_NOPE]
                k_nope = kv[:, j * MLA_NOPE:(j + 1) * MLA_NOPE]
                q_ref[rows, c0:c0 + MLA_NOPE] = q_nope.astype(BF16)
                q_ref[rows, c0 + MLA_NOPE:c0 + MLA_QK] = pe.astype(BF16)
                k_ref[rows, c0:c0 + MLA_NOPE] = k_nope.astype(BF16)
                k_ref[rows, c0 + MLA_NOPE:c0 + MLA_QK] = k_pe
                q_sq.append(q_nope * q_nope + pe * pe)
                k_sq.append(k_nope * k_nope + k_pe_sq)
            q_norms.append(max_sq_norm(q_sq))
            k_norms.append(max_sq_norm(k_sq))
        _store_vt(vt_ref, kv[:, MLA_HEADS * MLA_NOPE:], MLA_HEADS, rows)
        return jnp.concatenate(q_norms, axis=1), jnp.concatenate(k_norms, axis=1)

    rows = x_ref.shape[0] // 2
    (qa, ka), (qb, kb) = project(slice(0, rows)), project(slice(rows, 2 * rows))
    qn_ref[0] = jnp.broadcast_to(jnp.maximum(qa, qb), qn_ref.shape[1:])
    kn_ref[0] = jnp.broadcast_to(jnp.maximum(ka, kb), kn_ref.shape[1:])


def _mla_proj(x, mod, gain, w_dq, q_gain, w_uq, w_dkv, kv_gain, w_ukv, pair_ones, cos, sin):
    qkw = MLA_HEADS * MLA_QK
    n_tiles = N_ALL // TP
    norm_spec = pl.BlockSpec((1, SUBLANES, MLA_HEADS * LANES), lambda i: (i, 0, 0))
    norm_shape = jax.ShapeDtypeStruct((n_tiles, SUBLANES, MLA_HEADS * LANES), F32)
    return pl.pallas_call(
        _mla_proj_kernel,
        grid=(N_ALL // TP,),
        in_specs=[
            _row_spec(TP, D), _mod_spec(TP), _resident((1, D)),
            _resident((D, MLA_Q_RANK)), _resident((1, MLA_Q_RANK)), _resident((MLA_Q_RANK, qkw)),
            _resident((D, MLA_KV_RANK + 2 * LANES)), _resident((1, MLA_KV_RANK)),
            _resident((MLA_KV_RANK, MLA_HEADS * (MLA_NOPE + MLA_V))),
            _resident((SUBLANES, 2 * LANES)),
            _rope_spec(TP), _rope_spec(TP),
        ],
        out_specs=[_row_spec(TP, qkw), _row_spec(TP, qkw), _col_spec(MLA_HEADS * VT_ROWS, TP),
                   norm_spec, norm_spec],
        out_shape=[jax.ShapeDtypeStruct((N_ALL, qkw), BF16),
                   jax.ShapeDtypeStruct((N_ALL, qkw), BF16),
                   jax.ShapeDtypeStruct((MLA_HEADS * VT_ROWS, N_ALL), BF16),
                   norm_shape, norm_shape],
        compiler_params=_params(1),
        name="mla_proj",
    )(x, mod, gain, w_dq, q_gain, w_uq, w_dkv, kv_gain, w_ukv, pair_ones, cos, sin)


def _score_bounds(qn, kn):
    qn = qn[:, 0, ::LANES]
    kn = kn[:, 0, ::LANES]
    per_seq = SEQ // TP
    q_lat = jnp.max(qn[:N_LAT // TP].reshape(BATCH, per_seq, -1), axis=1)
    k_lat = jnp.max(kn[:N_LAT // TP].reshape(BATCH, per_seq, -1), axis=1)
    ctx_tile = N_LAT // TP + (jnp.arange(BATCH) * CTX_LEN) // TP
    k_all = jnp.maximum(k_lat, kn[ctx_tile])
    return 1.02 * jnp.sqrt(jnp.max(q_lat * k_all, axis=1))


def _nt_dot(a, b):
    return lax.dot_general(a, b, (((1,), (1,)), ((), ())), preferred_element_type=F32)


def _attn_scores(q, k_parts):
    return [_nt_dot(k, q) for k in k_parts]


def _attn_kernel(bound_ref, q_ref, qc_ref, kl_ref, kc_ref, vtl_ref, vtc_ref, o_ref, oc_ref,
                 *, heads, group, dq, dv, shared_k):
    n = heads * group

    def k_cols(qh):
        kv = 0 if shared_k else qh // group
        return slice(kv * dq, (kv + 1) * dq)

    def v_rows(qh):
        return slice(qh // group * VT_ROWS, (qh // group + 1) * VT_ROWS)

    def run(queries, out_ref, k_refs, vt_refs, shift=None):
        def scores(qh):
            if shared_k:
                g = qh % group
                kv = pl.program_id(1) * heads + qh // group
                q = queries[:, g * dq:(g + 1) * dq]
                lane = lax.broadcasted_iota(jnp.int32, (1, dq), 1)
                q = jnp.where(((lane // (LANES // 2)) & 1) == kv, q, jnp.zeros_like(q))
            else:
                q = queries[:, qh * dq:(qh + 1) * dq]
            return _attn_scores(q, [k[:, k_cols(qh)] for k in k_refs])

        ahead = 2 if shift is None else 1
        pending = [scores(qh) for qh in range(min(ahead, n))]
        for qh in range(n):
            cur = pending.pop(0)
            if shift is None:
                mx = functools.reduce(jnp.maximum, [jnp.max(s, axis=0, keepdims=True) for s in cur])
            else:
                mx = shift
            probs = [jnp.exp2(s - mx) for s in cur]
            den = sum(jnp.sum(p, axis=0, keepdims=True) for p in probs)
            probs = [p.astype(BF16) for p in probs]
            if qh + ahead < n:
                pending.append(scores(qh + ahead))
            acc = sum(_dot(vt[v_rows(qh), :], p) for vt, p in zip(vt_refs, probs))
            out_ref[:, qh * dv:(qh + 1) * dv] = (acc / den).T.astype(BF16)

    lat = (q_ref, o_ref, [kl_ref, kc_ref], [vtl_ref, vtc_ref])
    bound = bound_ref[0, pl.program_id(0)]

    @pl.when(bound <= MAX_BOUND_SHIFT)
    def _():
        run(*lat, shift=bound)

    @pl.when(jnp.logical_not(bound <= MAX_BOUND_SHIFT))
    def _():
        run(*lat)

    @pl.when(pl.program_id(2) == 0)
    def _():
        run(qc_ref, oc_ref, [kc_ref], [vtc_ref])


def _attention(q, k, vt, score_bounds, casts, mod_layers, mod_operands, *, kv_heads, group, dq, dv,
               heads_per_step, shared_k):
    hp = heads_per_step
    assert hp == 1 or not shared_k
    lat_tiles = SEQ // TQ
    ctx_block0 = N_LAT // CTX_LEN
    kw = dq if shared_k else hp * dq
    k_col = (lambda h: 0) if shared_k else (lambda h: h)
    width = kv_heads * group * dv
    h_steps = kv_heads // hp
    n_steps = BATCH * h_steps * lat_tiles
    assert n_steps >= CAST_CHUNKS
    def chunk_of(b, h, t):
        return ((b * h_steps + h) * lat_tiles + t) * CAST_CHUNKS // n_steps

    r_in, r_out, r_shape = _cast_riders(casts, chunk_of)
    rider_operands = [w for w, _ in casts]
    if mod_layers is not None:
        m_in, m_out, m_shape = _mod_rider(*mod_layers, chunk_of)
        r_in, r_out, r_shape = r_in + m_in, r_out + [m_out], r_shape + [m_shape]
        rider_operands += list(mod_operands)
    body = functools.partial(_attn_kernel, heads=hp, group=group, dq=dq, dv=dv, shared_k=shared_k)
    return pl.pallas_call(
        _with_riders(body, 7, 2, mod_rider=mod_layers is not None),
        grid=(BATCH, h_steps, lat_tiles),
        in_specs=[
            pl.BlockSpec(memory_space=pltpu.SMEM),
            pl.BlockSpec((TQ, hp * group * dq), lambda b, h, t: (b * lat_tiles + t, k_col(h))),
            pl.BlockSpec((CTX_LEN, hp * group * dq), lambda b, h, t: (ctx_block0 + b, k_col(h))),
            pl.BlockSpec((SEQ, kw), lambda b, h, t: (b, k_col(h))),
            pl.BlockSpec((CTX_LEN, kw), lambda b, h, t: (ctx_block0 + b, k_col(h))),
            pl.BlockSpec((hp * VT_ROWS, SEQ), lambda b, h, t: (h, b)),
            pl.BlockSpec((hp * VT_ROWS, CTX_LEN), lambda b, h, t: (h, ctx_block0 + b)),
        ] + r_in,
        out_specs=[pl.BlockSpec((TQ, hp * group * dv), lambda b, h, t: (b * lat_tiles + t, h)),
                   pl.BlockSpec((CTX_LEN, hp * group * dv), lambda b, h, t: (b, h))] + r_out,
        out_shape=[jax.ShapeDtypeStruct((N_LAT, width), BF16), jax.ShapeDtypeStruct((N_CTX, width), BF16)] + r_shape,
        compiler_params=_params(3),
        name="attention",
    )(score_bounds.reshape(1, BATCH).astype(F32), q, q, k, k, vt, vt, *rider_operands)


def _ffn_tail(x1, m, g2_ref, w1_ref, w3_ref, w2_ref):
    rows = x1.shape[0] // FFN_SPLITS
    parts = [x1[i * rows:(i + 1) * rows] for i in range(FFN_SPLITS)]

    def up(x):
        h = _modulated_norm(x, g2_ref[...], m[3:4], m[4:5]).astype(BF16)
        return _dot(h, _w(w1_ref)), _dot(h, _w(w3_ref))

    def down(x, ab):
        a, b = ab
        gated = (a * jax.nn.sigmoid(a) * b).astype(BF16)
        return x + m[5:6] * _dot(gated, _w(w2_ref))

    outs = []
    nxt = up(parts[0])
    for i, x in enumerate(parts):
        cur = nxt
        if i + 1 < FFN_SPLITS:
            nxt = up(parts[i + 1])
        outs.append(down(x, cur))
    return jnp.concatenate(outs, axis=0)


def _attn_tail_kernel(al_ref, ac_ref, wo_ref, x_ref, mod_ref, g2_ref, w1_ref, w3_ref, w2_ref, o_ref):
    m = mod_ref[0]
    a = jnp.where(pl.program_id(0) < N_LAT // TM, al_ref[...], ac_ref[...])
    x1 = x_ref[...] + m[2:3] * _dot(a, _w(wo_ref))
    o_ref[...] = _ffn_tail(x1, m, g2_ref, w1_ref, w3_ref, w2_ref)


def _conv_tail_kernel(b_ref, u_ref, up_ref, un_ref, cw_ref, wo_ref, x_ref, mod_ref,
                      g2_ref, w1_ref, w3_ref, w2_ref, fg_ref, o_ref, bz_ref, *, final):
    m = mod_ref[0]
    x1 = x_ref[...] + m[2:3] * _conv_mix(b_ref, u_ref, up_ref, un_ref, cw_ref, wo_ref, bz_ref)
    x2 = _ffn_tail(x1, m, g2_ref, w1_ref, w3_ref, w2_ref)
    o_ref[...] = _head_norm(x2, fg_ref[...]) if final else x2


def _ffn_specs():
    return [_mod_spec(TM), _resident((1, D)), _resident((D, FFN_HIDDEN)),
            _resident((D, FFN_HIDDEN)), _resident((FFN_HIDDEN, D))]


def _attn_tail(a_lat, a_ctx, w_o, mixer_layer, x, mod, g2, w1, w3, w2):
    lat_tiles = N_LAT // TM
    return pl.pallas_call(
        _attn_tail_kernel,
        grid=(N_ALL // TM,),
        in_specs=[
            pl.BlockSpec((TM, D), lambda i: (jnp.minimum(i, lat_tiles - 1), 0)),
            pl.BlockSpec((TM, D), lambda i: (jnp.maximum(i - lat_tiles, 0), 0)),
            _resident((D, D), mixer_layer), _row_spec(TM, D),
        ] + _ffn_specs(),
        out_specs=_row_spec(TM, D),
        out_shape=jax.ShapeDtypeStruct((N_ALL, D), F32),
        compiler_params=_params(1),
        name="attn_tail",
    )(a_lat, a_ctx, w_o, x, mod, g2, w1, w3, w2)


def _conv_tail(b, u, conv_w, w_out, mixer_layer, x, mod, g2, w1, w3, w2, final_gain, n_rows, final):
    per = TM // HALO
    n_halo = n_rows // HALO
    return pl.pallas_call(
        functools.partial(_conv_tail_kernel, final=final),
        grid=(n_rows // TM,),
        in_specs=[
            _row_spec(TM, D), _row_spec(TM, D),
            pl.BlockSpec((HALO, D), lambda i: (jnp.maximum(i * per - 1, 0), 0)),
            pl.BlockSpec((HALO, D), lambda i: (jnp.minimum((i + 1) * per, n_halo - 1), 0)),
            _resident((3, D)), _resident((D, D), mixer_layer), _row_spec(TM, D),
        ] + _ffn_specs() + [_resident((1, D))],
        out_specs=_row_spec(TM, D),
        out_shape=jax.ShapeDtypeStruct((n_rows, D), F32),
        scratch_shapes=[pltpu.VMEM((TM, D), BF16)],
        compiler_params=_params(1),
        name="conv_tail",
    )(b, u, u, u, conv_w, w_out, x, mod, g2, w1, w3, w2, final_gain)


def _axial_angles(rot_dim):
    n = rot_dim // 4
    rows = jnp.repeat(jnp.arange(SEQ // GRID_W, dtype=F32), GRID_W)
    cols = jnp.tile(jnp.arange(GRID_W, dtype=F32), SEQ // GRID_W)
    freqs = ROPE_THETA ** (-jnp.arange(n, dtype=F32) / n)
    return jnp.concatenate([rows[:, None] * freqs, cols[:, None] * freqs], axis=-1)


def _rope_tables(rot_dim, tm):
    ang = _axial_angles(rot_dim)
    copies = LANES // (rot_dim // 2)
    cos_t = jnp.concatenate([jnp.tile(jnp.cos(ang), (1, copies)), jnp.ones((tm, LANES), F32)], axis=0)
    sin_t = jnp.concatenate([jnp.tile(jnp.sin(ang), (1, copies)), jnp.zeros((tm, LANES), F32)], axis=0)
    return cos_t, sin_t


def _gqa_slab_pairs(w, n_kv, n_group):
    half = GQA_HEAD_DIM // 2
    w = w.reshape(w.shape[0], n_kv, n_group, 2, half)
    return jnp.transpose(w, (0, 2, 3, 1, 4)).reshape(w.shape[0], n_group * GQA_QK)


def _mla_weights(w_dq, w_uq, w_dkv, w_ukv):
    half = MLA_ROPE // 2
    qk = MLA_NOPE + MLA_ROPE
    w_dq, w_uq, w_dkv, w_ukv = (w.astype(BF16) for w in (w_dq, w_uq, w_dkv, w_ukv))
    uq = w_uq.reshape(MLA_Q_RANK, MLA_HEADS // 2, 2, qk)
    nope = uq[..., :MLA_NOPE].reshape(MLA_Q_RANK, MLA_HEADS // 2, 2 * MLA_NOPE)
    x1 = uq[..., MLA_NOPE:MLA_NOPE + half].reshape(MLA_Q_RANK, MLA_HEADS // 2, 2 * half)
    x2 = uq[..., MLA_NOPE + half:].reshape(MLA_Q_RANK, MLA_HEADS // 2, 2 * half)
    uq = jnp.concatenate([nope, x1, x1, x2, x2], axis=-1).reshape(MLA_Q_RANK, MLA_HEADS * MLA_QK)
    k1 = w_dkv[:, MLA_KV_RANK:MLA_KV_RANK + half]
    k2 = w_dkv[:, MLA_KV_RANK + half:]
    dkv = jnp.concatenate([w_dkv[:, :MLA_KV_RANK]] + [k1] * 4 + [k2] * 4, axis=-1)
    ukv = w_ukv.reshape(MLA_KV_RANK, MLA_HEADS, MLA_NOPE + MLA_V)
    ukv = jnp.concatenate([ukv[:, :, :MLA_NOPE].reshape(MLA_KV_RANK, -1),
                           ukv[:, :, MLA_NOPE:].reshape(MLA_KV_RANK, -1)], axis=-1)
    return w_dq, uq, dkv, ukv


def kernel(x, c, ctx, c_ctx, ada_w, ada_b, norm1_g, norm2_g, ffn_w1, ffn_w3, ffn_w2, conv_w_in, conv_w, conv_w_out, gqa_wq, gqa_wk, gqa_wv, gqa_q_norm, gqa_k_norm, gqa_wo, mla_w_dq, mla_q_norm, mla_w_uq, mla_w_dkv, mla_kv_norm, mla_w_ukv, mla_wo, final_g):
    assert x.shape == (BATCH, SEQ, D) and ctx.shape == (BATCH, CTX_LEN, D)
    cond = jnp.concatenate(
        [c, c_ctx[None], jnp.zeros((MOD_ROWS - BATCH - 1, D), F32)], axis=0)
    mod_operands = (cond, ada_w, ada_b.reshape(DEPTH, 1, N_MOD * D))
    mods = list(_modulation_tables(*mod_operands, 2).reshape(2, MOD_ROWS, N_MOD, D))

    gqa_cos, gqa_sin = _rope_tables(GQA_HEAD_DIM, TPG)
    lane_half = jnp.arange(LANES) // (LANES // 2)
    half_ones = jnp.tile((lane_half[:, None] == lane_half[None, :]).astype(BF16), (2, 1))
    lane_head = jnp.arange(2 * LANES) // LANES
    pair_ones = (jnp.arange(SUBLANES)[:, None] == lane_head[None, :]).astype(BF16)
    mla_cos, mla_sin = _rope_tables(MLA_ROPE, TP)

    ffn_w = (ffn_w1, ffn_w3, ffn_w2)
    conv_stacks = (conv_w_in, conv_w_out)
    conv_bf16 = {0: (conv_w_in[0].astype(BF16), conv_w_out[0].astype(BF16))}
    gqa_wo = gqa_wo.astype(BF16)
    mla_wo = mla_wo.astype(BF16)

    xs = None
    for i in range(DEPTH):
        kind, j = i % 3, i // 3
        mod = mods[i]
        last = i == DEPTH - 1
        n_rows = N_LAT if last else N_ALL
        g1 = norm1_g[i][None]
        g2 = norm2_g[i][None]
        if kind == 0:
            w_in, w_out = conv_bf16[j]
            if i == 0:
                b, u, xs, *ffn = _conv_in_first(x.reshape(N_LAT, D), ctx.reshape(N_CTX, D), mod, g1,
                                                w_in, None, ffn_w, i)
            else:
                b, u, *ffn = _conv_in(xs, mod, g1, w_in, None, n_rows, ffn_w, i)
            xs = _conv_tail(b, u, conv_w[j], w_out, None, xs, mod, g2, *ffn, final_g[None], n_rows, last)
        elif kind == 1:
            w_qkv = jnp.concatenate([_gqa_slab_pairs(gqa_wq[j].astype(BF16), GQA_KV_HEADS, GQA_GROUP),
                                     _gqa_slab_pairs(gqa_wk[j].astype(BF16), GQA_KV_HEADS, 1),
                                     gqa_wv[j].astype(BF16)], axis=-1)
            q_gain = _gqa_slab_pairs(jnp.tile(gqa_q_norm[j], GQA_KV_HEADS)[None], GQA_KV_HEADS, 1)
            k_gain = _gqa_slab_pairs(jnp.tile(gqa_k_norm[j], GQA_KV_HEADS)[None], GQA_KV_HEADS, 1)
            q_gain = q_gain * (GQA_HEAD_DIM ** -0.5 * LOG2E)
            bound = 1.02 * GQA_HEAD_DIM * jnp.max(jnp.abs(q_gain)) * jnp.max(jnp.abs(k_gain))
            q, k, vt = _gqa_proj(xs, mod, g1, w_qkv, q_gain, k_gain, half_ones, gqa_cos, gqa_sin)
            casts = [(w, i) for w in ffn_w] + [(w, N_CONV_LAYERS - 1) for w in conv_stacks]
            a_lat, a_ctx, *cast_out, mod_next = _attention(
                q, k, vt, jnp.full((BATCH,), bound), casts, (i + 1, 2), mod_operands,
                kv_heads=GQA_KV_HEADS, group=GQA_GROUP, dq=GQA_QK, dv=GQA_HEAD_DIM,
                heads_per_step=1, shared_k=True)
            ffn, conv_bf16[N_CONV_LAYERS - 1] = cast_out[:3], tuple(cast_out[3:])
            mods += list(mod_next.reshape(2, MOD_ROWS, N_MOD, D))
            xs = _attn_tail(a_lat, a_ctx, gqa_wo, j, xs, mod, g2, *ffn)
        else:
            w_dq, w_uq, w_dkv, w_ukv = _mla_weights(mla_w_dq[j], mla_w_uq[j], mla_w_dkv[j], mla_w_ukv[j])
            q_gain = mla_q_norm[j][None] * ((MLA_NOPE + MLA_ROPE) ** -0.5 * LOG2E)
            q, k, vt, qn, kn = _mla_proj(xs, mod, g1, w_dq, q_gain, w_uq, w_dkv,
                                         mla_kv_norm[j][None], w_ukv, pair_ones, mla_cos, mla_sin)
            a_lat, a_ctx, *ffn = _attention(
                q, k, vt, _score_bounds(qn, kn), [(w, i) for w in ffn_w], None, mod_operands,
                kv_heads=MLA_HEADS, group=1, dq=MLA_QK, dv=MLA_V, heads_per_step=4, shared_k=False)
            xs = _attn_tail(a_lat, a_ctx, mla_wo, j, xs, mod, g2, *ffn)

    assert (DEPTH - 1) % 3 == 0
    return xs.reshape(BATCH, SEQ, D)
```
